```python
import jax, jax.numpy as jnp
from jax import lax
import numpy as np

D_MODEL = 1024
BATCH = 32
SEQ = 256
DEPTH = 4
DEC_BATCH = 4
DEC_SEQ = 1024
PAST_LEN = 512

GRID_W = 64
N_MOD = 6
RMS_EPS = 1e-6
POS_BASE = 10000.0
GLA_HEADS = 4
GLA_DK = 64
GLA_DV = 128
GLA_RANK = 16
GLA_GATE_NORM = 16.0
GLA_CHUNK = 64
FOURIER_GROUPS = 4
FOURIER_GW = 64
FOURIER_W = FOURIER_GROUPS * FOURIER_GW
LRU_BLOCKS = 4
LRU_BW = 64
LRU_W = LRU_BLOCKS * LRU_BW
LRU_C = 8.0
CONV_W = 4
N_EXPERTS = 16
EXPERT_FF = 1024
CAPACITY_FACTOR = 2
GLA_QK_W = GLA_HEADS * GLA_DK
GLA_V_W = GLA_HEADS * GLA_DV
GLA_LR_W = 2 * GLA_RANK
MIX_W = GLA_V_W + FOURIER_W + LRU_W
IN_SPLITS = (GLA_QK_W,
             2 * GLA_QK_W,
             2 * GLA_QK_W + GLA_V_W,
             2 * GLA_QK_W + 2 * GLA_V_W,
             2 * GLA_QK_W + 2 * GLA_V_W + GLA_LR_W,
             2 * GLA_QK_W + 2 * GLA_V_W + GLA_LR_W + FOURIER_W,
             2 * GLA_QK_W + 2 * GLA_V_W + GLA_LR_W + FOURIER_W + LRU_W)
IN_W = 2 * GLA_QK_W + 2 * GLA_V_W + GLA_LR_W + FOURIER_W + 2 * LRU_W

kernel_name = 'hybrid_flow_gla_fnet_rglru_ec_step'


def rms_norm(x, gain):
    x32 = x.astype(jnp.float32)
    y = x32 * lax.rsqrt(jnp.mean(x32 * x32, axis=-1, keepdims=True) + RMS_EPS)
    return y.astype(x.dtype) * gain


def flip(t):
    return t[:, ::-1]


def grid_position_embedding(n_tokens, dtype):
    rows = n_tokens // GRID_W
    r, col = jnp.meshgrid(jnp.arange(rows, dtype=jnp.float32), jnp.arange(GRID_W, dtype=jnp.float32), indexing='ij')
    n_freq = D_MODEL // 4
    omega = 1.0 / (POS_BASE ** (jnp.arange(n_freq, dtype=jnp.float32) / n_freq))
    ar = r.reshape(-1)[:, None] * omega
    ac = col.reshape(-1)[:, None] * omega
    return jnp.concatenate([jnp.sin(ar), jnp.cos(ar), jnp.sin(ac), jnp.cos(ac)], axis=-1).astype(dtype)


def gla_chunked(q, k, v, g, s0):
    f32 = jnp.float32
    B, L, H, K = q.shape
    V = v.shape[-1]
    N = L // GLA_CHUNK
    q, k, g = (t.astype(f32).reshape(B, N, GLA_CHUNK, H, K) for t in (q, k, g))
    v = v.astype(f32).reshape(B, N, GLA_CHUNK, H, V)
    b = jnp.cumsum(g, axis=2)
    b_last = b[:, :, -1]
    q_dec = q * jnp.exp(b)
    scores = jnp.einsum('bnthk,bnshk->bnhts', q_dec, k * jnp.exp(-b))
    lower_tri = jnp.tril(jnp.ones((GLA_CHUNK, GLA_CHUNK), dtype=bool))
    scores = jnp.where(lower_tri, scores, 0.0)
    o_intra = jnp.einsum('bnhts,bnshv->bnthv', scores, v)
    ds = jnp.einsum('bnshk,bnshv->bnhkv', k * jnp.exp(b_last[:, :, None] - b), v)

    def chunk_step(s, inputs):
        ds_n, bl_n = inputs
        return jnp.exp(bl_n)[..., None] * s + ds_n, s

    s_final, s_start = lax.scan(chunk_step, s0, (jnp.moveaxis(ds, 1, 0), jnp.moveaxis(b_last, 1, 0)))
    o_inter = jnp.einsum('bnthk,bnhkv->bnthv', q_dec, jnp.moveaxis(s_start, 0, 1))
    return (o_intra + o_inter).reshape(B, L, H, V), s_final


def gla_mixer(q, k, v, og, lr, w_dec, b_dec, norm_gain, s0):
    B, L = q.shape[:2]
    q = q.reshape(B, L, GLA_HEADS, GLA_DK) * (GLA_DK ** -0.5)
    k = k.reshape(B, L, GLA_HEADS, GLA_DK)
    v = v.reshape(B, L, GLA_HEADS, GLA_DV)
    z = jnp.einsum('bldr,drk->bldk', lr.reshape(B, L, 2, GLA_RANK).astype(jnp.float32),
                   w_dec.astype(jnp.float32)) + b_dec.astype(jnp.float32)
    g = (jax.nn.log_sigmoid(z) / GLA_GATE_NORM).reshape(B, L, 2, GLA_HEADS, GLA_DK)
    s0 = s0.astype(jnp.float32)
    o_f, s_f = gla_chunked(q, k, v, g[:, :, 0], s0[:, 0])
    o_b, s_b = gla_chunked(flip(q), flip(k), flip(v), flip(g[:, :, 1]), s0[:, 1])
    o = o_f + flip(o_b)
    o = o * lax.rsqrt(jnp.mean(o * o, axis=-1, keepdims=True) + RMS_EPS)
    o = o.astype(og.dtype) * norm_gain
    return o.reshape(B, L, GLA_V_W) * jax.nn.silu(og), jnp.stack([s_f, s_b], axis=1)


def fourier_mixer(u):
    B, L, _ = u.shape
    spec = jnp.fft.fftn(u.astype(jnp.float32).reshape(B, L, FOURIER_GROUPS, FOURIER_GW), axes=(1, 3), norm='ortho')
    return spec.real.reshape(B, L, FOURIER_W).astype(u.dtype)


def centred_depthwise_conv(x, w, b):
    L = x.shape[1]
    left = CONV_W // 2
    xp = jnp.pad(x, ((0, 0), (left, CONV_W - 1 - left), (0, 0)))
    return sum(xp[:, j:j + L] * w[j] for j in range(CONV_W)) + b


def rglru_scan(x, wa, ba, wx, bx, lam, h0):
    f32 = jnp.float32
    B, L, W = x.shape
    x32 = x.astype(f32)
    xb = x32.reshape(B, L, LRU_BLOCKS, LRU_BW)
    r = jax.nn.sigmoid(jnp.einsum('blnc,ncd->blnd', xb, wa.astype(f32)).reshape(B, L, W) + ba.astype(f32))
    i = jax.nn.sigmoid(jnp.einsum('blnc,ncd->blnd', xb, wx.astype(f32)).reshape(B, L, W) + bx.astype(f32))
    log_a = -LRU_C * r * jax.nn.softplus(-lam.astype(f32))
    a = jnp.exp(log_a)
    u = jnp.sqrt(-jnp.expm1(2.0 * log_a)) * (i * x32)
    u = u.at[:, 0].add(a[:, 0] * h0)

    def combine(lhs, rhs):
        a1, b1 = lhs
        a2, b2 = rhs
        return a1 * a2, a2 * b1 + b2

    _, h = lax.associative_scan(combine, (a, u), axis=1)
    return h, h[:, -1]


def lru_mixer(ux, ug, conv_w, conv_b, wa, ba, wx, bx, lam, s0):
    xc = centred_depthwise_conv(ux, conv_w, conv_b)
    s0 = s0.astype(jnp.float32)
    h_f, s_f = rglru_scan(xc, wa[0], ba[0], wx[0], bx[0], lam[0], s0[:, 0])
    h_b, s_b = rglru_scan(flip(xc), wa[1], ba[1], wx[1], bx[1], lam[1], s0[:, 1])
    h = (h_f + flip(h_b)).astype(ux.dtype)
    return h * jax.nn.gelu(ug), jnp.stack([s_f, s_b], axis=1)


def expert_choice_ffn(h, w_router, w_gate, w_up, w_down):
    B, L, D = h.shape
    T = B * L
    capacity = CAPACITY_FACTOR * T // N_EXPERTS
    ht = h.reshape(T, D)
    affinity = jax.nn.softmax((ht @ w_router).astype(jnp.float32), axis=-1)
    gate, idx = lax.top_k(affinity.T, capacity)
    xe = ht[idx]
    hid = jax.nn.silu(jnp.einsum('ecd,edf->ecf', xe, w_gate)) * jnp.einsum('ecd,edf->ecf', xe, w_up)
    ye = jnp.einsum('ecf,efd->ecd', hid, w_down) * gate[..., None].astype(h.dtype)
    out = jnp.zeros((T, D), h.dtype).at[idx.reshape(-1)].add(ye.reshape(-1, D))
    return out.reshape(B, L, D)


def setup_inputs(seed: int = 0) -> dict:
    key = jax.random.key(seed)
    ks = jax.random.split(key, 32)
    n = jax.random.normal
    f32 = jnp.float32
    a0 = jax.random.uniform(ks[20], (DEPTH, 2, LRU_W), f32, minval=0.9, maxval=0.999)
    return {
        'x_prompt': n(ks[0], (BATCH, SEQ, D_MODEL), f32),
        'x_sample': n(ks[1], (DEC_BATCH, DEC_SEQ, D_MODEL), f32),
        'state_gla': n(ks[2], (DEC_BATCH, DEPTH, 2, GLA_HEADS, GLA_DK, GLA_DV), f32),
        'state_rglru': 0.5 * n(ks[3], (DEC_BATCH, DEPTH, 2, LRU_W), f32),
        'c': n(ks[4], (DEC_BATCH, D_MODEL), f32),
        'c_ctx': n(ks[5], (D_MODEL,), f32),
        'w_mod': n(ks[6], (DEPTH, D_MODEL, N_MOD * D_MODEL), f32) * D_MODEL ** -0.5,
        'b_mod': 0.01 * n(ks[7], (DEPTH, N_MOD * D_MODEL), f32),
        'norm1': 1.0 + 0.01 * n(ks[8], (DEPTH, D_MODEL), f32),
        'norm2': 1.0 + 0.01 * n(ks[9], (DEPTH, D_MODEL), f32),
        'w_in': n(ks[10], (DEPTH, D_MODEL, IN_W), f32) * D_MODEL ** -0.5,
        'gla_w_decay': n(ks[11], (DEPTH, 2, GLA_RANK, GLA_QK_W), f32) * GLA_RANK ** -0.5,
        'gla_b_decay': 1.0 + 0.1 * n(ks[12], (DEPTH, 2, GLA_QK_W), f32),
        'gla_norm': 1.0 + 0.01 * n(ks[13], (DEPTH, GLA_DV), f32),
        'lru_conv_w': n(ks[14], (DEPTH, CONV_W, LRU_W), f32) * CONV_W ** -0.5,
        'lru_conv_b': 0.01 * n(ks[15], (DEPTH, LRU_W), f32),
        'lru_wa': n(ks[16], (DEPTH, 2, LRU_BLOCKS, LRU_BW, LRU_BW), f32) * LRU_BW ** -0.5,
        'lru_ba': 0.01 * n(ks[17], (DEPTH, 2, LRU_W), f32),
        'lru_wx': n(ks[18], (DEPTH, 2, LRU_BLOCKS, LRU_BW, LRU_BW), f32) * LRU_BW ** -0.5,
        'lru_bx': 0.01 * n(ks[19], (DEPTH, 2, LRU_W), f32),
        'lru_lambda': jnp.log(a0) - jnp.log1p(-a0),
        'w_out': n(ks[21], (DEPTH, MIX_W, D_MODEL), f32) * MIX_W ** -0.5,
        'w_router': n(ks[22], (DEPTH, D_MODEL, N_EXPERTS), f32) * D_MODEL ** -0.5,
        'w_expert_gate': n(ks[23], (DEPTH, N_EXPERTS, D_MODEL, EXPERT_FF), f32) * D_MODEL ** -0.5,
        'w_expert_up': n(ks[24], (DEPTH, N_EXPERTS, D_MODEL, EXPERT_FF), f32) * D_MODEL ** -0.5,
        'w_expert_down': n(ks[25], (DEPTH, N_EXPERTS, EXPERT_FF, D_MODEL), f32) * EXPERT_FF ** -0.5,
        'final_norm': 1.0 + 0.01 * n(ks[26], (D_MODEL,), f32),
    }


def reference(x_prompt, x_sample, state_gla, state_rglru, c, c_ctx, w_mod, b_mod, norm1, norm2, w_in,
              gla_w_decay, gla_b_decay, gla_norm, lru_conv_w, lru_conv_b, lru_wa, lru_ba, lru_wx, lru_bx,
              lru_lambda, w_out, w_router, w_expert_gate, w_expert_up, w_expert_down, final_norm):

    def layer(x, cond, l, gla_s0, lru_s0):
        mod = (jax.nn.silu(cond) @ w_mod[l] + b_mod[l])[:, None, :]
        shift1, scale1, gate1, shift2, scale2, gate2 = jnp.split(mod, N_MOD, axis=-1)
        h = rms_norm(x, norm1[l]) * (1 + scale1) + shift1
        q, k, v, og, lr, uf, ux, ug = jnp.split(h @ w_in[l], IN_SPLITS, axis=-1)
        o_gla, gla_s = gla_mixer(q, k, v, og, lr, gla_w_decay[l], gla_b_decay[l], gla_norm[l], gla_s0)
        o_fft = fourier_mixer(uf)
        o_lru, lru_s = lru_mixer(ux, ug, lru_conv_w[l], lru_conv_b[l], lru_wa[l], lru_ba[l],
                                 lru_wx[l], lru_bx[l], lru_lambda[l], lru_s0)
        x = x + gate1 * (jnp.concatenate([o_gla, o_fft, o_lru], axis=-1) @ w_out[l])
        h = rms_norm(x, norm2[l]) * (1 + scale2) + shift2
        x = x + gate2 * expert_choice_ffn(h, w_router[l], w_expert_gate[l], w_expert_up[l], w_expert_down[l])
        return x, gla_s, lru_s

    bp = x_prompt.shape[0]
    gla_zero = jnp.zeros((bp, 2, GLA_HEADS, GLA_DK, GLA_DV), jnp.float32)
    lru_zero = jnp.zeros((bp, 2, LRU_W), jnp.float32)
    xp = x_prompt
    gla_states = []
    lru_states = []
    for l in range(DEPTH):
        xp, gs, ls = layer(xp, c_ctx[None, :], l, gla_zero, lru_zero)
        gla_states.append(gs)
        lru_states.append(ls)
    y_prompt = rms_norm(xp, final_norm)
    new_state_gla = jnp.stack(gla_states, axis=1).astype(x_prompt.dtype)
    new_state_rglru = jnp.stack(lru_states, axis=1).astype(x_prompt.dtype)

    xs = x_sample + grid_position_embedding(x_sample.shape[1], x_sample.dtype)
    for l in range(DEPTH):
        xs, _, _ = layer(xs, c, l, state_gla[:, l], state_rglru[:, l])
    y_sample = rms_norm(xs, final_norm)

    return (y_prompt, y_sample, new_state_gla, new_state_rglru)
```

```python
import functools
import math

import numpy as np
import jax
import jax.numpy as jnp
from jax import lax
from jax.experimental import pallas as pl
from jax.experimental.pallas import tpu as pltpu

F32 = jnp.float32
BF16 = jnp.bfloat16
I32 = jnp.int32

D_MODEL = 1024
DEPTH = 4
GRID_W = 64
N_MOD = 6
RMS_EPS = 1e-6
POS_BASE = 10000.0
GLA_HEADS = 4
GLA_DK = 64
GLA_DV = 128
GLA_RANK = 16
GLA_GATE_NORM = 16.0
FOURIER_GROUPS = 4
FOURIER_GW = 64
FOURIER_W = FOURIER_GROUPS * FOURIER_GW
LRU_BLOCKS = 4
LRU_BW = 64
LRU_W = LRU_BLOCKS * LRU_BW
LRU_C = 8.0
CONV_W = 4
N_EXPERTS = 16
EXPERT_FF = 1024
CAPACITY_FACTOR = 2
GLA_QK_W = GLA_HEADS * GLA_DK
GLA_V_W = GLA_HEADS * GLA_DV
GLA_LR_W = 2 * GLA_RANK
MIX_W = GLA_V_W + FOURIER_W + LRU_W
IN_W = 2 * GLA_QK_W + 2 * GLA_V_W + GLA_LR_W + FOURIER_W + 2 * LRU_W
_C_OG_END = 2 * GLA_QK_W + 2 * GLA_V_W
_C_LR_END = _C_OG_END + GLA_LR_W

LANES = 128
SUBLANES = 8
VMEM_LIMIT = 56 * 1024 * 1024

TOKEN_TILE = 256
GLA_CH = 256
ROW_SUB = D_MODEL // LANES
ACC_SUB = 2 * ROW_SUB
MOE_ROWS = 256
DIGIT = 32


def _cparams(*sem):
    return pltpu.CompilerParams(dimension_semantics=sem, vmem_limit_bytes=VMEM_LIMIT)


def _dot(a, b):
    return jnp.dot(a, b, preferred_element_type=F32)


def _dot_nt(a, b):
    return lax.dot_general(a, b, (((1,), (1,)), ((), ())), preferred_element_type=F32)


def _dot_tn(a, b):
    return lax.dot_general(a, b, (((0,), (0,)), ((), ())), preferred_element_type=F32)


def _split(x):
    hi = x.astype(BF16)
    lo = (x - hi.astype(F32)).astype(BF16)
    return hi, lo


def _sigmoid(x):
    return 1.0 / (1.0 + jnp.exp(-x))


def _rms(x):
    return x * lax.rsqrt(jnp.mean(x * x, axis=-1, keepdims=True) + RMS_EPS)


def _mod_kernel(cond_ref, w_ref, b_ref, o_ref):
    a = cond_ref[...]
    a = a * _sigmoid(a)
    o_ref[0] = _dot(a.astype(BF16), w_ref[0].astype(BF16)) + b_ref[0]


def _modulation(cond, w_mod, b_mod):
    tn = 1536
    nw = N_MOD * D_MODEL
    return pl.pallas_call(
        _mod_kernel,
        grid=(DEPTH, nw // tn),
        in_specs=[pl.BlockSpec((SUBLANES, D_MODEL), lambda l, j: (0, 0)),
                  pl.BlockSpec((1, D_MODEL, tn), lambda l, j: (l, 0, j)),
                  pl.BlockSpec((1, 1, tn), lambda l, j: (l, 0, j))],
        out_specs=pl.BlockSpec((1, SUBLANES, tn), lambda l, j: (l, 0, j)),
        out_shape=jax.ShapeDtypeStruct((DEPTH, SUBLANES, nw), F32),
        compiler_params=_cparams("arbitrary", "arbitrary"),
        name="modulation",
    )(cond, w_mod, b_mod.reshape(DEPTH, 1, nw))


def _rows_to_flat(ref):
    return jnp.concatenate([ref[:, s, :] for s in range(ROW_SUB)], axis=1)


def _inproj_kernel(*refs, mode):
    if mode == "plain":
        x_ref, mod_ref, n1_ref, w_ref = refs[:4]
        outs = refs[4:]
        x = x_ref[...]
    elif mode == "pos":
        x_ref, pos_ref, mod_ref, n1_ref, w_ref = refs[:5]
        outs = refs[5:]
        x = x_ref[...] + pos_ref[...]
    else:
        x_ref, acc_ref, gmod_ref, mod_ref, n1_ref, w_ref = refs[:6]
        outs = refs[6:]
        x = x_ref[...] + gmod_ref[0][:, 5 * D_MODEL:6 * D_MODEL] * _rows_to_flat(acc_ref)
    if mode != "plain":
        xo_ref = outs[0]
        outs = outs[1:]
        xo_ref[...] = x
    qk_ref, v_ref, og_ref, uf_ref, ux_ref, ug_ref, lr_ref, wsc = outs

    @pl.when(pl.program_id(0) == 0)
    def _():
        for r in range(0, D_MODEL, 256):
            wsc[r:r + 256, 0:_C_OG_END] = w_ref[r:r + 256, 0:_C_OG_END].astype(BF16)
            wsc[r:r + 256, _C_OG_END:_C_OG_END + 768] = w_ref[r:r + 256, _C_LR_END:IN_W].astype(BF16)
            lrw = w_ref[r:r + 256, _C_OG_END:_C_LR_END].astype(BF16)
            wsc[r:r + 256, _C_OG_END + 768:_C_OG_END + 896] = jnp.concatenate(
                [lrw, jnp.zeros((256, LANES - GLA_LR_W), BF16)], axis=1)

    m = mod_ref[0]
    h = _rms(x) * n1_ref[...] * (1.0 + m[:, D_MODEL:2 * D_MODEL]) + m[:, 0:D_MODEL]
    hb = h.astype(BF16)
    qk_ref[...] = _dot(hb, wsc[:, 0:512])
    v_ref[...] = _dot(hb, wsc[:, 512:1024])
    og_ref[...] = _dot(hb, wsc[:, 1024:1536])
    uf_ref[...] = _dot(hb, wsc[:, 1536:1792])
    ux_ref[...] = _dot(hb, wsc[:, 1792:2048])
    ug_ref[...] = _dot(hb, wsc[:, 2048:2304])
    lr_ref[...] = _dot(hb, wsc[:, 2304:2432])


def _inproj(mode, x, extra, mod_l, norm1_l, w_in_l, B, L):
    T = B * L
    tm = TOKEN_TILE
    tpb = L // tm
    per_batch = mod_l.shape[0] > 1
    bidx = (lambda i: (i // tpb, 0, 0)) if per_batch else (lambda i: (0, 0, 0))
    row = lambda i: (i, 0)
    in_specs = [pl.BlockSpec((tm, D_MODEL), row)]
    args = [x]
    if mode == "pos":
        in_specs.append(pl.BlockSpec((tm, D_MODEL), lambda i: (i % tpb, 0)))
        args.append(extra)
    elif mode == "res":
        acc, gmod = extra
        in_specs += [pl.BlockSpec((tm, ROW_SUB, LANES), lambda i: (i, 0, 0)),
                     pl.BlockSpec((1, 1, N_MOD * D_MODEL), bidx)]
        args += [acc, gmod]
    in_specs += [pl.BlockSpec((1, 1, N_MOD * D_MODEL), bidx),
                 pl.BlockSpec((1, D_MODEL), lambda i: (0, 0)),
                 pl.BlockSpec((D_MODEL, IN_W), lambda i: (0, 0))]
    args += [mod_l, norm1_l.reshape(1, D_MODEL), w_in_l]
    widths = [512, 512, 512, 256, 256, 256, LANES]
    out_specs = [pl.BlockSpec((tm, w), row) for w in widths]
    out_shape = [jax.ShapeDtypeStruct((T, w), F32) for w in widths]
    if mode != "plain":
        out_specs = [pl.BlockSpec((tm, D_MODEL), row)] + out_specs
        out_shape = [jax.ShapeDtypeStruct((T, D_MODEL), F32)] + out_shape
    res = pl.pallas_call(
        functools.partial(_inproj_kernel, mode=mode),
        grid=(T // tm,),
        in_specs=in_specs,
        out_specs=out_specs,
        out_shape=out_shape,
        scratch_shapes=[pltpu.VMEM((D_MODEL, 2432), BF16)],
        compiler_params=_cparams("arbitrary"),
        name="inproj_" + mode,
    )(*args)
    if mode == "plain":
        return (x,) + tuple(res)
    return tuple(res)


def _gla_kernel(*refs, L, has_s0, want_state):
    qk_ref, v_ref, og_ref, lr_ref, wdec_ref, bdec_ref, gn_ref = refs[:7]
    p = 7
    s0_ref = None
    if has_s0:
        s0_ref = refs[p]
        p += 1
    o_ref = refs[p]
    p += 1
    sn_ref = None
    if want_state:
        sn_ref = refs[p]
        p += 1
    g_scr, oacc, s_scr = refs[p:p + 3]

    ch = GLA_CH
    n_chunks = L // ch
    kw = GLA_QK_W
    vw = GLA_V_W

    z16 = jnp.zeros((GLA_RANK, kw), F32)
    wc = jnp.concatenate([
        jnp.concatenate([wdec_ref[0], z16], axis=1),
        jnp.concatenate([z16, wdec_ref[1]], axis=1),
        jnp.zeros((LANES - GLA_LR_W, 2 * kw), F32)], axis=0).astype(BF16)
    bias = jnp.concatenate([bdec_ref[0], bdec_ref[1]], axis=1)
    z = _dot(lr_ref[0].astype(BF16), wc) + bias
    g_scr[...] = (jnp.minimum(z, 0.0) - jnp.log1p(jnp.exp(-jnp.abs(z)))) * (1.0 / GLA_GATE_NORM)

    row = lax.broadcasted_iota(I32, (ch, ch), 0)
    col = lax.broadcasted_iota(I32, (ch, ch), 1)
    lane_head = lax.broadcasted_iota(I32, (1, kw), 1) // GLA_DK
    blockdiag = (lax.broadcasted_iota(I32, (kw, vw), 0) // GLA_DK) == (lax.broadcasted_iota(I32, (kw, vw), 1) // GLA_DV)
    ones_t = jnp.ones((ch, LANES), BF16)
    gn = gn_ref[...]

    def finish(o, r0):
        parts = []
        for h in range(GLA_HEADS):
            parts.append(_rms(o[:, h * GLA_DV:(h + 1) * GLA_DV]) * gn)
        ogv = og_ref[0, r0:r0 + ch, :]
        return (jnp.concatenate(parts, axis=1) * (ogv * _sigmoid(ogv))).astype(BF16)

    for d in range(2):
        allowed = (col <= row) if d == 0 else (col >= row)
        tri = jnp.where(allowed, 1.0, 0.0).astype(BF16)
        if has_s0:
            s_scr[...] = jnp.zeros((kw, vw), F32)
            for h in range(GLA_HEADS):
                s_scr[h * GLA_DK:(h + 1) * GLA_DK, h * GLA_DV:(h + 1) * GLA_DV] = s0_ref[0, d, h]
        for i in range(n_chunks):
            n = i if d == 0 else n_chunks - 1 - i
            r0 = n * ch
            state_is_zero = (i == 0) and not has_s0
            gch = g_scr[r0:r0 + ch, d * kw:(d + 1) * kw]
            g_hi, g_lo = _split(gch)
            b = _dot(tri, g_hi) + _dot(tri, g_lo)
            b_last = b[ch - 1:ch, :] if d == 0 else b[0:1, :]
            bc = b - b[ch // 2:ch // 2 + 1, :]
            qch = qk_ref[0, r0:r0 + ch, 0:kw] * (GLA_DK ** -0.5)
            kch = qk_ref[0, r0:r0 + ch, kw:2 * kw]
            vb = v_ref[0, r0:r0 + ch, :].astype(BF16)
            q_s = (qch * jnp.exp(bc)).astype(BF16)
            k_s = (kch * jnp.exp(-bc)).astype(BF16)
            zero_q = jnp.zeros_like(q_s)
            qbig = jnp.concatenate([jnp.where(lane_head == h, q_s, zero_q) for h in range(GLA_HEADS)], axis=0)
            scores = _dot_nt(qbig, k_s)
            parts = []
            for h in range(GLA_HEADS):
                ph = jnp.where(allowed, scores[h * ch:(h + 1) * ch, :], 0.0).astype(BF16)
                parts.append(_dot(ph, vb[:, h * GLA_DV:(h + 1) * GLA_DV]))
            o = jnp.concatenate(parts, axis=1)
            if not state_is_zero:
                q_t = (qch * jnp.exp(b)).astype(BF16)
                o = o + _dot(q_t, s_scr[...].astype(BF16))
            if (i < n_chunks - 1) or want_state:
                k_d = (kch * jnp.exp(b_last - b)).astype(BF16)
                ds = jnp.where(blockdiag, _dot_tn(k_d, vb), 0.0)
                if state_is_zero:
                    s_scr[...] = ds
                else:
                    dcol = _dot_tn(g_hi, ones_t) + _dot_tn(g_lo, ones_t)
                    dec = jnp.exp(dcol)
                    s_scr[...] = s_scr[...] * jnp.concatenate([dec] * (vw // LANES), axis=1) + ds
            if d == 0:
                oacc[r0:r0 + ch, :] = o
            else:
                o_ref[0, r0:r0 + ch, :] = finish(o + oacc[r0:r0 + ch, :], r0)
        if want_state:
            for h in range(GLA_HEADS):
                sn_ref[0, d, h] = s_scr[h * GLA_DK:(h + 1) * GLA_DK, h * GLA_DV:(h + 1) * GLA_DV]


def _gla(qk, v, og, lr, wdec_l, bdec_l, gn_l, s0, B, L, want_state):
    has_s0 = s0 is not None
    blk = lambda w: pl.BlockSpec((1, L, w), lambda b: (b, 0, 0))
    in_specs = [blk(512), blk(512), blk(512), blk(LANES),
                pl.BlockSpec((2, GLA_RANK, GLA_QK_W), lambda b: (0, 0, 0)),
                pl.BlockSpec((2, 1, GLA_QK_W), lambda b: (0, 0, 0)),
                pl.BlockSpec((1, GLA_DV), lambda b: (0, 0))]
    args = [qk.reshape(B, L, 512), v.reshape(B, L, 512), og.reshape(B, L, 512), lr.reshape(B, L, LANES),
            wdec_l, bdec_l.reshape(2, 1, GLA_QK_W), gn_l.reshape(1, GLA_DV)]
    st_spec = pl.BlockSpec((1, 2, GLA_HEADS, GLA_DK, GLA_DV), lambda b: (b, 0, 0, 0, 0))
    if has_s0:
        in_specs.append(st_spec)
        args.append(s0)
    out_specs = [pl.BlockSpec((1, L, GLA_V_W), lambda b: (b, 0, 0))]
    out_shape = [jax.ShapeDtypeStruct((B, L, GLA_V_W), BF16)]
    if want_state:
        out_specs.append(st_spec)
        out_shape.append(jax.ShapeDtypeStruct((B, 2, GLA_HEADS, GLA_DK, GLA_DV), F32))
    res = pl.pallas_call(
        functools.partial(_gla_kernel, L=L, has_s0=has_s0, want_state=want_state),
        grid=(B,),
        in_specs=in_specs,
        out_specs=out_specs,
        out_shape=out_shape,
        scratch_shapes=[pltpu.VMEM((L, 2 * GLA_QK_W), F32),
                        pltpu.VMEM((L, GLA_V_W), F32),
                        pltpu.VMEM((GLA_QK_W, GLA_V_W), F32)],
        compiler_params=_cparams("arbitrary"),
        name="gla",
    )(*args)
    o = res[0].reshape(B * L, GLA_V_W)
    return o, (res[1] if want_state else None)


def _fft_tables(L):
    m = np.arange(L, dtype=np.int64)
    ang = 2.0 * np.pi * ((m[:, None] * m[None, :]) % L) / L
    cc = np.concatenate([np.cos(ang), -np.sin(ang)], axis=1)
    c = np.arange(FOURIER_GW, dtype=np.int64)
    angc = 2.0 * np.pi * ((c[:, None] * c[None, :]) % FOURIER_GW) / FOURIER_GW
    scale = 1.0 / math.sqrt(L * FOURIER_GW)
    eye = np.eye(FOURIER_GROUPS)
    bdc = np.kron(eye, np.cos(angc) * scale)
    bds = np.kron(eye, np.sin(angc) * scale)
    return (jnp.asarray(cc, dtype=F32), jnp.asarray(bdc, dtype=F32), jnp.asarray(bds, dtype=F32))


def _fft_kernel(u_ref, cc_ref, bdc_ref, bds_ref, o_ref):
    u_hi, u_lo = _split(u_ref[0])
    bdc = bdc_ref[...].astype(BF16)
    bds = bds_ref[...].astype(BF16)
    uc = _dot(u_hi, bdc) + _dot(u_lo, bdc)
    us = _dot(u_hi, bds) + _dot(u_lo, bds)
    w_hi, w_lo = _split(jnp.concatenate([uc, us], axis=0))
    cc = cc_ref[...].astype(BF16)
    o_ref[0] = (_dot(cc, w_hi) + _dot(cc, w_lo)).astype(BF16)


def _fft(uf, B, L):
    cc, bdc, bds = _fft_tables(L)
    res = pl.pallas_call(
        _fft_kernel,
        grid=(B,),
        in_specs=[pl.BlockSpec((1, L, FOURIER_W), lambda b: (b, 0, 0)),
                  pl.BlockSpec((L, 2 * L), lambda b: (0, 0)),
                  pl.BlockSpec((FOURIER_W, FOURIER_W), lambda b: (0, 0)),
                  pl.BlockSpec((FOURIER_W, FOURIER_W), lambda b: (0, 0))],
        out_specs=pl.BlockSpec((1, L, FOURIER_W), lambda b: (b, 0, 0)),
        out_shape=jax.ShapeDtypeStruct((B, L, FOURIER_W), BF16),
        compiler_params=_cparams("arbitrary"),
        name="fourier",
    )(uf.reshape(B, L, FOURIER_W), cc, bdc, bds)
    return res.reshape(B * L, FOURIER_W)


def _lru_kernel(*refs, L, has_s0, want_state):
    ux_ref, ug_ref, cw_ref, cb_ref, wa_ref, ba_ref, wx_ref, bx_ref, lam_ref = refs[:9]
    p = 9
    s0_ref = None
    if has_s0:
        s0_ref = refs[p]
        p += 1
    o_ref = refs[p]
    p += 1
    sn_ref = None
    if want_state:
        sn_ref = refs[p]
        p += 1
    bd_scr = refs[p]

    @pl.when(pl.program_id(0) == 0)
    def _():
        r = lax.broadcasted_iota(I32, (LRU_BW, LRU_W), 0)
        c = lax.broadcasted_iota(I32, (LRU_BW, LRU_W), 1)
        for d in range(2):
            for gi, w_ref in enumerate((wa_ref, wx_ref)):
                pieces = []
                for h in range(LRU_BLOCKS):
                    place = jnp.where(c == r + h * LRU_BW, 1.0, 0.0).astype(BF16)
                    pieces.append(_dot(w_ref[d, h].astype(BF16), place))
                bd_scr[2 * d + gi] = jnp.concatenate(pieces, axis=0).astype(BF16)

    t = lax.broadcasted_iota(I32, (L, 1), 0)
    x = ux_ref[0]
    xm2 = jnp.where(t >= 2, pltpu.roll(x, 2, 0), 0.0)
    xm1 = jnp.where(t >= 1, pltpu.roll(x, 1, 0), 0.0)
    xp1 = jnp.where(t <= L - 2, pltpu.roll(x, L - 1, 0), 0.0)
    xc = xm2 * cw_ref[0:1, :] + xm1 * cw_ref[1:2, :] + x * cw_ref[2:3, :] + xp1 * cw_ref[3:4, :] + cb_ref[...]
    xcb = xc.astype(BF16)

    hsum = None
    for d in range(2):
        r = _sigmoid(_dot(xcb, bd_scr[2 * d]) + ba_ref[d])
        ig = _sigmoid(_dot(xcb, bd_scr[2 * d + 1]) + bx_ref[d])
        lam = lam_ref[d]
        softplus = jnp.maximum(-lam, 0.0) + jnp.log1p(jnp.exp(-jnp.abs(lam)))
        log_a = -LRU_C * r * softplus
        a = jnp.exp(log_a)
        u = jnp.sqrt(1.0 - jnp.exp(2.0 * log_a)) * (ig * xc)
        if has_s0:
            edge = (t == 0) if d == 0 else (t == L - 1)
            u = u + jnp.where(edge, a * s0_ref[0, d:d + 1, :], 0.0)
        s = 1
        while s < L:
            if d == 0:
                keep = t >= s
                a_sh = jnp.where(keep, pltpu.roll(a, s, 0), 1.0)
                u_sh = jnp.where(keep, pltpu.roll(u, s, 0), 0.0)
            else:
                keep = t < L - s
                a_sh = jnp.where(keep, pltpu.roll(a, L - s, 0), 1.0)
                u_sh = jnp.where(keep, pltpu.roll(u, L - s, 0), 0.0)
            u = a * u_sh + u
            s *= 2
            if s < L:
                a = a * a_sh
        if want_state:
            sn_ref[0, d:d + 1, :] = u[L - 1:L, :] if d == 0 else u[0:1, :]
        hsum = u if hsum is None else hsum + u

    ugv = ug_ref[0]
    gelu = 0.5 * ugv * (1.0 + jnp.tanh(math.sqrt(2.0 / math.pi) * (ugv + 0.044715 * (ugv * ugv * ugv))))
    o_ref[0] = (hsum * gelu).astype(BF16)


def _lru(ux, ug, cw_l, cb_l, wa_l, ba_l, wx_l, bx_l, lam_l, s0, B, L, want_state):
    has_s0 = s0 is not None
    blk = pl.BlockSpec((1, L, LRU_W), lambda b: (b, 0, 0))
    vec2 = pl.BlockSpec((2, 1, LRU_W), lambda b: (0, 0, 0))
    wsp = pl.BlockSpec((2, LRU_BLOCKS, LRU_BW, LRU_BW), lambda b: (0, 0, 0, 0))
    in_specs = [blk, blk,
                pl.BlockSpec((CONV_W, LRU_W), lambda b: (0, 0)),
                pl.BlockSpec((1, LRU_W), lambda b: (0, 0)),
                wsp, vec2, wsp, vec2, vec2]
    args = [ux.reshape(B, L, LRU_W), ug.reshape(B, L, LRU_W), cw_l, cb_l.reshape(1, LRU_W),
            wa_l, ba_l.reshape(2, 1, LRU_W), wx_l, bx_l.reshape(2, 1, LRU_W), lam_l.reshape(2, 1, LRU_W)]
    st_spec = pl.BlockSpec((1, 2, LRU_W), lambda b: (b, 0, 0))
    if has_s0:
        in_specs.append(st_spec)
        args.append(s0)
    out_specs = [blk]
    out_shape = [jax.ShapeDtypeStruct((B, L, LRU_W), BF16)]
    if want_state:
        out_specs.append(st_spec)
        out_shape.append(jax.ShapeDtypeStruct((B, 2, LRU_W), F32))
    res = pl.pallas_call(
        functools.partial(_lru_kernel, L=L, has_s0=has_s0, want_state=want_state),
        grid=(B,),
        in_specs=in_specs,
        out_specs=out_specs,
        out_shape=out_shape,
        scratch_shapes=[pltpu.VMEM((4, LRU_W, LRU_W), BF16)],
        compiler_params=_cparams("arbitrary"),
        name="rglru",
    )(*args)
    return res[0].reshape(B * L, LRU_W), (res[1] if want_state else None)


def _outproj_kernel(og_ref, of_ref, ol_ref, x_ref, mod_ref, n2_ref, wout_ref, wr_ref,
                    x1_ref, hx_ref, acc_ref, afft_ref, wsc, wrs):
    @pl.when(pl.program_id(0) == 0)
    def _():
        for r in range(0, MIX_W, 256):
            wsc[r:r + 256, :] = wout_ref[r:r + 256, :].astype(BF16)
        wrs[...] = jnp.concatenate([wr_ref[...], jnp.zeros((D_MODEL, LANES - N_EXPERTS), F32)], axis=1)

    m = mod_ref[0]
    y = (_dot(og_ref[...], wsc[0:GLA_V_W, :])
         + _dot(of_ref[...], wsc[GLA_V_W:GLA_V_W + FOURIER_W, :])
         + _dot(ol_ref[...], wsc[GLA_V_W + FOURIER_W:MIX_W, :]))
    x1 = x_ref[...] + m[:, 2 * D_MODEL:3 * D_MODEL] * y
    x1_ref[...] = x1
    h2 = _rms(x1) * n2_ref[...] * (1.0 + m[:, 4 * D_MODEL:5 * D_MODEL]) + m[:, 3 * D_MODEL:4 * D_MODEL]
    h_hi = h2.astype(BF16)
    h_hi32 = h_hi.astype(F32)
    for s in range(ROW_SUB):
        hx_ref[:, s, :] = h_hi32[:, s * LANES:(s + 1) * LANES]
    h_lo = (h2 - h_hi32).astype(BF16)
    w_hi, w_lo = _split(wrs[...])
    logits = _dot(h_hi, w_hi) + _dot(h_lo, w_hi) + _dot(h_hi, w_lo)
    lane = lax.broadcasted_iota(I32, logits.shape, 1)
    logits = jnp.where(lane < N_EXPERTS, logits, -jnp.inf)
    ex = jnp.exp(logits - jnp.max(logits, axis=-1, keepdims=True))
    aff = ex / jnp.sum(ex, axis=-1, keepdims=True)
    acc_ref[...] = jnp.zeros(acc_ref.shape, F32)
    acc_ref[:, ROW_SUB, :] = aff
    afft_ref[...] = aff.T[0:N_EXPERTS, :]


def _outproj(o_gla, o_fft, o_lru, x, mod_l, norm2_l, w_out_l, w_router_l, B, L):
    T = B * L
    tm = TOKEN_TILE
    tpb = L // tm
    per_batch = mod_l.shape[0] > 1
    bidx = (lambda i: (i // tpb, 0, 0)) if per_batch else (lambda i: (0, 0, 0))
    row = lambda i: (i, 0)
    return pl.pallas_call(
        _outproj_kernel,
        grid=(T // tm,),
        in_specs=[pl.BlockSpec((tm, GLA_V_W), row), pl.BlockSpec((tm, FOURIER_W), row),
                  pl.BlockSpec((tm, LRU_W), row), pl.BlockSpec((tm, D_MODEL), row),
                  pl.BlockSpec((1, 1, N_MOD * D_MODEL), bidx),
                  pl.BlockSpec((1, D_MODEL), lambda i: (0, 0)),
                  pl.BlockSpec((MIX_W, D_MODEL), lambda i: (0, 0)),
                  pl.BlockSpec((D_MODEL, N_EXPERTS), lambda i: (0, 0))],
        out_specs=[pl.BlockSpec((tm, D_MODEL), row),
                   pl.BlockSpec((tm, ROW_SUB, LANES), lambda i: (i, 0, 0)),
                   pl.BlockSpec((tm, ACC_SUB, LANES), lambda i: (i, 0, 0)),
                   pl.BlockSpec((N_EXPERTS, tm), lambda i: (0, i))],
        out_shape=[jax.ShapeDtypeStruct((T, D_MODEL), F32),
                   jax.ShapeDtypeStruct((T, ROW_SUB, LANES), F32),
                   jax.ShapeDtypeStruct((T, ACC_SUB, LANES), F32),
                   jax.ShapeDtypeStruct((N_EXPERTS, T), F32)],
        scratch_shapes=[pltpu.VMEM((MIX_W, D_MODEL), BF16), pltpu.VMEM((D_MODEL, LANES), F32)],
        compiler_params=_cparams("arbitrary"),
        name="outproj",
    )(o_gla, o_fft, o_lru, x, mod_l, norm2_l.reshape(1, D_MODEL), w_out_l, w_router_l)


def _prefix_lanes(x):
    T = x.shape[1]
    w = 256
    nb = T // w
    stacked = jnp.concatenate([x[:, j * w:(j + 1) * w] for j in range(nb)], axis=0)
    upper = jnp.where(lax.broadcasted_iota(I32, (w, w), 0) <= lax.broadcasted_iota(I32, (w, w), 1), 1.0, 0.0)
    pe = _dot(stacked, upper.astype(BF16))
    carry = jnp.zeros((N_EXPERTS, 1), F32)
    outs = []
    for j in range(nb):
        blk = pe[j * N_EXPERTS:(j + 1) * N_EXPERTS, :]
        outs.append(blk + carry)
        carry = carry + blk[:, w - 1:w]
    return jnp.concatenate(outs, axis=1)


def _topk_kernel(aff_ref, out_ref, *, T, C):
    n_a = C // DIGIT
    aff = aff_ref[...]
    bits = jnp.zeros((N_EXPERTS, 1), I32)
    for bit in range(30, -1, -1):
        cand = bits | (1 << bit)
        cnt = jnp.sum(jnp.where(aff >= pltpu.bitcast(cand, F32), 1.0, 0.0), axis=1, keepdims=True)
        bits = jnp.where(cnt >= C, cand, bits)
    thr = pltpu.bitcast(bits, F32)
    gt = aff > thr
    eq = aff == thr
    eqf = jnp.where(eq, 1.0, 0.0)
    need = C - jnp.sum(jnp.where(gt, 1.0, 0.0), axis=1, keepdims=True)
    eq_before = _prefix_lanes(eqf.astype(BF16)) - eqf
    sel = gt | (eq & (eq_before < need))
    cnt = _prefix_lanes(jnp.where(sel, 1.0, 0.0).astype(BF16))

    p_dig = jnp.floor(cnt * (1.0 / DIGIT))
    q_dig = cnt - DIGIT * p_dig
    a_col = lax.broadcasted_iota(I32, (n_a, 1), 0).astype(F32)
    b_col = lax.broadcasted_iota(I32, (DIGIT, 1), 0).astype(F32)
    kc = min(T, 2048)
    acc = jnp.zeros((N_EXPERTS * n_a, N_EXPERTS * DIGIT), F32)
    for c0 in range(0, T, kc):
        u = jnp.concatenate([jnp.where(p_dig[e:e + 1, c0:c0 + kc] == a_col, 1.0, 0.0).astype(BF16)
                             for e in range(N_EXPERTS)], axis=0)
        v = jnp.concatenate([jnp.where(q_dig[e:e + 1, c0:c0 + kc] <= b_col, 1.0, 0.0).astype(BF16)
                             for e in range(N_EXPERTS)], axis=0)
        acc = acc + _dot_nt(u, v)
    below = jnp.concatenate([jnp.sum(jnp.where(p_dig[e:e + 1, :] < a_col, 1.0, 0.0), axis=1, keepdims=True)
                             for e in range(N_EXPERTS)], axis=0)
    r_i = lax.broadcasted_iota(I32, acc.shape, 0) // n_a
    c_i = lax.broadcasted_iota(I32, acc.shape, 1) // DIGIT
    x = jnp.where(r_i == c_i, acc, 0.0)
    x = x[:, 0:256] + x[:, 256:512]
    x = x[:, 0:LANES] + x[:, LANES:2 * LANES]
    x = x + pltpu.roll(x, 64, 1)
    x = x + pltpu.roll(x, 32, 1)
    out_ref[...] = x + below


def _expert_choice(aff_t, T, C):
    n_a = C // DIGIT
    res = pl.pallas_call(
        functools.partial(_topk_kernel, T=T, C=C),
        grid=(1,),
        in_specs=[pl.BlockSpec((N_EXPERTS, T), lambda i: (0, 0))],
        out_specs=pl.BlockSpec((N_EXPERTS * n_a, LANES), lambda i: (0, 0)),
        out_shape=jax.ShapeDtypeStruct((N_EXPERTS * n_a, LANES), F32),
        compiler_params=_cparams("arbitrary"),
        name="expert_choice",
    )(aff_t)
    return res[:, 0:DIGIT].astype(I32).reshape(N_EXPERTS, C)


def _moe_kernel(idx_ref, hx_hbm, acc_in, wg_ref, wu_ref, wd_ref, acc_hbm,
                xbuf, obuf, wgb, wub, wdb, sems, *, C):
    del acc_in
    e = pl.program_id(0)

    def gather(c, carry):
        t = idx_ref[e, c]
        pltpu.make_async_copy(hx_hbm.at[t], xbuf.at[c], sems.at[0]).start()
        pltpu.make_async_copy(acc_hbm.at[t], obuf.at[c], sems.at[1]).start()
        return carry

    lax.fori_loop(0, C, gather, 0)
    for r in range(0, D_MODEL, 256):
        wgb[r:r + 256, :] = wg_ref[0, r:r + 256, :].astype(BF16)
        wub[r:r + 256, :] = wu_ref[0, r:r + 256, :].astype(BF16)
        wdb[r:r + 256, :] = wd_ref[0, r:r + 256, :].astype(BF16)
    pltpu.make_async_copy(hx_hbm.at[pl.ds(0, C)], xbuf, sems.at[0]).wait()
    pltpu.make_async_copy(acc_hbm.at[pl.ds(0, C)], obuf, sems.at[1]).wait()

    lane = lax.broadcasted_iota(I32, (1, LANES), 1)
    step = min(MOE_ROWS, C)
    for r0 in range(0, C, step):
        rows = pl.ds(r0, step)
        x = jnp.concatenate([xbuf[rows, s, :] for s in range(ROW_SUB)], axis=1).astype(BF16)
        gate = jnp.sum(jnp.where(lane == e, obuf[rows, ROW_SUB, :], 0.0), axis=1, keepdims=True)
        g = _dot(x, wgb[...])
        u = _dot(x, wub[...])
        hid = (g * _sigmoid(g) * u).astype(BF16)
        y = _dot(hid, wdb[...]) * gate
        for s in range(ROW_SUB):
            obuf[rows, s, :] = obuf[rows, s, :] + y[:, s * LANES:(s + 1) * LANES]

    def scatter(c, carry):
        t = idx_ref[e, c]
        pltpu.make_async_copy(obuf.at[c, pl.ds(0, ROW_SUB)], acc_hbm.at[t, pl.ds(0, ROW_SUB)], sems.at[2]).start()
        return carry

    lax.fori_loop(0, C, scatter, 0)
    pltpu.make_async_copy(obuf.at[:, pl.ds(0, ROW_SUB)], acc_hbm.at[pl.ds(0, C), pl.ds(0, ROW_SUB)],
                          sems.at[2]).wait()


def _moe(idx, hx, acc, wg_l, wu_l, wd_l, T, C):
    wspec = pl.BlockSpec((1, D_MODEL, EXPERT_FF), lambda e, idx_ref: (e, 0, 0))
    any_spec = pl.BlockSpec(memory_space=pl.ANY)
    return pl.pallas_call(
        functools.partial(_moe_kernel, C=C),
        grid_spec=pltpu.PrefetchScalarGridSpec(
            num_scalar_prefetch=1,
            grid=(N_EXPERTS,),
            in_specs=[any_spec, any_spec, wspec, wspec,
                      pl.BlockSpec((1, EXPERT_FF, D_MODEL), lambda e, idx_ref: (e, 0, 0))],
            out_specs=any_spec,
            scratch_shapes=[pltpu.VMEM((C, ROW_SUB, LANES), F32),
                            pltpu.VMEM((C, ACC_SUB, LANES), F32),
                            pltpu.VMEM((D_MODEL, EXPERT_FF), BF16),
                            pltpu.VMEM((D_MODEL, EXPERT_FF), BF16),
                            pltpu.VMEM((EXPERT_FF, D_MODEL), BF16),
                            pltpu.SemaphoreType.DMA((3,))]),
        out_shape=jax.ShapeDtypeStruct((T, ACC_SUB, LANES), F32),
        input_output_aliases={2: 0},
        compiler_params=_cparams("arbitrary"),
        name="expert_ffn",
    )(idx, hx, acc, wg_l, wu_l, wd_l)


def _final_kernel(x_ref, acc_ref, gmod_ref, fn_ref, o_ref):
    x = x_ref[...] + gmod_ref[0][:, 5 * D_MODEL:6 * D_MODEL] * _rows_to_flat(acc_ref)
    o_ref[...] = _rms(x) * fn_ref[...]


def _final(x1, acc, gmod, final_norm, B, L):
    T = B * L
    tm = TOKEN_TILE
    tpb = L // tm
    per_batch = gmod.shape[0] > 1
    bidx = (lambda i: (i // tpb, 0, 0)) if per_batch else (lambda i: (0, 0, 0))
    return pl.pallas_call(
        _final_kernel,
        grid=(T // tm,),
        in_specs=[pl.BlockSpec((tm, D_MODEL), lambda i: (i, 0)),
                  pl.BlockSpec((tm, ROW_SUB, LANES), lambda i: (i, 0, 0)),
                  pl.BlockSpec((1, 1, N_MOD * D_MODEL), bidx),
                  pl.BlockSpec((1, D_MODEL), lambda i: (0, 0))],
        out_specs=pl.BlockSpec((tm, D_MODEL), lambda i: (i, 0)),
        out_shape=jax.ShapeDtypeStruct((T, D_MODEL), F32),
        compiler_params=_cparams("arbitrary"),
        name="final_norm",
    )(x1, acc, gmod, final_norm.reshape(1, D_MODEL))


def _grid_position_embedding(n_tokens):
    rows = n_tokens // GRID_W
    r, col = jnp.meshgrid(jnp.arange(rows, dtype=F32), jnp.arange(GRID_W, dtype=F32), indexing="ij")
    n_freq = D_MODEL // 4
    omega = 1.0 / (POS_BASE ** (jnp.arange(n_freq, dtype=F32) / n_freq))
    ar = r.reshape(-1)[:, None] * omega
    ac = col.reshape(-1)[:, None] * omega
    return jnp.concatenate([jnp.sin(ar), jnp.cos(ar), jnp.sin(ac), jnp.cos(ac)], axis=-1)


def _trunk(x_in, pos, mod_g, gla_s0, lru_s0, want_state, w):
    B, L, _ = x_in.shape
    T = B * L
    C = CAPACITY_FACTOR * T // N_EXPERTS
    x = x_in.reshape(T, D_MODEL)
    x1 = acc = None
    gla_states = []
    lru_states = []
    for l in range(DEPTH):
        mod_l = mod_g[l]
        if l == 0:
            mode, extra = ("pos", pos) if pos is not None else ("plain", None)
        else:
            mode, extra = "res", (acc, mod_g[l - 1])
            x = x1
        x, qk, v, og, uf, ux, ug, lr = _inproj(mode, x, extra, mod_l, w["norm1"][l], w["w_in"][l], B, L)
        o_gla, gs = _gla(qk, v, og, lr, w["gla_w_decay"][l], w["gla_b_decay"][l], w["gla_norm"][l],
                         None if gla_s0 is None else gla_s0[:, l], B, L, want_state)
        o_fft = _fft(uf, B, L)
        o_lru, ls = _lru(ux, ug, w["lru_conv_w"][l], w["lru_conv_b"][l], w["lru_wa"][l], w["lru_ba"][l],
                         w["lru_wx"][l], w["lru_bx"][l], w["lru_lambda"][l],
                         None if lru_s0 is None else lru_s0[:, l], B, L, want_state)
        x1, hx, acc0, aff_t = _outproj(o_gla, o_fft, o_lru, x, mod_l, w["norm2"][l], w["w_out"][l],
                                       w["w_router"][l], B, L)
        idx = _expert_choice(aff_t, T, C)
        acc = _moe(idx, hx, acc0, w["w_expert_gate"][l], w["w_expert_up"][l], w["w_expert_down"][l], T, C)
        gla_states.append(gs)
        lru_states.append(ls)
    y = _final(x1, acc, mod_g[DEPTH - 1], w["final_norm"], B, L).reshape(B, L, D_MODEL)
    return y, gla_states, lru_states


def kernel(x_prompt, x_sample, state_gla, state_rglru, c, c_ctx, w_mod, b_mod, norm1, norm2, w_in, gla_w_decay, gla_b_decay, gla_norm, lru_conv_w, lru_conv_b, lru_wa, lru_ba, lru_wx, lru_bx, lru_lambda, w_out, w_router, w_expert_gate, w_expert_up, w_expert_down, final_norm):
    w = dict(norm1=norm1, norm2=norm2, w_in=w_in, gla_w_decay=gla_w_decay, gla_b_decay=gla_b_decay,
             gla_norm=gla_norm, lru_conv_w=lru_conv_w, lru_conv_b=lru_conv_b, lru_wa=lru_wa, lru_ba=lru_ba,
             lru_wx=lru_wx, lru_bx=lru_bx, lru_lambda=lru_lambda, w_out=w_out, w_router=w_router,
             w_expert_gate=w_expert_gate, w_expert_up=w_expert_up, w_expert_down=w_expert_down,
             final_norm=final_norm)
    n_lat = c.shape[0]
    cond = jnp.concatenate([c_ctx[None, :], c, jnp.zeros((SUBLANES - 1 - n_lat, D_MODEL), F32)], axis=0)
    mod = _modulation(cond, w_mod, b_mod)
    mod_ctx = mod[:, 0:1].reshape(DEPTH, 1, 1, N_MOD * D_MODEL)
    mod_lat = mod[:, 1:1 + n_lat].reshape(DEPTH, n_lat, 1, N_MOD * D_MODEL)

    y_prompt, gla_states, lru_states = _trunk(x_prompt, None, mod_ctx, None, None, True, w)
    pos = _grid_position_embedding(x_sample.shape[1])
    y_sample, _, _ = _trunk(x_sample, pos, mod_lat, state_gla, state_rglru, False, w)
    new_state_gla = jnp.stack(gla_states, axis=1)
    new_state_rglru = jnp.stack(lru_states, axis=1)
    return (y_prompt, y_sample, new_state_gla, new_state_rglru)
```

```python
import functools
import math

import numpy as np
import jax
import jax.numpy as jnp
from jax import lax
from jax.experimental import pallas as pl
from jax.experimental.pallas import tpu as pltpu

F32 = jnp.float32
BF16 = jnp.bfloat16
I32 = jnp.int32

D_MODEL = 1024
DEPTH = 4
GRID_W = 64
N_MOD = 6
RMS_EPS = 1e-6
POS_BASE = 10000.0
GLA_HEADS = 4
GLA_DK = 64
GLA_DV = 128
GLA_RANK = 16
GLA_GATE_NORM = 16.0
FOURIER_GROUPS = 4
FOURIER_GW = 64
FOURIER_W = FOURIER_GROUPS * FOURIER_GW
LRU_BLOCKS = 4
LRU_BW = 64
LRU_W = LRU_BLOCKS * LRU_BW
LRU_C = 8.0
CONV_W = 4
N_EXPERTS = 16
EXPERT_FF = 1024
CAPACITY_FACTOR = 2
GLA_QK_W = GLA_HEADS * GLA_DK
GLA_V_W = GLA_HEADS * GLA_DV
GLA_LR_W = 2 * GLA_RANK
MIX_W = GLA_V_W + FOURIER_W + LRU_W
IN_W = 2 * GLA_QK_W + 2 * GLA_V_W + GLA_LR_W + FOURIER_W + 2 * LRU_W
_C_OG_END = 2 * GLA_QK_W + 2 * GLA_V_W
_C_LR_END = _C_OG_END + GLA_LR_W

LANES = 128
SUBLANES = 8
VMEM_LIMIT = 56 * 1024 * 1024

TOKEN_TILE = 256
GLA_CH = 256
ROW_SUB = D_MODEL // LANES
HX_SUB = 2 * ROW_SUB
MOE_ROWS = 256
DIGIT = 32
COMBINE_TILE = 512
COMBINE_WIN = 128
BF16_ROWS = 16


def _cparams(*sem):
    return pltpu.CompilerParams(dimension_semantics=sem, vmem_limit_bytes=VMEM_LIMIT)


def _dot(a, b):
    return jnp.dot(a, b, preferred_element_type=F32)


def _dot_nt(a, b):
    return lax.dot_general(a, b, (((1,), (1,)), ((), ())), preferred_element_type=F32)


def _dot_tn(a, b):
    return lax.dot_general(a, b, (((0,), (0,)), ((), ())), preferred_element_type=F32)


def _split(x):
    hi = x.astype(BF16)
    lo = (x - hi.astype(F32)).astype(BF16)
    return hi, lo


def _sigmoid(x):
    return 1.0 / (1.0 + jnp.exp(-x))


def _rms(x):
    return x * lax.rsqrt(jnp.mean(x * x, axis=-1, keepdims=True) + RMS_EPS)


def _mod_kernel(cond_ref, w_ref, b_ref, o_ref):
    a = cond_ref[...]
    a = a * _sigmoid(a)
    o_ref[0] = _dot(a.astype(BF16), w_ref[0].astype(BF16)) + b_ref[0]


def _modulation(cond, w_mod, b_mod):
    tn = 1536
    nw = N_MOD * D_MODEL
    return pl.pallas_call(
        _mod_kernel,
        grid=(DEPTH, nw // tn),
        in_specs=[pl.BlockSpec((SUBLANES, D_MODEL), lambda l, j: (0, 0)),
                  pl.BlockSpec((1, D_MODEL, tn), lambda l, j: (l, 0, j)),
                  pl.BlockSpec((1, 1, tn), lambda l, j: (l, 0, j))],
        out_specs=pl.BlockSpec((1, SUBLANES, tn), lambda l, j: (l, 0, j)),
        out_shape=jax.ShapeDtypeStruct((DEPTH, SUBLANES, nw), F32),
        compiler_params=_cparams("arbitrary", "arbitrary"),
        name="modulation",
    )(cond, w_mod, b_mod.reshape(DEPTH, 1, nw))


def _inproj_kernel(*refs, mode):
    if mode == "plain":
        x_ref, mod_ref, n1_ref, w_ref = refs[:4]
        outs = refs[4:]
        x = x_ref[...]
    else:
        x_ref, pos_ref, mod_ref, n1_ref, w_ref, xo_ref = refs[:6]
        outs = refs[6:]
        x = x_ref[...] + pos_ref[...]
        xo_ref[...] = x
    qk_ref, v_ref, og_ref, uf_ref, ux_ref, ug_ref, lr_ref, wsc = outs

    @pl.when(pl.program_id(0) == 0)
    def _():
        for r in range(0, D_MODEL, 256):
            wsc[r:r + 256, 0:_C_OG_END] = w_ref[r:r + 256, 0:_C_OG_END].astype(BF16)
            wsc[r:r + 256, _C_OG_END:_C_OG_END + 768] = w_ref[r:r + 256, _C_LR_END:IN_W].astype(BF16)
            lrw = w_ref[r:r + 256, _C_OG_END:_C_LR_END].astype(BF16)
            wsc[r:r + 256, _C_OG_END + 768:_C_OG_END + 896] = jnp.concatenate(
                [lrw, jnp.zeros((256, LANES - GLA_LR_W), BF16)], axis=1)

    m = mod_ref[0]
    h = _rms(x) * n1_ref[...] * (1.0 + m[:, D_MODEL:2 * D_MODEL]) + m[:, 0:D_MODEL]
    hb = h.astype(BF16)
    qk_ref[...] = _dot(hb, wsc[:, 0:512])
    v_ref[...] = _dot(hb, wsc[:, 512:1024])
    og_ref[...] = _dot(hb, wsc[:, 1024:1536])
    uf_ref[...] = _dot(hb, wsc[:, 1536:1792])
    ux_ref[...] = _dot(hb, wsc[:, 1792:2048])
    ug_ref[...] = _dot(hb, wsc[:, 2048:2304])
    lr_ref[...] = _dot(hb, wsc[:, 2304:2432])


def _inproj(mode, x, extra, mod_l, norm1_l, w_in, l, B, L):
    T = B * L
    tm = TOKEN_TILE
    tpb = L // tm
    per_batch = mod_l.shape[0] > 1
    bidx = (lambda i: (i // tpb, 0, 0)) if per_batch else (lambda i: (0, 0, 0))
    row = lambda i: (i, 0)
    in_specs = [pl.BlockSpec((tm, D_MODEL), row)]
    args = [x]
    if mode == "pos":
        in_specs.append(pl.BlockSpec((tm, D_MODEL), lambda i: (i % tpb, 0)))
        args.append(extra)
    in_specs += [pl.BlockSpec((1, 1, N_MOD * D_MODEL), bidx),
                 pl.BlockSpec((1, D_MODEL), lambda i: (0, 0)),
                 pl.BlockSpec((None, D_MODEL, IN_W), lambda i: (l, 0, 0))]
    args += [mod_l, norm1_l.reshape(1, D_MODEL), w_in]
    widths = [512, 512, 512, 256, 256, 256, LANES]
    out_specs = [pl.BlockSpec((tm, w), row) for w in widths]
    out_shape = [jax.ShapeDtypeStruct((T, w), F32) for w in widths]
    if mode != "plain":
        out_specs = [pl.BlockSpec((tm, D_MODEL), row)] + out_specs
        out_shape = [jax.ShapeDtypeStruct((T, D_MODEL), F32)] + out_shape
    res = pl.pallas_call(
        functools.partial(_inproj_kernel, mode=mode),
        grid=(T // tm,),
        in_specs=in_specs,
        out_specs=out_specs,
        out_shape=out_shape,
        scratch_shapes=[pltpu.VMEM((D_MODEL, 2432), BF16)],
        compiler_params=_cparams("arbitrary"),
        name="inproj_" + mode,
    )(*args)
    if mode == "plain":
        return (x,) + tuple(res)
    return tuple(res)


def _gla_kernel(*refs, L, has_s0, want_state):
    qk_ref, v_ref, og_ref, lr_ref, wdec_ref, bdec_ref, gn_ref = refs[:7]
    p = 7
    s0_ref = None
    if has_s0:
        s0_ref = refs[p]
        p += 1
    o_ref = refs[p]
    p += 1
    sn_ref = None
    if want_state:
        sn_ref = refs[p]
        p += 1
    g_scr, oacc, s_scr = refs[p:p + 3]

    ch = GLA_CH
    n_chunks = L // ch
    kw = GLA_QK_W
    vw = GLA_V_W

    z16 = jnp.zeros((GLA_RANK, kw), F32)
    wc = jnp.concatenate([
        jnp.concatenate([wdec_ref[0], z16], axis=1),
        jnp.concatenate([z16, wdec_ref[1]], axis=1),
        jnp.zeros((LANES - GLA_LR_W, 2 * kw), F32)], axis=0).astype(BF16)
    bias = jnp.concatenate([bdec_ref[0], bdec_ref[1]], axis=1)
    z = _dot(lr_ref[0].astype(BF16), wc) + bias
    g_scr[...] = (jnp.minimum(z, 0.0) - jnp.log1p(jnp.exp(-jnp.abs(z)))) * (1.0 / GLA_GATE_NORM)

    row = lax.broadcasted_iota(I32, (ch, ch), 0)
    col = lax.broadcasted_iota(I32, (ch, ch), 1)
    lane_head = lax.broadcasted_iota(I32, (1, kw), 1) // GLA_DK
    blockdiag = (lax.broadcasted_iota(I32, (kw, vw), 0) // GLA_DK) == (lax.broadcasted_iota(I32, (kw, vw), 1) // GLA_DV)
    ones_t = jnp.ones((ch, LANES), BF16)
    gn = gn_ref[...]

    def finish(o, r0):
        parts = []
        for h in range(GLA_HEADS):
            parts.append(_rms(o[:, h * GLA_DV:(h + 1) * GLA_DV]) * gn)
        ogv = og_ref[0, r0:r0 + ch, :]
        return (jnp.concatenate(parts, axis=1) * (ogv * _sigmoid(ogv))).astype(BF16)

    for d in range(2):
        allowed = (col <= row) if d == 0 else (col >= row)
        tri = jnp.where(allowed, 1.0, 0.0).astype(BF16)
        if has_s0:
            s_scr[...] = jnp.zeros((kw, vw), F32)
            for h in range(GLA_HEADS):
                s_scr[h * GLA_DK:(h + 1) * GLA_DK, h * GLA_DV:(h + 1) * GLA_DV] = s0_ref[0, d, h]
        for i in range(n_chunks):
            n = i if d == 0 else n_chunks - 1 - i
            r0 = n * ch
            state_is_zero = (i == 0) and not has_s0
            gch = g_scr[r0:r0 + ch, d * kw:(d + 1) * kw]
            g_hi, g_lo = _split(gch)
            b = _dot(tri, g_hi) + _dot(tri, g_lo)
            b_last = b[ch - 1:ch, :] if d == 0 else b[0:1, :]
            bc = b - b[ch // 2:ch // 2 + 1, :]
            qch = qk_ref[0, r0:r0 + ch, 0:kw] * (GLA_DK ** -0.5)
            kch = qk_ref[0, r0:r0 + ch, kw:2 * kw]
            vb = v_ref[0, r0:r0 + ch, :].astype(BF16)
            q_s = (qch * jnp.exp(bc)).astype(BF16)
            k_s = (kch * jnp.exp(-bc)).astype(BF16)
            zero_q = jnp.zeros_like(q_s)
            qbig = jnp.concatenate([jnp.where(lane_head == h, q_s, zero_q) for h in range(GLA_HEADS)], axis=0)
            scores = _dot_nt(qbig, k_s)
            parts = []
            for h in range(GLA_HEADS):
                ph = jnp.where(allowed, scores[h * ch:(h + 1) * ch, :], 0.0).astype(BF16)
                parts.append(_dot(ph, vb[:, h * GLA_DV:(h + 1) * GLA_DV]))
            o = jnp.concatenate(parts, axis=1)
            if not state_is_zero:
                q_t = (qch * jnp.exp(b)).astype(BF16)
                o = o + _dot(q_t, s_scr[...].astype(BF16))
            if (i < n_chunks - 1) or want_state:
                k_d = (kch * jnp.exp(b_last - b)).astype(BF16)
                ds = jnp.where(blockdiag, _dot_tn(k_d, vb), 0.0)
                if state_is_zero:
                    s_scr[...] = ds
                else:
                    dcol = _dot_tn(g_hi, ones_t) + _dot_tn(g_lo, ones_t)
                    dec = jnp.exp(dcol)
                    s_scr[...] = s_scr[...] * jnp.concatenate([dec] * (vw // LANES), axis=1) + ds
            if d == 0:
                oacc[r0:r0 + ch, :] = o
            else:
                o_ref[0, r0:r0 + ch, :] = finish(o + oacc[r0:r0 + ch, :], r0)
        if want_state:
            for h in range(GLA_HEADS):
                sn_ref[0, d, h] = s_scr[h * GLA_DK:(h + 1) * GLA_DK, h * GLA_DV:(h + 1) * GLA_DV]


def _gla(qk, v, og, lr, wdec_l, bdec_l, gn_l, s0, B, L, want_state):
    has_s0 = s0 is not None
    blk = lambda w: pl.BlockSpec((1, L, w), lambda b: (b, 0, 0))
    in_specs = [blk(512), blk(512), blk(512), blk(LANES),
                pl.BlockSpec((2, GLA_RANK, GLA_QK_W), lambda b: (0, 0, 0)),
                pl.BlockSpec((2, 1, GLA_QK_W), lambda b: (0, 0, 0)),
                pl.BlockSpec((1, GLA_DV), lambda b: (0, 0))]
    args = [qk.reshape(B, L, 512), v.reshape(B, L, 512), og.reshape(B, L, 512), lr.reshape(B, L, LANES),
            wdec_l, bdec_l.reshape(2, 1, GLA_QK_W), gn_l.reshape(1, GLA_DV)]
    st_spec = pl.BlockSpec((1, 2, GLA_HEADS, GLA_DK, GLA_DV), lambda b: (b, 0, 0, 0, 0))
    if has_s0:
        in_specs.append(st_spec)
        args.append(s0)
    out_specs = [pl.BlockSpec((1, L, GLA_V_W), lambda b: (b, 0, 0))]
    out_shape = [jax.ShapeDtypeStruct((B, L, GLA_V_W), BF16)]
    if want_state:
        out_specs.append(st_spec)
        out_shape.append(jax.ShapeDtypeStruct((B, 2, GLA_HEADS, GLA_DK, GLA_DV), F32))
    res = pl.pallas_call(
        functools.partial(_gla_kernel, L=L, has_s0=has_s0, want_state=want_state),
        grid=(B,),
        in_specs=in_specs,
        out_specs=out_specs,
        out_shape=out_shape,
        scratch_shapes=[pltpu.VMEM((L, 2 * GLA_QK_W), F32),
                        pltpu.VMEM((L, GLA_V_W), F32),
                        pltpu.VMEM((GLA_QK_W, GLA_V_W), F32)],
        compiler_params=_cparams("arbitrary"),
        name="gla",
    )(*args)
    o = res[0].reshape(B * L, GLA_V_W)
    return o, (res[1] if want_state else None)


def _fft_tables(L):
    m = np.arange(L, dtype=np.int64)
    ang = 2.0 * np.pi * ((m[:, None] * m[None, :]) % L) / L
    cc = np.concatenate([np.cos(ang), -np.sin(ang)], axis=1)
    c = np.arange(FOURIER_GW, dtype=np.int64)
    angc = 2.0 * np.pi * ((c[:, None] * c[None, :]) % FOURIER_GW) / FOURIER_GW
    scale = 1.0 / math.sqrt(L * FOURIER_GW)
    eye = np.eye(FOURIER_GROUPS)
    bdc = np.kron(eye, np.cos(angc) * scale)
    bds = np.kron(eye, np.sin(angc) * scale)
    return (jnp.asarray(cc, dtype=F32), jnp.asarray(bdc, dtype=F32), jnp.asarray(bds, dtype=F32))


def _fft_kernel(u_ref, cc_ref, bdc_ref, bds_ref, o_ref):
    u_hi, u_lo = _split(u_ref[0])
    bdc = bdc_ref[...].astype(BF16)
    bds = bds_ref[...].astype(BF16)
    uc = _dot(u_hi, bdc) + _dot(u_lo, bdc)
    us = _dot(u_hi, bds) + _dot(u_lo, bds)
    w_hi, w_lo = _split(jnp.concatenate([uc, us], axis=0))
    cc = cc_ref[...].astype(BF16)
    o_ref[0] = (_dot(cc, w_hi) + _dot(cc, w_lo)).astype(BF16)


def _fft(uf, B, L):
    cc, bdc, bds = _fft_tables(L)
    res = pl.pallas_call(
        _fft_kernel,
        grid=(B,),
        in_specs=[pl.BlockSpec((1, L, FOURIER_W), lambda b: (b, 0, 0)),
                  pl.BlockSpec((L, 2 * L), lambda b: (0, 0)),
                  pl.BlockSpec((FOURIER_W, FOURIER_W), lambda b: (0, 0)),
                  pl.BlockSpec((FOURIER_W, FOURIER_W), lambda b: (0, 0))],
        out_specs=pl.BlockSpec((1, L, FOURIER_W), lambda b: (b, 0, 0)),
        out_shape=jax.ShapeDtypeStruct((B, L, FOURIER_W), BF16),
        compiler_params=_cparams("arbitrary"),
        name="fourier",
    )(uf.reshape(B, L, FOURIER_W), cc, bdc, bds)
    return res.reshape(B * L, FOURIER_W)


def _lru_kernel(*refs, L, has_s0, want_state):
    ux_ref, ug_ref, cw_ref, cb_ref, wa_ref, ba_ref, wx_ref, bx_ref, lam_ref = refs[:9]
    p = 9
    s0_ref = None
    if has_s0:
        s0_ref = refs[p]
        p += 1
    o_ref = refs[p]
    p += 1
    sn_ref = None
    if want_state:
        sn_ref = refs[p]
        p += 1
    bd_scr = refs[p]

    @pl.when(pl.program_id(0) == 0)
    def _():
        r = lax.broadcasted_iota(I32, (LRU_BW, LRU_W), 0)
        c = lax.broadcasted_iota(I32, (LRU_BW, LRU_W), 1)
        for d in range(2):
            for gi, w_ref in enumerate((wa_ref, wx_ref)):
                pieces = []
                for h in range(LRU_BLOCKS):
                    place = jnp.where(c == r + h * LRU_BW, 1.0, 0.0).astype(BF16)
                    pieces.append(_dot(w_ref[d, h].astype(BF16), place))
                bd_scr[2 * d + gi] = jnp.concatenate(pieces, axis=0).astype(BF16)

    t = lax.broadcasted_iota(I32, (L, 1), 0)
    x = ux_ref[0]
    xm2 = jnp.where(t >= 2, pltpu.roll(x, 2, 0), 0.0)
    xm1 = jnp.where(t >= 1, pltpu.roll(x, 1, 0), 0.0)
    xp1 = jnp.where(t <= L - 2, pltpu.roll(x, L - 1, 0), 0.0)
    xc = xm2 * cw_ref[0:1, :] + xm1 * cw_ref[1:2, :] + x * cw_ref[2:3, :] + xp1 * cw_ref[3:4, :] + cb_ref[...]
    xcb = xc.astype(BF16)

    hsum = None
    for d in range(2):
        r = _sigmoid(_dot(xcb, bd_scr[2 * d]) + ba_ref[d])
        ig = _sigmoid(_dot(xcb, bd_scr[2 * d + 1]) + bx_ref[d])
        lam = lam_ref[d]
        softplus = jnp.maximum(-lam, 0.0) + jnp.log1p(jnp.exp(-jnp.abs(lam)))
        log_a = -LRU_C * r * softplus
        a = jnp.exp(log_a)
        u = jnp.sqrt(1.0 - jnp.exp(2.0 * log_a)) * (ig * xc)
        if has_s0:
            edge = (t == 0) if d == 0 else (t == L - 1)
            u = u + jnp.where(edge, a * s0_ref[0, d:d + 1, :], 0.0)
        s = 1
        while s < L:
            if d == 0:
                keep = t >= s
                a_sh = jnp.where(keep, pltpu.roll(a, s, 0), 1.0)
                u_sh = jnp.where(keep, pltpu.roll(u, s, 0), 0.0)
            else:
                keep = t < L - s
                a_sh = jnp.where(keep, pltpu.roll(a, L - s, 0), 1.0)
                u_sh = jnp.where(keep, pltpu.roll(u, L - s, 0), 0.0)
            u = a * u_sh + u
            s *= 2
            if s < L:
                a = a * a_sh
        if want_state:
            sn_ref[0, d:d + 1, :] = u[L - 1:L, :] if d == 0 else u[0:1, :]
        hsum = u if hsum is None else hsum + u

    ugv = ug_ref[0]
    gelu = 0.5 * ugv * (1.0 + jnp.tanh(math.sqrt(2.0 / math.pi) * (ugv + 0.044715 * (ugv * ugv * ugv))))
    o_ref[0] = (hsum * gelu).astype(BF16)


def _lru(ux, ug, cw_l, cb_l, wa_l, ba_l, wx_l, bx_l, lam_l, s0, B, L, want_state):
    has_s0 = s0 is not None
    blk = pl.BlockSpec((1, L, LRU_W), lambda b: (b, 0, 0))
    vec2 = pl.BlockSpec((2, 1, LRU_W), lambda b: (0, 0, 0))
    wsp = pl.BlockSpec((2, LRU_BLOCKS, LRU_BW, LRU_BW), lambda b: (0, 0, 0, 0))
    in_specs = [blk, blk,
                pl.BlockSpec((CONV_W, LRU_W), lambda b: (0, 0)),
                pl.BlockSpec((1, LRU_W), lambda b: (0, 0)),
                wsp, vec2, wsp, vec2, vec2]
    args = [ux.reshape(B, L, LRU_W), ug.reshape(B, L, LRU_W), cw_l, cb_l.reshape(1, LRU_W),
            wa_l, ba_l.reshape(2, 1, LRU_W), wx_l, bx_l.reshape(2, 1, LRU_W), lam_l.reshape(2, 1, LRU_W)]
    st_spec = pl.BlockSpec((1, 2, LRU_W), lambda b: (b, 0, 0))
    if has_s0:
        in_specs.append(st_spec)
        args.append(s0)
    out_specs = [blk]
    out_shape = [jax.ShapeDtypeStruct((B, L, LRU_W), BF16)]
    if want_state:
        out_specs.append(st_spec)
        out_shape.append(jax.ShapeDtypeStruct((B, 2, LRU_W), F32))
    res = pl.pallas_call(
        functools.partial(_lru_kernel, L=L, has_s0=has_s0, want_state=want_state),
        grid=(B,),
        in_specs=in_specs,
        out_specs=out_specs,
        out_shape=out_shape,
        scratch_shapes=[pltpu.VMEM((4, LRU_W, LRU_W), BF16)],
        compiler_params=_cparams("arbitrary"),
        name="rglru",
    )(*args)
    return res[0].reshape(B * L, LRU_W), (res[1] if want_state else None)


def _outproj_kernel(og_ref, of_ref, ol_ref, x_ref, mod_ref, n2_ref, wout_ref, wr_ref,
                    x1_ref, hx_ref, afft_ref, wsc, wrs):
    @pl.when(pl.program_id(0) == 0)
    def _():
        for r in range(0, MIX_W, 256):
            wsc[r:r + 256, :] = wout_ref[r:r + 256, :].astype(BF16)
        wrs[...] = jnp.concatenate([wr_ref[...], jnp.zeros((D_MODEL, LANES - N_EXPERTS), F32)], axis=1)

    m = mod_ref[0]
    y = (_dot(og_ref[...], wsc[0:GLA_V_W, :])
         + _dot(of_ref[...], wsc[GLA_V_W:GLA_V_W + FOURIER_W, :])
         + _dot(ol_ref[...], wsc[GLA_V_W + FOURIER_W:MIX_W, :]))
    x1 = x_ref[...] + m[:, 2 * D_MODEL:3 * D_MODEL] * y
    x1_ref[...] = x1
    h2 = _rms(x1) * n2_ref[...] * (1.0 + m[:, 4 * D_MODEL:5 * D_MODEL]) + m[:, 3 * D_MODEL:4 * D_MODEL]
    h_hi = h2.astype(BF16)
    h_hi32 = h_hi.astype(F32)
    for s in range(ROW_SUB):
        hx_ref[:, s, :] = h_hi32[:, s * LANES:(s + 1) * LANES]
    h_lo = (h2 - h_hi32).astype(BF16)
    w_hi, w_lo = _split(wrs[...])
    logits = _dot(h_hi, w_hi) + _dot(h_lo, w_hi) + _dot(h_hi, w_lo)
    lane = lax.broadcasted_iota(I32, logits.shape, 1)
    logits = jnp.where(lane < N_EXPERTS, logits, -jnp.inf)
    ex = jnp.exp(logits - jnp.max(logits, axis=-1, keepdims=True))
    aff = ex / jnp.sum(ex, axis=-1, keepdims=True)
    hx_ref[:, ROW_SUB, :] = aff
    for s in range(ROW_SUB + 1, HX_SUB):
        hx_ref[:, s, :] = jnp.zeros(aff.shape, F32)
    afft_ref[...] = aff.T[0:N_EXPERTS, :]


def _outproj(o_gla, o_fft, o_lru, x, mod_l, norm2_l, w_out, w_router_l, l, B, L):
    T = B * L
    tm = TOKEN_TILE
    tpb = L // tm
    per_batch = mod_l.shape[0] > 1
    bidx = (lambda i: (i // tpb, 0, 0)) if per_batch else (lambda i: (0, 0, 0))
    row = lambda i: (i, 0)
    return pl.pallas_call(
        _outproj_kernel,
        grid=(T // tm,),
        in_specs=[pl.BlockSpec((tm, GLA_V_W), row), pl.BlockSpec((tm, FOURIER_W), row),
                  pl.BlockSpec((tm, LRU_W), row), pl.BlockSpec((tm, D_MODEL), row),
                  pl.BlockSpec((1, 1, N_MOD * D_MODEL), bidx),
                  pl.BlockSpec((1, D_MODEL), lambda i: (0, 0)),
                  pl.BlockSpec((None, MIX_W, D_MODEL), lambda i: (l, 0, 0)),
                  pl.BlockSpec((D_MODEL, N_EXPERTS), lambda i: (0, 0))],
        out_specs=[pl.BlockSpec((tm, D_MODEL), row),
                   pl.BlockSpec((tm, HX_SUB, LANES), lambda i: (i, 0, 0)),
                   pl.BlockSpec((N_EXPERTS, tm), lambda i: (0, i))],
        out_shape=[jax.ShapeDtypeStruct((T, D_MODEL), F32),
                   jax.ShapeDtypeStruct((T, HX_SUB, LANES), F32),
                   jax.ShapeDtypeStruct((N_EXPERTS, T), F32)],
        scratch_shapes=[pltpu.VMEM((MIX_W, D_MODEL), BF16), pltpu.VMEM((D_MODEL, LANES), F32)],
        compiler_params=_cparams("arbitrary"),
        name="outproj",
    )(o_gla, o_fft, o_lru, x, mod_l, norm2_l.reshape(1, D_MODEL), w_out, w_router_l)


def _prefix_lanes(x):
    T = x.shape[1]
    w = 256
    nb = T // w
    stacked = jnp.concatenate([x[:, j * w:(j + 1) * w] for j in range(nb)], axis=0)
    upper = jnp.where(lax.broadcasted_iota(I32, (w, w), 0) <= lax.broadcasted_iota(I32, (w, w), 1), 1.0, 0.0)
    pe = _dot(stacked, upper.astype(BF16))
    carry = jnp.zeros((N_EXPERTS, 1), F32)
    outs = []
    for j in range(nb):
        blk = pe[j * N_EXPERTS:(j + 1) * N_EXPERTS, :]
        outs.append(blk + carry)
        carry = carry + blk[:, w - 1:w]
    return jnp.concatenate(outs, axis=1)


def _topk_kernel(aff_ref, out_ref, slot_ref, bnd_ref, *, T, C):
    n_a = C // DIGIT
    aff = aff_ref[...]
    bits = jnp.zeros((N_EXPERTS, 1), I32)
    for bit in range(30, -1, -1):
        cand = bits | (1 << bit)
        cnt = jnp.sum(jnp.where(aff >= pltpu.bitcast(cand, F32), 1.0, 0.0), axis=1, keepdims=True)
        bits = jnp.where(cnt >= C, cand, bits)
    thr = pltpu.bitcast(bits, F32)
    gt = aff > thr
    eq = aff == thr
    eqf = jnp.where(eq, 1.0, 0.0)
    need = C - jnp.sum(jnp.where(gt, 1.0, 0.0), axis=1, keepdims=True)
    eq_before = _prefix_lanes(eqf.astype(BF16)) - eqf
    sel = gt | (eq & (eq_before < need))
    self32 = jnp.where(sel, 1.0, 0.0)
    cnt = _prefix_lanes(self32.astype(BF16))
    slot_ref[...] = jnp.where(sel, cnt - 1.0, -1.0)
    tok = lax.broadcasted_iota(I32, (1, T), 1)
    lane = lax.broadcasted_iota(I32, (1, LANES), 1)
    bnd = jnp.zeros((N_EXPERTS, LANES), F32)
    for j in range(1, T // COMBINE_TILE + 1):
        before = jnp.sum(jnp.where(tok < j * COMBINE_TILE, self32, 0.0), axis=1, keepdims=True)
        bnd = jnp.where(lane == j, before, bnd)
    bnd_ref[...] = bnd

    p_dig = jnp.floor(cnt * (1.0 / DIGIT))
    q_dig = cnt - DIGIT * p_dig
    a_col = lax.broadcasted_iota(I32, (n_a, 1), 0).astype(F32)
    b_col = lax.broadcasted_iota(I32, (DIGIT, 1), 0).astype(F32)
    kc = min(T, 2048)
    acc = jnp.zeros((N_EXPERTS * n_a, N_EXPERTS * DIGIT), F32)
    for c0 in range(0, T, kc):
        u = jnp.concatenate([jnp.where(p_dig[e:e + 1, c0:c0 + kc] == a_col, 1.0, 0.0).astype(BF16)
                             for e in range(N_EXPERTS)], axis=0)
        v = jnp.concatenate([jnp.where(q_dig[e:e + 1, c0:c0 + kc] <= b_col, 1.0, 0.0).astype(BF16)
                             for e in range(N_EXPERTS)], axis=0)
        acc = acc + _dot_nt(u, v)
    below = jnp.concatenate([jnp.sum(jnp.where(p_dig[e:e + 1, :] < a_col, 1.0, 0.0), axis=1, keepdims=True)
                             for e in range(N_EXPERTS)], axis=0)
    r_i = lax.broadcasted_iota(I32, acc.shape, 0) // n_a
    c_i = lax.broadcasted_iota(I32, acc.shape, 1) // DIGIT
    x = jnp.where(r_i == c_i, acc, 0.0)
    x = x[:, 0:256] + x[:, 256:512]
    x = x[:, 0:LANES] + x[:, LANES:2 * LANES]
    x = x + pltpu.roll(x, 64, 1)
    x = x + pltpu.roll(x, 32, 1)
    out_ref[...] = x + below


def _expert_choice(aff_t, T, C):
    n_a = C // DIGIT
    res = pl.pallas_call(
        functools.partial(_topk_kernel, T=T, C=C),
        grid=(1,),
        in_specs=[pl.BlockSpec((N_EXPERTS, T), lambda i: (0, 0))],
        out_specs=[pl.BlockSpec((N_EXPERTS * n_a, LANES), lambda i: (0, 0)),
                   pl.BlockSpec((N_EXPERTS, T), lambda i: (0, 0)),
                   pl.BlockSpec((N_EXPERTS, LANES), lambda i: (0, 0))],
        out_shape=[jax.ShapeDtypeStruct((N_EXPERTS * n_a, LANES), F32),
                   jax.ShapeDtypeStruct((N_EXPERTS, T), F32),
                   jax.ShapeDtypeStruct((N_EXPERTS, LANES), F32)],
        compiler_params=_cparams("arbitrary"),
        name="expert_choice",
    )(aff_t)
    idx = res[0][:, 0:DIGIT].astype(I32).reshape(N_EXPERTS, C)
    bnd = res[2][:, 0:T // COMBINE_TILE + 1].astype(I32)
    return idx, res[1], bnd


def _ffn_kernel(idx_ref, hx_hbm, wg_ref, wu_ref, wd_ref, y_ref, xbuf, wgb, wub, wdb, sems, *, C):
    e = pl.program_id(0)
    buf = e % 2

    def gather(expert, into):
        def body(c, carry):
            pltpu.make_async_copy(hx_hbm.at[idx_ref[expert, c]], xbuf.at[into, c], sems.at[into]).start()
            return carry
        lax.fori_loop(0, C, body, 0)

    @pl.when(e == 0)
    def _():
        gather(0, 0)

    @pl.when(e + 1 < pl.num_programs(0))
    def _():
        gather(e + 1, 1 - buf)

    for r in range(0, D_MODEL, 256):
        wgb[r:r + 256, :] = wg_ref[r:r + 256, :].astype(BF16)
        wub[r:r + 256, :] = wu_ref[r:r + 256, :].astype(BF16)
        wdb[r:r + 256, :] = wd_ref[r:r + 256, :].astype(BF16)
    pltpu.make_async_copy(hx_hbm.at[pl.ds(0, C)], xbuf.at[buf], sems.at[buf]).wait()

    lane = lax.broadcasted_iota(I32, (1, LANES), 1)
    step = min(MOE_ROWS, C)
    for r0 in range(0, C, step):
        rows = pl.ds(r0, step)
        x = jnp.concatenate([xbuf[buf, rows, s, :] for s in range(ROW_SUB)], axis=1).astype(BF16)
        gate = jnp.sum(jnp.where(lane == e, xbuf[buf, rows, ROW_SUB, :], 0.0), axis=1, keepdims=True)
        g = _dot(x, wgb[...])
        u = _dot(x, wub[...])
        hid = (g * _sigmoid(g) * u).astype(BF16)
        y_ref[rows, :] = (_dot(hid, wdb[...]) * gate).astype(BF16)


def _expert_ffn(idx, hx, wg, wu, wd, l, C):
    wspec = pl.BlockSpec((None, None, D_MODEL, EXPERT_FF), lambda e, idx_ref: (l, e, 0, 0))
    return pl.pallas_call(
        functools.partial(_ffn_kernel, C=C),
        grid_spec=pltpu.PrefetchScalarGridSpec(
            num_scalar_prefetch=1,
            grid=(N_EXPERTS,),
            in_specs=[pl.BlockSpec(memory_space=pl.ANY), wspec, wspec,
                      pl.BlockSpec((None, None, EXPERT_FF, D_MODEL), lambda e, idx_ref: (l, e, 0, 0))],
            out_specs=pl.BlockSpec((None, C, D_MODEL), lambda e, idx_ref: (e, 0, 0)),
            scratch_shapes=[pltpu.VMEM((2, C, HX_SUB, LANES), F32),
                            pltpu.VMEM((D_MODEL, EXPERT_FF), BF16),
                            pltpu.VMEM((D_MODEL, EXPERT_FF), BF16),
                            pltpu.VMEM((EXPERT_FF, D_MODEL), BF16),
                            pltpu.SemaphoreType.DMA((2,))]),
        out_shape=jax.ShapeDtypeStruct((N_EXPERTS, C, D_MODEL), BF16),
        compiler_params=pltpu.CompilerParams(dimension_semantics=("arbitrary",), vmem_limit_bytes=VMEM_LIMIT,
                                             disable_bounds_checks=True),
        name="expert_ffn",
    )(idx, hx, wg, wu, wd)


def _combine_kernel(bnd_ref, slot_ref, y_hbm, x1_ref, gmod_ref, fn_ref, o_ref, ybuf, acc, sem, *, C, final):
    j = pl.program_id(0)
    win = COMBINE_WIN
    tt = slot_ref.shape[1]
    starts = []
    rounds = jnp.int32(1)
    for e in range(N_EXPERTS):
        first = bnd_ref[e, j] & (-BF16_ROWS)
        starts.append(first)
        rounds = jnp.maximum(rounds, lax.div(bnd_ref[e, j + 1] - first + (win - 1), jnp.int32(win)))
    slot_t = jnp.concatenate([slot_ref[...], jnp.zeros((LANES - N_EXPERTS, tt), F32)], axis=0).T.astype(I32)
    w_row = lax.broadcasted_iota(I32, (1, win), 1)
    acc[...] = jnp.zeros(acc.shape, F32)

    def one_round(r, carry):
        lows = [starts[e] + r * win for e in range(N_EXPERTS)]
        fetch = [pl.multiple_of(jnp.minimum(lo, C - win), BF16_ROWS) for lo in lows]
        copies = [pltpu.make_async_copy(y_hbm.at[e, pl.ds(fetch[e], win)], ybuf.at[pl.ds(e * win, win)], sem)
                  for e in range(N_EXPERTS)]
        for cp in copies:
            cp.start()
        cols = []
        for e in range(N_EXPERTS):
            sc = slot_t[:, e:e + 1]
            hit = (sc - fetch[e] == w_row) & (sc >= lows[e]) & (sc < lows[e] + win)
            cols.append(jnp.where(hit, 1.0, 0.0).astype(BF16))
        onehot = jnp.concatenate(cols, axis=1)
        for cp in copies:
            cp.wait()
        acc[...] += _dot(onehot, ybuf[...])
        return carry

    lax.fori_loop(0, rounds, one_round, 0)
    x = x1_ref[...] + gmod_ref[0][:, 5 * D_MODEL:6 * D_MODEL] * acc[...]
    o_ref[...] = _rms(x) * fn_ref[...] if final else x


def _combine(bnd, slot, y, x1, gmod, final_norm, final, B, L, C):
    T = B * L
    tt = COMBINE_TILE
    tpb = L // tt if L >= tt else None
    per_batch = gmod.shape[0] > 1
    if per_batch:
        bidx = lambda j, b: (j // tpb, 0, 0)
    else:
        bidx = lambda j, b: (0, 0, 0)
    return pl.pallas_call(
        functools.partial(_combine_kernel, C=C, final=final),
        grid_spec=pltpu.PrefetchScalarGridSpec(
            num_scalar_prefetch=1,
            grid=(T // tt,),
            in_specs=[pl.BlockSpec((N_EXPERTS, tt), lambda j, b: (0, j)),
                      pl.BlockSpec(memory_space=pl.ANY),
                      pl.BlockSpec((tt, D_MODEL), lambda j, b: (j, 0)),
                      pl.BlockSpec((1, 1, N_MOD * D_MODEL), bidx),
                      pl.BlockSpec((1, D_MODEL), lambda j, b: (0, 0))],
            out_specs=pl.BlockSpec((tt, D_MODEL), lambda j, b: (j, 0)),
            scratch_shapes=[pltpu.VMEM((N_EXPERTS * COMBINE_WIN, D_MODEL), BF16),
                            pltpu.VMEM((tt, D_MODEL), F32),
                            pltpu.SemaphoreType.DMA(())]),
        out_shape=jax.ShapeDtypeStruct((T, D_MODEL), F32),
        compiler_params=_cparams("arbitrary"),
        name="combine_final" if final else "combine",
    )(bnd, slot, y, x1, gmod, final_norm.reshape(1, D_MODEL))


def _grid_position_embedding(n_tokens):
    rows = n_tokens // GRID_W
    r, col = jnp.meshgrid(jnp.arange(rows, dtype=F32), jnp.arange(GRID_W, dtype=F32), indexing="ij")
    n_freq = D_MODEL // 4
    omega = 1.0 / (POS_BASE ** (jnp.arange(n_freq, dtype=F32) / n_freq))
    ar = r.reshape(-1)[:, None] * omega
    ac = col.reshape(-1)[:, None] * omega
    return jnp.concatenate([jnp.sin(ar), jnp.cos(ar), jnp.sin(ac), jnp.cos(ac)], axis=-1)


def _trunk(x_in, pos, mod_g, gla_s0, lru_s0, want_state, w):
    B, L, _ = x_in.shape
    T = B * L
    C = CAPACITY_FACTOR * T // N_EXPERTS
    x = x_in.reshape(T, D_MODEL)
    gla_states = []
    lru_states = []
    for l in range(DEPTH):
        mod_l = mod_g[l]
        mode = "pos" if (l == 0 and pos is not None) else "plain"
        x, qk, v, og, uf, ux, ug, lr = _inproj(mode, x, pos, mod_l, w["norm1"][l], w["w_in"], l, B, L)
        o_gla, gs = _gla(qk, v, og, lr, w["gla_w_decay"][l], w["gla_b_decay"][l], w["gla_norm"][l],
                         None if gla_s0 is None else gla_s0[:, l], B, L, want_state)
        o_fft = _fft(uf, B, L)
        o_lru, ls = _lru(ux, ug, w["lru_conv_w"][l], w["lru_conv_b"][l], w["lru_wa"][l], w["lru_ba"][l],
                         w["lru_wx"][l], w["lru_bx"][l], w["lru_lambda"][l],
                         None if lru_s0 is None else lru_s0[:, l], B, L, want_state)
        x1, hx, aff_t = _outproj(o_gla, o_fft, o_lru, x, mod_l, w["norm2"][l], w["w_out"], w["w_router"][l],
                                 l, B, L)
        idx, slot, bnd = _expert_choice(aff_t, T, C)
        y = _expert_ffn(idx, hx, w["w_expert_gate"], w["w_expert_up"], w["w_expert_down"], l, C)
        x = _combine(bnd, slot, y, x1, mod_l, w["final_norm"], l == DEPTH - 1, B, L, C)
        gla_states.append(gs)
        lru_states.append(ls)
    return x.reshape(B, L, D_MODEL), gla_states, lru_states


def kernel(x_prompt, x_sample, state_gla, state_rglru, c, c_ctx, w_mod, b_mod, norm1, norm2, w_in, gla_w_decay, gla_b_decay, gla_norm, lru_conv_w, lru_conv_b, lru_wa, lru_ba, lru_wx, lru_bx, lru_lambda, w_out, w_router, w_expert_gate, w_expert_up, w_expert_down, final_norm):
    w = dict(norm1=norm1, norm2=norm2, w_in=w_in, gla_w_decay=gla_w_decay, gla_b_decay=gla_b_decay,
             gla_norm=gla_norm, lru_conv_w=lru_conv_w, lru_conv_b=lru_conv_b, lru_wa=lru_wa, lru_ba=lru_ba,
             lru_wx=lru_wx, lru_bx=lru_bx, lru_lambda=lru_lambda, w_out=w_out, w_router=w_router,
             w_expert_gate=w_expert_gate, w_expert_up=w_expert_up, w_expert_down=w_expert_down,
             final_norm=final_norm)
    n_lat = c.shape[0]
    cond = jnp.concatenate([c_ctx[None, :], c, jnp.zeros((SUBLANES - 1 - n_lat, D_MODEL), F32)], axis=0)
    mod = _modulation(cond, w_mod, b_mod)
    mod_ctx = mod[:, 0:1].reshape(DEPTH, 1, 1, N_MOD * D_MODEL)
    mod_lat = mod[:, 1:1 + n_lat].reshape(DEPTH, n_lat, 1, N_MOD * D_MODEL)

    y_prompt, gla_states, lru_states = _trunk(x_prompt, None, mod_ctx, None, None, True, w)
    pos = _grid_position_embedding(x_sample.shape[1])
    y_sample, _, _ = _trunk(x_sample, pos, mod_lat, state_gla, state_rglru, False, w)
    new_state_gla = jnp.stack(gla_states, axis=1)
    new_state_rglru = jnp.stack(lru_states, axis=1)
    return (y_prompt, y_sample, new_state_gla, new_state_rglru)
```

```python
import functools
import math

import numpy as np
import jax
import jax.numpy as jnp
from jax import lax
from jax.experimental import pallas as pl
from jax.experimental.pallas import tpu as pltpu

F32 = jnp.float32
BF16 = jnp.bfloat16
I32 = jnp.int32

D_MODEL = 1024
DEPTH = 4
GRID_W = 64
N_MOD = 6
RMS_EPS = 1e-6
POS_BASE = 10000.0
GLA_HEADS = 4
GLA_DK = 64
GLA_DV = 128
GLA_RANK = 16
GLA_GATE_NORM = 16.0
FOURIER_GROUPS = 4
FOURIER_GW = 64
FOURIER_W = FOURIER_GROUPS * FOURIER_GW
LRU_BLOCKS = 4
LRU_BW = 64
LRU_W = LRU_BLOCKS * LRU_BW
LRU_C = 8.0
CONV_W = 4
N_EXPERTS = 16
EXPERT_FF = 1024
CAPACITY_FACTOR = 2
GLA_QK_W = GLA_HEADS * GLA_DK
GLA_V_W = GLA_HEADS * GLA_DV
GLA_LR_W = 2 * GLA_RANK
MIX_W = GLA_V_W + FOURIER_W + LRU_W
IN_W = 2 * GLA_QK_W + 2 * GLA_V_W + GLA_LR_W + FOURIER_W + 2 * LRU_W
_C_OG_END = 2 * GLA_QK_W + 2 * GLA_V_W
_C_LR_END = _C_OG_END + GLA_LR_W

LANES = 128
SUBLANES = 8
VMEM_LIMIT = 56 * 1024 * 1024

TOKEN_TILE = 512
OUT_TILE = 256
GLA_CH = 256
ROW_SUB = D_MODEL // LANES
MOE_ROWS = 256
DIGIT = 32
COMBINE_TILE = 512
COMBINE_WIN = 128
BF16_ROWS = 16


def _cparams(*sem):
    return pltpu.CompilerParams(dimension_semantics=sem, vmem_limit_bytes=VMEM_LIMIT)


def _dot(a, b):
    return jnp.dot(a, b, preferred_element_type=F32)


def _dot_nt(a, b):
    return lax.dot_general(a, b, (((1,), (1,)), ((), ())), preferred_element_type=F32)


def _dot_tn(a, b):
    return lax.dot_general(a, b, (((0,), (0,)), ((), ())), preferred_element_type=F32)


def _split(x):
    hi = x.astype(BF16)
    lo = (x - hi.astype(F32)).astype(BF16)
    return hi, lo


def _sigmoid(x):
    return 1.0 / (1.0 + jnp.exp(-x))


def _rms(x):
    return x * lax.rsqrt(jnp.mean(x * x, axis=-1, keepdims=True) + RMS_EPS)


def _mod_kernel(cond_ref, w_ref, b_ref, o_ref):
    a = cond_ref[...]
    a = a * _sigmoid(a)
    o_ref[0] = _dot(a.astype(BF16), w_ref[0].astype(BF16)) + b_ref[0]


def _modulation(cond, w_mod, b_mod):
    tn = 1536
    nw = N_MOD * D_MODEL
    return pl.pallas_call(
        _mod_kernel,
        grid=(DEPTH, nw // tn),
        in_specs=[pl.BlockSpec((SUBLANES, D_MODEL), lambda l, j: (0, 0)),
                  pl.BlockSpec((1, D_MODEL, tn), lambda l, j: (l, 0, j)),
                  pl.BlockSpec((1, 1, tn), lambda l, j: (l, 0, j))],
        out_specs=pl.BlockSpec((1, SUBLANES, tn), lambda l, j: (l, 0, j)),
        out_shape=jax.ShapeDtypeStruct((DEPTH, SUBLANES, nw), F32),
        compiler_params=_cparams("arbitrary", "arbitrary"),
        name="modulation",
    )(cond, w_mod, b_mod.reshape(DEPTH, 1, nw))


def _inproj_kernel(*refs, mode):
    if mode == "plain":
        x_ref, mod_ref, n1_ref, w_ref = refs[:4]
        outs = refs[4:]
        x = x_ref[...]
    else:
        x_ref, pos_ref, mod_ref, n1_ref, w_ref, xo_ref = refs[:6]
        outs = refs[6:]
        x = x_ref[...] + pos_ref[...]
        xo_ref[...] = x
    qk_ref, v_ref, og_ref, uf_ref, ux_ref, ug_ref, lr_ref, wsc = outs

    @pl.when(pl.program_id(0) == 0)
    def _():
        for r in range(0, D_MODEL, 256):
            wsc[r:r + 256, 0:_C_OG_END] = w_ref[r:r + 256, 0:_C_OG_END].astype(BF16)
            wsc[r:r + 256, _C_OG_END:_C_OG_END + 768] = w_ref[r:r + 256, _C_LR_END:IN_W].astype(BF16)
            lrw = w_ref[r:r + 256, _C_OG_END:_C_LR_END].astype(BF16)
            wsc[r:r + 256, _C_OG_END + 768:_C_OG_END + 896] = jnp.concatenate(
                [lrw, jnp.zeros((256, LANES - GLA_LR_W), BF16)], axis=1)

    m = mod_ref[0]
    h = _rms(x) * n1_ref[...] * (1.0 + m[:, D_MODEL:2 * D_MODEL]) + m[:, 0:D_MODEL]
    hb = h.astype(BF16)
    qk_ref[...] = _dot(hb, wsc[:, 0:512])
    v_ref[...] = _dot(hb, wsc[:, 512:1024])
    og_ref[...] = _dot(hb, wsc[:, 1024:1536])
    uf_ref[...] = _dot(hb, wsc[:, 1536:1792])
    ux_ref[...] = _dot(hb, wsc[:, 1792:2048])
    ug_ref[...] = _dot(hb, wsc[:, 2048:2304])
    lr_ref[...] = _dot(hb, wsc[:, 2304:2432])


def _inproj(mode, x, extra, mod_l, norm1_l, w_in, l, B, L):
    T = B * L
    tm = TOKEN_TILE
    tpb = L // tm
    per_batch = mod_l.shape[0] > 1
    bidx = (lambda i: (i // tpb, 0, 0)) if per_batch else (lambda i: (0, 0, 0))
    row = lambda i: (i, 0)
    in_specs = [pl.BlockSpec((tm, D_MODEL), row)]
    args = [x]
    if mode == "pos":
        in_specs.append(pl.BlockSpec((tm, D_MODEL), lambda i: (i % tpb, 0)))
        args.append(extra)
    in_specs += [pl.BlockSpec((1, 1, N_MOD * D_MODEL), bidx),
                 pl.BlockSpec((1, D_MODEL), lambda i: (0, 0)),
                 pl.BlockSpec((None, D_MODEL, IN_W), lambda i: (l, 0, 0))]
    args += [mod_l, norm1_l.reshape(1, D_MODEL), w_in]
    widths = [512, 512, 512, 256, 256, 256, LANES]
    out_specs = [pl.BlockSpec((tm, w), row) for w in widths]
    out_shape = [jax.ShapeDtypeStruct((T, w), F32) for w in widths]
    if mode != "plain":
        out_specs = [pl.BlockSpec((tm, D_MODEL), row)] + out_specs
        out_shape = [jax.ShapeDtypeStruct((T, D_MODEL), F32)] + out_shape
    res = pl.pallas_call(
        functools.partial(_inproj_kernel, mode=mode),
        grid=(T // tm,),
        in_specs=in_specs,
        out_specs=out_specs,
        out_shape=out_shape,
        scratch_shapes=[pltpu.VMEM((D_MODEL, 2432), BF16)],
        compiler_params=_cparams("arbitrary"),
        name="inproj_" + mode,
    )(*args)
    if mode == "plain":
        return (x,) + tuple(res)
    return tuple(res)


def _gla_kernel(*refs, L, has_s0, want_state):
    qk_ref, v_ref, og_ref, lr_ref, wdec_ref, bdec_ref, gn_ref = refs[:7]
    p = 7
    s0_ref = None
    if has_s0:
        s0_ref = refs[p]
        p += 1
    o_ref = refs[p]
    p += 1
    sn_ref = None
    if want_state:
        sn_ref = refs[p]
        p += 1
    g_scr, oacc, s_scr = refs[p:p + 3]

    ch = GLA_CH
    n_chunks = L // ch
    kw = GLA_QK_W
    vw = GLA_V_W

    z16 = jnp.zeros((GLA_RANK, kw), F32)
    wc = jnp.concatenate([
        jnp.concatenate([wdec_ref[0], z16], axis=1),
        jnp.concatenate([z16, wdec_ref[1]], axis=1),
        jnp.zeros((LANES - GLA_LR_W, 2 * kw), F32)], axis=0).astype(BF16)
    bias = jnp.concatenate([bdec_ref[0], bdec_ref[1]], axis=1)
    z = _dot(lr_ref[0].astype(BF16), wc) + bias
    g_scr[...] = (jnp.minimum(z, 0.0) - jnp.log1p(jnp.exp(-jnp.abs(z)))) * (1.0 / GLA_GATE_NORM)

    row = lax.broadcasted_iota(I32, (ch, ch), 0)
    col = lax.broadcasted_iota(I32, (ch, ch), 1)
    lane_head = lax.broadcasted_iota(I32, (1, kw), 1) // GLA_DK
    blockdiag = (lax.broadcasted_iota(I32, (kw, vw), 0) // GLA_DK) == (lax.broadcasted_iota(I32, (kw, vw), 1) // GLA_DV)
    ones_t = jnp.ones((ch, LANES), BF16)
    gn = gn_ref[...]

    def finish(o, r0):
        parts = []
        for h in range(GLA_HEADS):
            parts.append(_rms(o[:, h * GLA_DV:(h + 1) * GLA_DV]) * gn)
        ogv = og_ref[0, r0:r0 + ch, :]
        return (jnp.concatenate(parts, axis=1) * (ogv * _sigmoid(ogv))).astype(BF16)

    for d in range(2):
        allowed = (col <= row) if d == 0 else (col >= row)
        tri = jnp.where(allowed, 1.0, 0.0).astype(BF16)
        if has_s0:
            s_scr[...] = jnp.zeros((kw, vw), F32)
            for h in range(GLA_HEADS):
                s_scr[h * GLA_DK:(h + 1) * GLA_DK, h * GLA_DV:(h + 1) * GLA_DV] = s0_ref[0, d, h]
        for i in range(n_chunks):
            n = i if d == 0 else n_chunks - 1 - i
            r0 = n * ch
            state_is_zero = (i == 0) and not has_s0
            gch = g_scr[r0:r0 + ch, d * kw:(d + 1) * kw]
            g_hi, g_lo = _split(gch)
            b = _dot(tri, g_hi) + _dot(tri, g_lo)
            b_last = b[ch - 1:ch, :] if d == 0 else b[0:1, :]
            bc = b - b[ch // 2:ch // 2 + 1, :]
            qch = qk_ref[0, r0:r0 + ch, 0:kw] * (GLA_DK ** -0.5)
            kch = qk_ref[0, r0:r0 + ch, kw:2 * kw]
            vb = v_ref[0, r0:r0 + ch, :].astype(BF16)
            q_s = (qch * jnp.exp(bc)).astype(BF16)
            k_s = (kch * jnp.exp(-bc)).astype(BF16)
            zero_q = jnp.zeros_like(q_s)
            qbig = jnp.concatenate([jnp.where(lane_head == h, q_s, zero_q) for h in range(GLA_HEADS)], axis=0)
            scores = _dot_nt(qbig, k_s)
            parts = []
            for h in range(GLA_HEADS):
                ph = jnp.where(allowed, scores[h * ch:(h + 1) * ch, :], 0.0).astype(BF16)
                parts.append(_dot(ph, vb[:, h * GLA_DV:(h + 1) * GLA_DV]))
            o = jnp.concatenate(parts, axis=1)
            if not state_is_zero:
                q_t = (qch * jnp.exp(b)).astype(BF16)
                o = o + _dot(q_t, s_scr[...].astype(BF16))
            if (i < n_chunks - 1) or want_state:
                k_d = (kch * jnp.exp(b_last - b)).astype(BF16)
                ds = jnp.where(blockdiag, _dot_tn(k_d, vb), 0.0)
                if state_is_zero:
                    s_scr[...] = ds
                else:
                    dcol = _dot_tn(g_hi, ones_t) + _dot_tn(g_lo, ones_t)
                    dec = jnp.exp(dcol)
                    s_scr[...] = s_scr[...] * jnp.concatenate([dec] * (vw // LANES), axis=1) + ds
            if d == 0:
                oacc[r0:r0 + ch, :] = o
            else:
                o_ref[0, r0:r0 + ch, :] = finish(o + oacc[r0:r0 + ch, :], r0)
        if want_state:
            for h in range(GLA_HEADS):
                sn_ref[0, d, h] = s_scr[h * GLA_DK:(h + 1) * GLA_DK, h * GLA_DV:(h + 1) * GLA_DV]


def _gla(qk, v, og, lr, wdec_l, bdec_l, gn_l, s0, B, L, want_state):
    has_s0 = s0 is not None
    blk = lambda w: pl.BlockSpec((1, L, w), lambda b: (b, 0, 0))
    in_specs = [blk(512), blk(512), blk(512), blk(LANES),
                pl.BlockSpec((2, GLA_RANK, GLA_QK_W), lambda b: (0, 0, 0)),
                pl.BlockSpec((2, 1, GLA_QK_W), lambda b: (0, 0, 0)),
                pl.BlockSpec((1, GLA_DV), lambda b: (0, 0))]
    args = [qk.reshape(B, L, 512), v.reshape(B, L, 512), og.reshape(B, L, 512), lr.reshape(B, L, LANES),
            wdec_l, bdec_l.reshape(2, 1, GLA_QK_W), gn_l.reshape(1, GLA_DV)]
    st_spec = pl.BlockSpec((1, 2, GLA_HEADS, GLA_DK, GLA_DV), lambda b: (b, 0, 0, 0, 0))
    if has_s0:
        in_specs.append(st_spec)
        args.append(s0)
    out_specs = [pl.BlockSpec((1, L, GLA_V_W), lambda b: (b, 0, 0))]
    out_shape = [jax.ShapeDtypeStruct((B, L, GLA_V_W), BF16)]
    if want_state:
        out_specs.append(st_spec)
        out_shape.append(jax.ShapeDtypeStruct((B, 2, GLA_HEADS, GLA_DK, GLA_DV), F32))
    res = pl.pallas_call(
        functools.partial(_gla_kernel, L=L, has_s0=has_s0, want_state=want_state),
        grid=(B,),
        in_specs=in_specs,
        out_specs=out_specs,
        out_shape=out_shape,
        scratch_shapes=[pltpu.VMEM((L, 2 * GLA_QK_W), F32),
                        pltpu.VMEM((L, GLA_V_W), F32),
                        pltpu.VMEM((GLA_QK_W, GLA_V_W), F32)],
        compiler_params=_cparams("arbitrary"),
        name="gla",
    )(*args)
    o = res[0].reshape(B * L, GLA_V_W)
    return o, (res[1] if want_state else None)


def _fft_tables(L):
    m = np.arange(L, dtype=np.int64)
    ang = 2.0 * np.pi * ((m[:, None] * m[None, :]) % L) / L
    cc = np.concatenate([np.cos(ang), -np.sin(ang)], axis=1)
    c = np.arange(FOURIER_GW, dtype=np.int64)
    angc = 2.0 * np.pi * ((c[:, None] * c[None, :]) % FOURIER_GW) / FOURIER_GW
    scale = 1.0 / math.sqrt(L * FOURIER_GW)
    eye = np.eye(FOURIER_GROUPS)
    bdc = np.kron(eye, np.cos(angc) * scale)
    bds = np.kron(eye, np.sin(angc) * scale)
    return (jnp.asarray(cc, dtype=F32), jnp.asarray(bdc, dtype=F32), jnp.asarray(bds, dtype=F32))


def _fft_kernel(u_ref, cc_ref, bdc_ref, bds_ref, o_ref):
    u_hi, u_lo = _split(u_ref[0])
    bdc = bdc_ref[...].astype(BF16)
    bds = bds_ref[...].astype(BF16)
    uc = _dot(u_hi, bdc) + _dot(u_lo, bdc)
    us = _dot(u_hi, bds) + _dot(u_lo, bds)
    w_hi, w_lo = _split(jnp.concatenate([uc, us], axis=0))
    cc = cc_ref[...].astype(BF16)
    o_ref[0] = (_dot(cc, w_hi) + _dot(cc, w_lo)).astype(BF16)


def _fft(uf, B, L):
    cc, bdc, bds = _fft_tables(L)
    res = pl.pallas_call(
        _fft_kernel,
        grid=(B,),
        in_specs=[pl.BlockSpec((1, L, FOURIER_W), lambda b: (b, 0, 0)),
                  pl.BlockSpec((L, 2 * L), lambda b: (0, 0)),
                  pl.BlockSpec((FOURIER_W, FOURIER_W), lambda b: (0, 0)),
                  pl.BlockSpec((FOURIER_W, FOURIER_W), lambda b: (0, 0))],
        out_specs=pl.BlockSpec((1, L, FOURIER_W), lambda b: (b, 0, 0)),
        out_shape=jax.ShapeDtypeStruct((B, L, FOURIER_W), BF16),
        compiler_params=_cparams("arbitrary"),
        name="fourier",
    )(uf.reshape(B, L, FOURIER_W), cc, bdc, bds)
    return res.reshape(B * L, FOURIER_W)


def _lru_kernel(*refs, L, has_s0, want_state):
    ux_ref, ug_ref, cw_ref, cb_ref, wa_ref, ba_ref, wx_ref, bx_ref, lam_ref = refs[:9]
    p = 9
    s0_ref = None
    if has_s0:
        s0_ref = refs[p]
        p += 1
    o_ref = refs[p]
    p += 1
    sn_ref = None
    if want_state:
        sn_ref = refs[p]
        p += 1
    bd_scr = refs[p]

    @pl.when(pl.program_id(0) == 0)
    def _():
        r = lax.broadcasted_iota(I32, (LRU_BW, LRU_W), 0)
        c = lax.broadcasted_iota(I32, (LRU_BW, LRU_W), 1)
        for d in range(2):
            for gi, w_ref in enumerate((wa_ref, wx_ref)):
                pieces = []
                for h in range(LRU_BLOCKS):
                    place = jnp.where(c == r + h * LRU_BW, 1.0, 0.0).astype(BF16)
                    pieces.append(_dot(w_ref[d, h].astype(BF16), place))
                bd_scr[2 * d + gi] = jnp.concatenate(pieces, axis=0).astype(BF16)

    t = lax.broadcasted_iota(I32, (L, 1), 0)
    x = ux_ref[0]
    xm2 = jnp.where(t >= 2, pltpu.roll(x, 2, 0), 0.0)
    xm1 = jnp.where(t >= 1, pltpu.roll(x, 1, 0), 0.0)
    xp1 = jnp.where(t <= L - 2, pltpu.roll(x, L - 1, 0), 0.0)
    xc = xm2 * cw_ref[0:1, :] + xm1 * cw_ref[1:2, :] + x * cw_ref[2:3, :] + xp1 * cw_ref[3:4, :] + cb_ref[...]
    xcb = xc.astype(BF16)

    hsum = None
    for d in range(2):
        r = _sigmoid(_dot(xcb, bd_scr[2 * d]) + ba_ref[d])
        ig = _sigmoid(_dot(xcb, bd_scr[2 * d + 1]) + bx_ref[d])
        lam = lam_ref[d]
        softplus = jnp.maximum(-lam, 0.0) + jnp.log1p(jnp.exp(-jnp.abs(lam)))
        log_a = -LRU_C * r * softplus
        a = jnp.exp(log_a)
        u = jnp.sqrt(1.0 - jnp.exp(2.0 * log_a)) * (ig * xc)
        if has_s0:
            edge = (t == 0) if d == 0 else (t == L - 1)
            u = u + jnp.where(edge, a * s0_ref[0, d:d + 1, :], 0.0)
        s = 1
        while s < L:
            if d == 0:
                keep = t >= s
                a_sh = jnp.where(keep, pltpu.roll(a, s, 0), 1.0)
                u_sh = jnp.where(keep, pltpu.roll(u, s, 0), 0.0)
            else:
                keep = t < L - s
                a_sh = jnp.where(keep, pltpu.roll(a, L - s, 0), 1.0)
                u_sh = jnp.where(keep, pltpu.roll(u, L - s, 0), 0.0)
            u = a * u_sh + u
            s *= 2
            if s < L:
                a = a * a_sh
        if want_state:
            sn_ref[0, d:d + 1, :] = u[L - 1:L, :] if d == 0 else u[0:1, :]
        hsum = u if hsum is None else hsum + u

    ugv = ug_ref[0]
    gelu = 0.5 * ugv * (1.0 + jnp.tanh(math.sqrt(2.0 / math.pi) * (ugv + 0.044715 * (ugv * ugv * ugv))))
    o_ref[0] = (hsum * gelu).astype(BF16)


def _lru(ux, ug, cw_l, cb_l, wa_l, ba_l, wx_l, bx_l, lam_l, s0, B, L, want_state):
    has_s0 = s0 is not None
    blk = pl.BlockSpec((1, L, LRU_W), lambda b: (b, 0, 0))
    vec2 = pl.BlockSpec((2, 1, LRU_W), lambda b: (0, 0, 0))
    wsp = pl.BlockSpec((2, LRU_BLOCKS, LRU_BW, LRU_BW), lambda b: (0, 0, 0, 0))
    in_specs = [blk, blk,
                pl.BlockSpec((CONV_W, LRU_W), lambda b: (0, 0)),
                pl.BlockSpec((1, LRU_W), lambda b: (0, 0)),
                wsp, vec2, wsp, vec2, vec2]
    args = [ux.reshape(B, L, LRU_W), ug.reshape(B, L, LRU_W), cw_l, cb_l.reshape(1, LRU_W),
            wa_l, ba_l.reshape(2, 1, LRU_W), wx_l, bx_l.reshape(2, 1, LRU_W), lam_l.reshape(2, 1, LRU_W)]
    st_spec = pl.BlockSpec((1, 2, LRU_W), lambda b: (b, 0, 0))
    if has_s0:
        in_specs.append(st_spec)
        args.append(s0)
    out_specs = [blk]
    out_shape = [jax.ShapeDtypeStruct((B, L, LRU_W), BF16)]
    if want_state:
        out_specs.append(st_spec)
        out_shape.append(jax.ShapeDtypeStruct((B, 2, LRU_W), F32))
    res = pl.pallas_call(
        functools.partial(_lru_kernel, L=L, has_s0=has_s0, want_state=want_state),
        grid=(B,),
        in_specs=in_specs,
        out_specs=out_specs,
        out_shape=out_shape,
        scratch_shapes=[pltpu.VMEM((4, LRU_W, LRU_W), BF16)],
        compiler_params=_cparams("arbitrary"),
        name="rglru",
    )(*args)
    return res[0].reshape(B * L, LRU_W), (res[1] if want_state else None)


def _outproj_kernel(og_ref, of_ref, ol_ref, x_ref, mod_ref, n2_ref, wout_ref, wr_ref,
                    x1_ref, hx_ref, afft_ref, wsc, wrs):
    @pl.when(pl.program_id(0) == 0)
    def _():
        for r in range(0, MIX_W, 256):
            wsc[r:r + 256, :] = wout_ref[r:r + 256, :].astype(BF16)
        wrs[...] = jnp.concatenate([wr_ref[...], jnp.zeros((D_MODEL, LANES - N_EXPERTS), F32)], axis=1)

    m = mod_ref[0]
    y = (_dot(og_ref[...], wsc[0:GLA_V_W, :])
         + _dot(of_ref[...], wsc[GLA_V_W:GLA_V_W + FOURIER_W, :])
         + _dot(ol_ref[...], wsc[GLA_V_W + FOURIER_W:MIX_W, :]))
    x1 = x_ref[...] + m[:, 2 * D_MODEL:3 * D_MODEL] * y
    x1_ref[...] = x1
    h2 = _rms(x1) * n2_ref[...] * (1.0 + m[:, 4 * D_MODEL:5 * D_MODEL]) + m[:, 3 * D_MODEL:4 * D_MODEL]
    h_hi = h2.astype(BF16)
    h_hi32 = h_hi.astype(F32)
    tm = h2.shape[0]
    for s in range(ROW_SUB):
        hx_ref[pl.ds(s, tm, stride=ROW_SUB), :] = h_hi32[:, s * LANES:(s + 1) * LANES]
    h_lo = (h2 - h_hi32).astype(BF16)
    w_hi, w_lo = _split(wrs[...])
    logits = _dot(h_hi, w_hi) + _dot(h_lo, w_hi) + _dot(h_hi, w_lo)
    lane = lax.broadcasted_iota(I32, logits.shape, 1)
    logits = jnp.where(lane < N_EXPERTS, logits, -jnp.inf)
    ex = jnp.exp(logits - jnp.max(logits, axis=-1, keepdims=True))
    aff = ex / jnp.sum(ex, axis=-1, keepdims=True)
    afft_ref[...] = aff.T[0:N_EXPERTS, :]


def _outproj(o_gla, o_fft, o_lru, x, mod_l, norm2_l, w_out, w_router_l, l, B, L):
    T = B * L
    tm = OUT_TILE
    tpb = L // tm
    per_batch = mod_l.shape[0] > 1
    bidx = (lambda i: (i // tpb, 0, 0)) if per_batch else (lambda i: (0, 0, 0))
    row = lambda i: (i, 0)
    return pl.pallas_call(
        _outproj_kernel,
        grid=(T // tm,),
        in_specs=[pl.BlockSpec((tm, GLA_V_W), row), pl.BlockSpec((tm, FOURIER_W), row),
                  pl.BlockSpec((tm, LRU_W), row), pl.BlockSpec((tm, D_MODEL), row),
                  pl.BlockSpec((1, 1, N_MOD * D_MODEL), bidx),
                  pl.BlockSpec((1, D_MODEL), lambda i: (0, 0)),
                  pl.BlockSpec((None, MIX_W, D_MODEL), lambda i: (l, 0, 0)),
                  pl.BlockSpec((D_MODEL, N_EXPERTS), lambda i: (0, 0))],
        out_specs=[pl.BlockSpec((tm, D_MODEL), row),
                   pl.BlockSpec((tm * ROW_SUB, LANES), lambda i: (i, 0)),
                   pl.BlockSpec((N_EXPERTS, tm), lambda i: (0, i))],
        out_shape=[jax.ShapeDtypeStruct((T, D_MODEL), F32),
                   jax.ShapeDtypeStruct((T * ROW_SUB, LANES), F32),
                   jax.ShapeDtypeStruct((N_EXPERTS, T), F32)],
        scratch_shapes=[pltpu.VMEM((MIX_W, D_MODEL), BF16), pltpu.VMEM((D_MODEL, LANES), F32)],
        compiler_params=_cparams("arbitrary"),
        name="outproj",
    )(o_gla, o_fft, o_lru, x, mod_l, norm2_l.reshape(1, D_MODEL), w_out, w_router_l)


def _prefix_lanes(x):
    T = x.shape[1]
    w = 256
    nb = T // w
    stacked = jnp.concatenate([x[:, j * w:(j + 1) * w] for j in range(nb)], axis=0)
    upper = jnp.where(lax.broadcasted_iota(I32, (w, w), 0) <= lax.broadcasted_iota(I32, (w, w), 1), 1.0, 0.0)
    pe = _dot(stacked, upper.astype(BF16))
    carry = jnp.zeros((N_EXPERTS, 1), F32)
    outs = []
    for j in range(nb):
        blk = pe[j * N_EXPERTS:(j + 1) * N_EXPERTS, :]
        outs.append(blk + carry)
        carry = carry + blk[:, w - 1:w]
    return jnp.concatenate(outs, axis=1)


def _topk_kernel(aff_ref, out_ref, gate_ref, slot_ref, bnd_ref, *, T, C):
    n_a = C // DIGIT
    aff = aff_ref[...]
    bits = jnp.zeros((N_EXPERTS, 1), I32)
    for bit in range(30, -1, -1):
        cand = bits | (1 << bit)
        cnt = jnp.sum(jnp.where(aff >= pltpu.bitcast(cand, F32), 1.0, 0.0), axis=1, keepdims=True)
        bits = jnp.where(cnt >= C, cand, bits)
    thr = pltpu.bitcast(bits, F32)
    gt = aff > thr
    eq = aff == thr
    eqf = jnp.where(eq, 1.0, 0.0)
    need = C - jnp.sum(jnp.where(gt, 1.0, 0.0), axis=1, keepdims=True)
    eq_before = _prefix_lanes(eqf.astype(BF16)) - eqf
    sel = gt | (eq & (eq_before < need))
    self32 = jnp.where(sel, 1.0, 0.0)
    cnt = _prefix_lanes(self32.astype(BF16))
    slot_ref[...] = jnp.where(sel, cnt - 1.0, -1.0)
    tok = lax.broadcasted_iota(I32, (1, T), 1)
    lane = lax.broadcasted_iota(I32, (1, LANES), 1)
    bnd = jnp.zeros((N_EXPERTS, LANES), F32)
    for j in range(1, T // COMBINE_TILE + 1):
        before = jnp.sum(jnp.where(tok < j * COMBINE_TILE, self32, 0.0), axis=1, keepdims=True)
        bnd = jnp.where(lane == j, before, bnd)
    bnd_ref[...] = bnd

    p_dig = jnp.floor(cnt * (1.0 / DIGIT))
    q_dig = cnt - DIGIT * p_dig
    a_col = lax.broadcasted_iota(I32, (n_a, 1), 0).astype(F32)
    b_col = lax.broadcasted_iota(I32, (DIGIT, 1), 0).astype(F32)
    slot = cnt - 1.0
    ps_dig = jnp.where(sel, jnp.floor(slot * (1.0 / DIGIT)), -1.0)
    qs_dig = slot - DIGIT * jnp.floor(slot * (1.0 / DIGIT))
    aff_hi = aff.astype(BF16).astype(F32)
    aff_lo = aff - aff_hi
    kc = min(T, 2048)
    acc = jnp.zeros((N_EXPERTS * n_a, N_EXPERTS * DIGIT), F32)
    gacc = jnp.zeros((N_EXPERTS * n_a, N_EXPERTS * DIGIT), F32)
    for c0 in range(0, T, kc):
        tk = slice(c0, c0 + kc)
        u = jnp.concatenate([jnp.where(p_dig[e:e + 1, tk] == a_col, 1.0, 0.0).astype(BF16)
                             for e in range(N_EXPERTS)], axis=0)
        v = jnp.concatenate([jnp.where(q_dig[e:e + 1, tk] <= b_col, 1.0, 0.0).astype(BF16)
                             for e in range(N_EXPERTS)], axis=0)
        acc = acc + _dot_nt(u, v)
        us = jnp.concatenate([jnp.where(ps_dig[e:e + 1, tk] == a_col, 1.0, 0.0).astype(BF16)
                              for e in range(N_EXPERTS)], axis=0)
        for part in (aff_hi, aff_lo):
            vs = jnp.concatenate([jnp.where(qs_dig[e:e + 1, tk] == b_col, part[e:e + 1, tk], 0.0).astype(BF16)
                                  for e in range(N_EXPERTS)], axis=0)
            gacc = gacc + _dot_nt(us, vs)
    below = jnp.concatenate([jnp.sum(jnp.where(p_dig[e:e + 1, :] < a_col, 1.0, 0.0), axis=1, keepdims=True)
                             for e in range(N_EXPERTS)], axis=0)
    r_i = lax.broadcasted_iota(I32, acc.shape, 0) // n_a
    c_i = lax.broadcasted_iota(I32, acc.shape, 1) // DIGIT

    def own_block(x):
        x = jnp.where(r_i == c_i, x, 0.0)
        x = x[:, 0:256] + x[:, 256:512]
        x = x[:, 0:LANES] + x[:, LANES:2 * LANES]
        x = x + pltpu.roll(x, 64, 1)
        return x + pltpu.roll(x, 32, 1)

    out_ref[...] = own_block(acc) + below
    gate_ref[...] = own_block(gacc)


def _expert_choice(aff_t, T, C):
    n_a = C // DIGIT
    res = pl.pallas_call(
        functools.partial(_topk_kernel, T=T, C=C),
        grid=(1,),
        in_specs=[pl.BlockSpec((N_EXPERTS, T), lambda i: (0, 0))],
        out_specs=[pl.BlockSpec((N_EXPERTS * n_a, LANES), lambda i: (0, 0)),
                   pl.BlockSpec((N_EXPERTS * n_a, LANES), lambda i: (0, 0)),
                   pl.BlockSpec((N_EXPERTS, T), lambda i: (0, 0)),
                   pl.BlockSpec((N_EXPERTS, LANES), lambda i: (0, 0))],
        out_shape=[jax.ShapeDtypeStruct((N_EXPERTS * n_a, LANES), F32),
                   jax.ShapeDtypeStruct((N_EXPERTS * n_a, LANES), F32),
                   jax.ShapeDtypeStruct((N_EXPERTS, T), F32),
                   jax.ShapeDtypeStruct((N_EXPERTS, LANES), F32)],
        compiler_params=_cparams("arbitrary"),
        name="expert_choice",
    )(aff_t)
    idx = res[0][:, 0:DIGIT].astype(I32).reshape(N_EXPERTS, C)
    gate = res[1][:, 0:DIGIT].reshape(N_EXPERTS, 1, C)
    bnd = res[3][:, 0:T // COMBINE_TILE + 1].astype(I32)
    return idx, gate, res[2], bnd


def _ffn_kernel(idx_ref, hx_hbm, gate_ref, wg_ref, wu_ref, wd_ref, y_ref, xbuf, wgb, wub, wdb, sems, *, C):
    e = pl.program_id(0)
    buf = e % 2

    def gather(expert, into):
        def body(c, carry):
            src = hx_hbm.at[pl.ds(pl.multiple_of(idx_ref[expert, c] * ROW_SUB, ROW_SUB), ROW_SUB)]
            dst = xbuf.at[into, pl.ds(pl.multiple_of(c * ROW_SUB, ROW_SUB), ROW_SUB)]
            pltpu.make_async_copy(src, dst, sems.at[into]).start()
            return carry
        lax.fori_loop(0, C, body, 0)

    @pl.when(e == 0)
    def _():
        gather(0, 0)

    @pl.when(e + 1 < pl.num_programs(0))
    def _():
        gather(e + 1, 1 - buf)

    for r in range(0, D_MODEL, 256):
        wgb[r:r + 256, :] = wg_ref[r:r + 256, :].astype(BF16)
        wub[r:r + 256, :] = wu_ref[r:r + 256, :].astype(BF16)
        wdb[r:r + 256, :] = wd_ref[r:r + 256, :].astype(BF16)
    pltpu.make_async_copy(hx_hbm.at[pl.ds(0, C * ROW_SUB)], xbuf.at[buf], sems.at[buf]).wait()

    step = min(MOE_ROWS, C)
    for r0 in range(0, C, step):
        x = jnp.concatenate([xbuf[buf, pl.ds(r0 * ROW_SUB + s, step, stride=ROW_SUB), :] for s in range(ROW_SUB)],
                            axis=1).astype(BF16)
        g = _dot(x, wgb[...])
        u = _dot(x, wub[...])
        hid = (g * _sigmoid(g) * u).astype(BF16)
        gate = jnp.broadcast_to(gate_ref[:, r0:r0 + step], (LANES, step)).T[:, 0:1]
        y_ref[pl.ds(r0, step), :] = (_dot(hid, wdb[...]) * gate).astype(BF16)


def _expert_ffn(idx, hx, gate, wg, wu, wd, l, C):
    wspec = pl.BlockSpec((None, None, D_MODEL, EXPERT_FF), lambda e, idx_ref: (l, e, 0, 0))
    return pl.pallas_call(
        functools.partial(_ffn_kernel, C=C),
        grid_spec=pltpu.PrefetchScalarGridSpec(
            num_scalar_prefetch=1,
            grid=(N_EXPERTS,),
            in_specs=[pl.BlockSpec(memory_space=pl.ANY),
                      pl.BlockSpec((None, 1, C), lambda e, idx_ref: (e, 0, 0)), wspec, wspec,
                      pl.BlockSpec((None, None, EXPERT_FF, D_MODEL), lambda e, idx_ref: (l, e, 0, 0))],
            out_specs=pl.BlockSpec((None, C, D_MODEL), lambda e, idx_ref: (e, 0, 0)),
            scratch_shapes=[pltpu.VMEM((2, C * ROW_SUB, LANES), F32),
                            pltpu.VMEM((D_MODEL, EXPERT_FF), BF16),
                            pltpu.VMEM((D_MODEL, EXPERT_FF), BF16),
                            pltpu.VMEM((EXPERT_FF, D_MODEL), BF16),
                            pltpu.SemaphoreType.DMA((2,))]),
        out_shape=jax.ShapeDtypeStruct((N_EXPERTS, C, D_MODEL), BF16),
        compiler_params=pltpu.CompilerParams(dimension_semantics=("arbitrary",), vmem_limit_bytes=VMEM_LIMIT,
                                             disable_bounds_checks=True),
        name="expert_ffn",
    )(idx, hx, gate, wg, wu, wd)


def _combine_kernel(bnd_ref, slot_ref, y_hbm, x1_ref, gmod_ref, fn_ref, o_ref, ybuf, acc, sems, *, C, final):
    j = pl.program_id(0)
    cur = j % 2
    win = COMBINE_WIN

    def window(tile, r):
        lows = [(bnd_ref[e, tile] & (-BF16_ROWS)) + r * win for e in range(N_EXPERTS)]
        fetch = [pl.multiple_of(jnp.minimum(lo, C - win), BF16_ROWS) for lo in lows]
        return lows, fetch

    def copies(fetch, into):
        return [pltpu.make_async_copy(y_hbm.at[e, pl.ds(fetch[e], win)], ybuf.at[into, pl.ds(e * win, win)],
                                      sems.at[into]) for e in range(N_EXPERTS)]

    @pl.when(j == 0)
    def _():
        for cp in copies(window(0, 0)[1], 0):
            cp.start()

    @pl.when(j + 1 < pl.num_programs(0))
    def _():
        for cp in copies(window(j + 1, 0)[1], 1 - cur):
            cp.start()

    w_col = lax.broadcasted_iota(I32, (win, 1), 0)

    def gathered(lows, fetch, into):
        parts = []
        for e in range(N_EXPERTS):
            row = fetch[e] + w_col
            want = jnp.where((row >= lows[e]) & (row < lows[e] + win), row, -2).astype(F32)
            parts.append(jnp.where(slot_ref[e:e + 1, :] == want, 1.0, 0.0).astype(BF16))
        return _dot_tn(jnp.concatenate(parts, axis=0), ybuf[into])

    lows0, fetch0 = window(j, 0)
    for cp in copies(fetch0, cur):
        cp.wait()
    acc[...] = gathered(lows0, fetch0, cur)

    rounds = jnp.int32(1)
    for e in range(N_EXPERTS):
        rounds = jnp.maximum(rounds, lax.div(bnd_ref[e, j + 1] - lows0[e] + (win - 1), jnp.int32(win)))

    def extra_round(r, carry):
        lows, fetch = window(j, r)
        cps = copies(fetch, cur)
        for cp in cps:
            cp.start()
        for cp in cps:
            cp.wait()
        acc[...] += gathered(lows, fetch, cur)
        return carry

    lax.fori_loop(1, rounds, extra_round, 0)
    x = x1_ref[...] + gmod_ref[0][:, 5 * D_MODEL:6 * D_MODEL] * acc[...]
    o_ref[...] = _rms(x) * fn_ref[...] if final else x


def _combine(bnd, slot, y, x1, gmod, final_norm, final, B, L, C):
    T = B * L
    tt = COMBINE_TILE
    tpb = L // tt if L >= tt else None
    per_batch = gmod.shape[0] > 1
    if per_batch:
        bidx = lambda j, b: (j // tpb, 0, 0)
    else:
        bidx = lambda j, b: (0, 0, 0)
    return pl.pallas_call(
        functools.partial(_combine_kernel, C=C, final=final),
        grid_spec=pltpu.PrefetchScalarGridSpec(
            num_scalar_prefetch=1,
            grid=(T // tt,),
            in_specs=[pl.BlockSpec((N_EXPERTS, tt), lambda j, b: (0, j)),
                      pl.BlockSpec(memory_space=pl.ANY),
                      pl.BlockSpec((tt, D_MODEL), lambda j, b: (j, 0)),
                      pl.BlockSpec((1, 1, N_MOD * D_MODEL), bidx),
                      pl.BlockSpec((1, D_MODEL), lambda j, b: (0, 0))],
            out_specs=pl.BlockSpec((tt, D_MODEL), lambda j, b: (j, 0)),
            scratch_shapes=[pltpu.VMEM((2, N_EXPERTS * COMBINE_WIN, D_MODEL), BF16),
                            pltpu.VMEM((tt, D_MODEL), F32),
                            pltpu.SemaphoreType.DMA((2,))]),
        out_shape=jax.ShapeDtypeStruct((T, D_MODEL), F32),
        compiler_params=_cparams("arbitrary"),
        name="combine_final" if final else "combine",
    )(bnd, slot, y, x1, gmod, final_norm.reshape(1, D_MODEL))


def _grid_position_embedding(n_tokens):
    rows = n_tokens // GRID_W
    r, col = jnp.meshgrid(jnp.arange(rows, dtype=F32), jnp.arange(GRID_W, dtype=F32), indexing="ij")
    n_freq = D_MODEL // 4
    omega = 1.0 / (POS_BASE ** (jnp.arange(n_freq, dtype=F32) / n_freq))
    ar = r.reshape(-1)[:, None] * omega
    ac = col.reshape(-1)[:, None] * omega
    return jnp.concatenate([jnp.sin(ar), jnp.cos(ar), jnp.sin(ac), jnp.cos(ac)], axis=-1)


def _trunk(x_in, pos, mod_g, gla_s0, lru_s0, want_state, w):
    B, L, _ = x_in.shape
    T = B * L
    C = CAPACITY_FACTOR * T // N_EXPERTS
    x = x_in.reshape(T, D_MODEL)
    gla_states = []
    lru_states = []
    for l in range(DEPTH):
        mod_l = mod_g[l]
        mode = "pos" if (l == 0 and pos is not None) else "plain"
        x, qk, v, og, uf, ux, ug, lr = _inproj(mode, x, pos, mod_l, w["norm1"][l], w["w_in"], l, B, L)
        o_gla, gs = _gla(qk, v, og, lr, w["gla_w_decay"][l], w["gla_b_decay"][l], w["gla_norm"][l],
                         None if gla_s0 is None else gla_s0[:, l], B, L, want_state)
        o_fft = _fft(uf, B, L)
        o_lru, ls = _lru(ux, ug, w["lru_conv_w"][l], w["lru_conv_b"][l], w["lru_wa"][l], w["lru_ba"][l],
                         w["lru_wx"][l], w["lru_bx"][l], w["lru_lambda"][l],
                         None if lru_s0 is None else lru_s0[:, l], B, L, want_state)
        x1, hx, aff_t = _outproj(o_gla, o_fft, o_lru, x, mod_l, w["norm2"][l], w["w_out"], w["w_router"][l],
                                 l, B, L)
        idx, gate, slot, bnd = _expert_choice(aff_t, T, C)
        y = _expert_ffn(idx, hx, gate, w["w_expert_gate"], w["w_expert_up"], w["w_expert_down"], l, C)
        x = _combine(bnd, slot, y, x1, mod_l, w["final_norm"], l == DEPTH - 1, B, L, C)
        gla_states.append(gs)
        lru_states.append(ls)
    return x.reshape(B, L, D_MODEL), gla_states, lru_states


def kernel(x_prompt, x_sample, state_gla, state_rglru, c, c_ctx, w_mod, b_mod, norm1, norm2, w_in, gla_w_decay, gla_b_decay, gla_norm, lru_conv_w, lru_conv_b, lru_wa, lru_ba, lru_wx, lru_bx, lru_lambda, w_out, w_router, w_expert_gate, w_expert_up, w_expert_down, final_norm):
    w = dict(norm1=norm1, norm2=norm2, w_in=w_in, gla_w_decay=gla_w_decay, gla_b_decay=gla_b_decay,
             gla_norm=gla_norm, lru_conv_w=lru_conv_w, lru_conv_b=lru_conv_b, lru_wa=lru_wa, lru_ba=lru_ba,
             lru_wx=lru_wx, lru_bx=lru_bx, lru_lambda=lru_lambda, w_out=w_out, w_router=w_router,
             w_expert_gate=w_expert_gate, w_expert_up=w_expert_up, w_expert_down=w_expert_down,
             final_norm=final_norm)
    n_lat = c.shape[0]
    cond = jnp.concatenate([c_ctx[None, :], c, jnp.zeros((SUBLANES - 1 - n_lat, D_MODEL), F32)], axis=0)
    mod = _modulation(cond, w_mod, b_mod)
    mod_ctx = mod[:, 0:1].reshape(DEPTH, 1, 1, N_MOD * D_MODEL)
    mod_lat = mod[:, 1:1 + n_lat].reshape(DEPTH, n_lat, 1, N_MOD * D_MODEL)

    y_prompt, gla_states, lru_states = _trunk(x_prompt, None, mod_ctx, None, None, True, w)
    pos = _grid_position_embedding(x_sample.shape[1])
    y_sample, _, _ = _trunk(x_sample, pos, mod_lat, state_gla, state_rglru, False, w)
    new_state_gla = jnp.stack(gla_states, axis=1)
    new_state_rglru = jnp.stack(lru_states, axis=1)
    return (y_prompt, y_sample, new_state_gla, new_state_rglru)
```

```python
import functools
import math

import numpy as np
import jax
import jax.numpy as jnp
from jax import lax
from jax.experimental import pallas as pl
from jax.experimental.pallas import tpu as pltpu

F32 = jnp.float32
BF16 = jnp.bfloat16
I32 = jnp.int32

D_MODEL = 1024
DEPTH = 4
GRID_W = 64
N_MOD = 6
RMS_EPS = 1e-6
POS_BASE = 10000.0
GLA_HEADS = 4
GLA_DK = 64
GLA_DV = 128
GLA_RANK = 16
GLA_GATE_NORM = 16.0
FOURIER_GROUPS = 4
FOURIER_GW = 64
FOURIER_W = FOURIER_GROUPS * FOURIER_GW
LRU_BLOCKS = 4
LRU_BW = 64
LRU_W = LRU_BLOCKS * LRU_BW
LRU_C = 8.0
CONV_W = 4
N_EXPERTS = 16
EXPERT_FF = 1024
CAPACITY_FACTOR = 2
GLA_QK_W = GLA_HEADS * GLA_DK
GLA_V_W = GLA_HEADS * GLA_DV
GLA_LR_W = 2 * GLA_RANK
MIX_W = GLA_V_W + FOURIER_W + LRU_W
IN_W = 2 * GLA_QK_W + 2 * GLA_V_W + GLA_LR_W + FOURIER_W + 2 * LRU_W
_C_OG_END = 2 * GLA_QK_W + 2 * GLA_V_W
_C_LR_END = _C_OG_END + GLA_LR_W

LANES = 128
SUBLANES = 8
VMEM_LIMIT = 56 * 1024 * 1024

TOKEN_TILE = 512
OUT_TILE = 256
GLA_CH = 256
ROW_SUB = D_MODEL // LANES
MOE_ROWS = 256
GATHER_UNROLL = 8
DIGIT = 32
COMBINE_TILE = 512
COMBINE_WIN = 128
BF16_ROWS = 16


def _cparams(*sem):
    return pltpu.CompilerParams(dimension_semantics=sem, vmem_limit_bytes=VMEM_LIMIT)


def _dot(a, b):
    return jnp.dot(a, b, preferred_element_type=F32)


def _dot_nt(a, b):
    return lax.dot_general(a, b, (((1,), (1,)), ((), ())), preferred_element_type=F32)


def _dot_tn(a, b):
    return lax.dot_general(a, b, (((0,), (0,)), ((), ())), preferred_element_type=F32)


def _split(x):
    hi = x.astype(BF16)
    lo = (x - hi.astype(F32)).astype(BF16)
    return hi, lo


def _sigmoid(x):
    return 1.0 / (1.0 + jnp.exp(-x))


def _rms(x):
    return x * lax.rsqrt(jnp.mean(x * x, axis=-1, keepdims=True) + RMS_EPS)


def _mod_kernel(cond_ref, w_ref, b_ref, o_ref):
    a = cond_ref[...]
    a = a * _sigmoid(a)
    o_ref[0] = _dot(a.astype(BF16), w_ref[0].astype(BF16)) + b_ref[0]


def _modulation(cond, w_mod, b_mod):
    tn = 1536
    nw = N_MOD * D_MODEL
    return pl.pallas_call(
        _mod_kernel,
        grid=(DEPTH, nw // tn),
        in_specs=[pl.BlockSpec((SUBLANES, D_MODEL), lambda l, j: (0, 0)),
                  pl.BlockSpec((1, D_MODEL, tn), lambda l, j: (l, 0, j)),
                  pl.BlockSpec((1, 1, tn), lambda l, j: (l, 0, j))],
        out_specs=pl.BlockSpec((1, SUBLANES, tn), lambda l, j: (l, 0, j)),
        out_shape=jax.ShapeDtypeStruct((DEPTH, SUBLANES, nw), F32),
        compiler_params=_cparams("arbitrary", "arbitrary"),
        name="modulation",
    )(cond, w_mod, b_mod.reshape(DEPTH, 1, nw))


def _inproj_kernel(*refs, mode):
    if mode == "plain":
        x_ref, mod_ref, n1_ref, w_ref = refs[:4]
        outs = refs[4:]
        x = x_ref[...]
    else:
        x_ref, pos_ref, mod_ref, n1_ref, w_ref, xo_ref = refs[:6]
        outs = refs[6:]
        x = x_ref[...] + pos_ref[...]
        xo_ref[...] = x
    qk_ref, v_ref, og_ref, uf_ref, ux_ref, ug_ref, lr_ref, wsc = outs

    @pl.when(pl.program_id(0) == 0)
    def _():
        for r in range(0, D_MODEL, 256):
            wsc[r:r + 256, 0:_C_OG_END] = w_ref[r:r + 256, 0:_C_OG_END].astype(BF16)
            wsc[r:r + 256, _C_OG_END:_C_OG_END + 768] = w_ref[r:r + 256, _C_LR_END:IN_W].astype(BF16)
            lrw = w_ref[r:r + 256, _C_OG_END:_C_LR_END].astype(BF16)
            wsc[r:r + 256, _C_OG_END + 768:_C_OG_END + 896] = jnp.concatenate(
                [lrw, jnp.zeros((256, LANES - GLA_LR_W), BF16)], axis=1)

    m = mod_ref[0]
    h = _rms(x) * n1_ref[...] * (1.0 + m[:, D_MODEL:2 * D_MODEL]) + m[:, 0:D_MODEL]
    hb = h.astype(BF16)
    qk_ref[...] = _dot(hb, wsc[:, 0:512])
    v_ref[...] = _dot(hb, wsc[:, 512:1024])
    og_ref[...] = _dot(hb, wsc[:, 1024:1536])
    uf_ref[...] = _dot(hb, wsc[:, 1536:1792])
    ux_ref[...] = _dot(hb, wsc[:, 1792:2048])
    ug_ref[...] = _dot(hb, wsc[:, 2048:2304])
    lr_ref[...] = _dot(hb, wsc[:, 2304:2432])


def _inproj(mode, x, extra, mod_l, norm1_l, w_in, l, B, L):
    T = B * L
    tm = TOKEN_TILE
    tpb = L // tm
    per_batch = mod_l.shape[0] > 1
    bidx = (lambda i: (i // tpb, 0, 0)) if per_batch else (lambda i: (0, 0, 0))
    row = lambda i: (i, 0)
    in_specs = [pl.BlockSpec((tm, D_MODEL), row)]
    args = [x]
    if mode == "pos":
        in_specs.append(pl.BlockSpec((tm, D_MODEL), lambda i: (i % tpb, 0)))
        args.append(extra)
    in_specs += [pl.BlockSpec((1, 1, N_MOD * D_MODEL), bidx),
                 pl.BlockSpec((1, D_MODEL), lambda i: (0, 0)),
                 pl.BlockSpec((None, D_MODEL, IN_W), lambda i: (l, 0, 0))]
    args += [mod_l, norm1_l.reshape(1, D_MODEL), w_in]
    widths = [512, 512, 512, 256, 256, 256, LANES]
    out_specs = [pl.BlockSpec((tm, w), row) for w in widths]
    out_shape = [jax.ShapeDtypeStruct((T, w), F32) for w in widths]
    if mode != "plain":
        out_specs = [pl.BlockSpec((tm, D_MODEL), row)] + out_specs
        out_shape = [jax.ShapeDtypeStruct((T, D_MODEL), F32)] + out_shape
    res = pl.pallas_call(
        functools.partial(_inproj_kernel, mode=mode),
        grid=(T // tm,),
        in_specs=in_specs,
        out_specs=out_specs,
        out_shape=out_shape,
        scratch_shapes=[pltpu.VMEM((D_MODEL, 2432), BF16)],
        compiler_params=_cparams("arbitrary"),
        name="inproj_" + mode,
    )(*args)
    if mode == "plain":
        return (x,) + tuple(res)
    return tuple(res)


def _gla_kernel(*refs, L, has_s0, want_state):
    qk_ref, v_ref, og_ref, lr_ref, wdec_ref, bdec_ref, gn_ref = refs[:7]
    p = 7
    s0_ref = None
    if has_s0:
        s0_ref = refs[p]
        p += 1
    o_ref = refs[p]
    p += 1
    sn_ref = None
    if want_state:
        sn_ref = refs[p]
        p += 1
    g_scr, oacc, s_scr = refs[p:p + 3]

    ch = GLA_CH
    n_chunks = L // ch
    kw = GLA_QK_W
    vw = GLA_V_W

    z16 = jnp.zeros((GLA_RANK, kw), F32)
    wc = jnp.concatenate([
        jnp.concatenate([wdec_ref[0], z16], axis=1),
        jnp.concatenate([z16, wdec_ref[1]], axis=1),
        jnp.zeros((LANES - GLA_LR_W, 2 * kw), F32)], axis=0).astype(BF16)
    bias = jnp.concatenate([bdec_ref[0], bdec_ref[1]], axis=1)
    z = _dot(lr_ref[0].astype(BF16), wc) + bias
    g_scr[...] = (jnp.minimum(z, 0.0) - jnp.log1p(jnp.exp(-jnp.abs(z)))) * (1.0 / GLA_GATE_NORM)

    row = lax.broadcasted_iota(I32, (ch, ch), 0)
    col = lax.broadcasted_iota(I32, (ch, ch), 1)
    lane_head = lax.broadcasted_iota(I32, (1, kw), 1) // GLA_DK
    blockdiag = (lax.broadcasted_iota(I32, (kw, vw), 0) // GLA_DK) == (lax.broadcasted_iota(I32, (kw, vw), 1) // GLA_DV)
    ones_t = jnp.ones((ch, LANES), BF16)
    gn = gn_ref[...]

    def finish(o, r0):
        parts = []
        for h in range(GLA_HEADS):
            parts.append(_rms(o[:, h * GLA_DV:(h + 1) * GLA_DV]) * gn)
        ogv = og_ref[0, r0:r0 + ch, :]
        return (jnp.concatenate(parts, axis=1) * (ogv * _sigmoid(ogv))).astype(BF16)

    for d in range(2):
        allowed = (col <= row) if d == 0 else (col >= row)
        tri = jnp.where(allowed, 1.0, 0.0).astype(BF16)
        if has_s0:
            s_scr[...] = jnp.zeros((kw, vw), F32)
            for h in range(GLA_HEADS):
                s_scr[h * GLA_DK:(h + 1) * GLA_DK, h * GLA_DV:(h + 1) * GLA_DV] = s0_ref[0, d, h]
        for i in range(n_chunks):
            n = i if d == 0 else n_chunks - 1 - i
            r0 = n * ch
            state_is_zero = (i == 0) and not has_s0
            gch = g_scr[r0:r0 + ch, d * kw:(d + 1) * kw]
            g_hi, g_lo = _split(gch)
            b = _dot(tri, g_hi) + _dot(tri, g_lo)
            b_last = b[ch - 1:ch, :] if d == 0 else b[0:1, :]
            bc = b - b[ch // 2:ch // 2 + 1, :]
            qch = qk_ref[0, r0:r0 + ch, 0:kw] * (GLA_DK ** -0.5)
            kch = qk_ref[0, r0:r0 + ch, kw:2 * kw]
            vb = v_ref[0, r0:r0 + ch, :].astype(BF16)
            q_s = (qch * jnp.exp(bc)).astype(BF16)
            k_s = (kch * jnp.exp(-bc)).astype(BF16)
            zero_q = jnp.zeros_like(q_s)
            qbig = jnp.concatenate([jnp.where(lane_head == h, q_s, zero_q) for h in range(GLA_HEADS)], axis=0)
            scores = _dot_nt(qbig, k_s)
            parts = []
            for h in range(GLA_HEADS):
                ph = jnp.where(allowed, scores[h * ch:(h + 1) * ch, :], 0.0).astype(BF16)
                parts.append(_dot(ph, vb[:, h * GLA_DV:(h + 1) * GLA_DV]))
            o = jnp.concatenate(parts, axis=1)
            if not state_is_zero:
                q_t = (qch * jnp.exp(b)).astype(BF16)
                o = o + _dot(q_t, s_scr[...].astype(BF16))
            if (i < n_chunks - 1) or want_state:
                k_d = (kch * jnp.exp(b_last - b)).astype(BF16)
                ds = jnp.where(blockdiag, _dot_tn(k_d, vb), 0.0)
                if state_is_zero:
                    s_scr[...] = ds
                else:
                    dcol = _dot_tn(g_hi, ones_t) + _dot_tn(g_lo, ones_t)
                    dec = jnp.exp(dcol)
                    s_scr[...] = s_scr[...] * jnp.concatenate([dec] * (vw // LANES), axis=1) + ds
            if d == 0:
                oacc[r0:r0 + ch, :] = o
            else:
                o_ref[0, r0:r0 + ch, :] = finish(o + oacc[r0:r0 + ch, :], r0)
        if want_state:
            for h in range(GLA_HEADS):
                sn_ref[0, d, h] = s_scr[h * GLA_DK:(h + 1) * GLA_DK, h * GLA_DV:(h + 1) * GLA_DV]


def _gla(qk, v, og, lr, wdec_l, bdec_l, gn_l, s0, B, L, want_state):
    has_s0 = s0 is not None
    blk = lambda w: pl.BlockSpec((1, L, w), lambda b: (b, 0, 0))
    in_specs = [blk(512), blk(512), blk(512), blk(LANES),
                pl.BlockSpec((2, GLA_RANK, GLA_QK_W), lambda b: (0, 0, 0)),
                pl.BlockSpec((2, 1, GLA_QK_W), lambda b: (0, 0, 0)),
                pl.BlockSpec((1, GLA_DV), lambda b: (0, 0))]
    args = [qk.reshape(B, L, 512), v.reshape(B, L, 512), og.reshape(B, L, 512), lr.reshape(B, L, LANES),
            wdec_l, bdec_l.reshape(2, 1, GLA_QK_W), gn_l.reshape(1, GLA_DV)]
    st_spec = pl.BlockSpec((1, 2, GLA_HEADS, GLA_DK, GLA_DV), lambda b: (b, 0, 0, 0, 0))
    if has_s0:
        in_specs.append(st_spec)
        args.append(s0)
    out_specs = [pl.BlockSpec((1, L, GLA_V_W), lambda b: (b, 0, 0))]
    out_shape = [jax.ShapeDtypeStruct((B, L, GLA_V_W), BF16)]
    if want_state:
        out_specs.append(st_spec)
        out_shape.append(jax.ShapeDtypeStruct((B, 2, GLA_HEADS, GLA_DK, GLA_DV), F32))
    res = pl.pallas_call(
        functools.partial(_gla_kernel, L=L, has_s0=has_s0, want_state=want_state),
        grid=(B,),
        in_specs=in_specs,
        out_specs=out_specs,
        out_shape=out_shape,
        scratch_shapes=[pltpu.VMEM((L, 2 * GLA_QK_W), F32),
                        pltpu.VMEM((L, GLA_V_W), F32),
                        pltpu.VMEM((GLA_QK_W, GLA_V_W), F32)],
        compiler_params=_cparams("arbitrary"),
        name="gla",
    )(*args)
    o = res[0].reshape(B * L, GLA_V_W)
    return o, (res[1] if want_state else None)


def _fft_tables(L):
    m = np.arange(L, dtype=np.int64)
    ang = 2.0 * np.pi * ((m[:, None] * m[None, :]) % L) / L
    cc = np.concatenate([np.cos(ang), -np.sin(ang)], axis=1)
    c = np.arange(FOURIER_GW, dtype=np.int64)
    angc = 2.0 * np.pi * ((c[:, None] * c[None, :]) % FOURIER_GW) / FOURIER_GW
    scale = 1.0 / math.sqrt(L * FOURIER_GW)
    eye = np.eye(FOURIER_GROUPS)
    bdc = np.kron(eye, np.cos(angc) * scale)
    bds = np.kron(eye, np.sin(angc) * scale)
    return (jnp.asarray(cc, dtype=F32), jnp.asarray(bdc, dtype=F32), jnp.asarray(bds, dtype=F32))


def _fft_kernel(u_ref, cc_ref, bdc_ref, bds_ref, o_ref):
    u_hi, u_lo = _split(u_ref[0])
    bdc = bdc_ref[...].astype(BF16)
    bds = bds_ref[...].astype(BF16)
    uc = _dot(u_hi, bdc) + _dot(u_lo, bdc)
    us = _dot(u_hi, bds) + _dot(u_lo, bds)
    w_hi, w_lo = _split(jnp.concatenate([uc, us], axis=0))
    cc = cc_ref[...].astype(BF16)
    o_ref[0] = (_dot(cc, w_hi) + _dot(cc, w_lo)).astype(BF16)


def _fft(uf, B, L):
    cc, bdc, bds = _fft_tables(L)
    res = pl.pallas_call(
        _fft_kernel,
        grid=(B,),
        in_specs=[pl.BlockSpec((1, L, FOURIER_W), lambda b: (b, 0, 0)),
                  pl.BlockSpec((L, 2 * L), lambda b: (0, 0)),
                  pl.BlockSpec((FOURIER_W, FOURIER_W), lambda b: (0, 0)),
                  pl.BlockSpec((FOURIER_W, FOURIER_W), lambda b: (0, 0))],
        out_specs=pl.BlockSpec((1, L, FOURIER_W), lambda b: (b, 0, 0)),
        out_shape=jax.ShapeDtypeStruct((B, L, FOURIER_W), BF16),
        compiler_params=_cparams("arbitrary"),
        name="fourier",
    )(uf.reshape(B, L, FOURIER_W), cc, bdc, bds)
    return res.reshape(B * L, FOURIER_W)


def _lru_kernel(*refs, L, has_s0, want_state):
    ux_ref, ug_ref, cw_ref, cb_ref, wa_ref, ba_ref, wx_ref, bx_ref, lam_ref = refs[:9]
    p = 9
    s0_ref = None
    if has_s0:
        s0_ref = refs[p]
        p += 1
    o_ref = refs[p]
    p += 1
    sn_ref = None
    if want_state:
        sn_ref = refs[p]
        p += 1
    bd_scr = refs[p]
    scan_scr = refs[p + 1:p + 9]

    @pl.when(pl.program_id(0) == 0)
    def _():
        r = lax.broadcasted_iota(I32, (LRU_BW, LRU_W), 0)
        c = lax.broadcasted_iota(I32, (LRU_BW, LRU_W), 1)
        for d in range(2):
            for gi, w_ref in enumerate((wa_ref, wx_ref)):
                pieces = []
                for h in range(LRU_BLOCKS):
                    place = jnp.where(c == r + h * LRU_BW, 1.0, 0.0).astype(BF16)
                    pieces.append(_dot(w_ref[d, h].astype(BF16), place))
                bd_scr[2 * d + gi] = jnp.concatenate(pieces, axis=0).astype(BF16)

    t = lax.broadcasted_iota(I32, (L, 1), 0)
    x = ux_ref[0]
    xm2 = jnp.where(t >= 2, pltpu.roll(x, 2, 0), 0.0)
    xm1 = jnp.where(t >= 1, pltpu.roll(x, 1, 0), 0.0)
    xp1 = jnp.where(t <= L - 2, pltpu.roll(x, L - 1, 0), 0.0)
    xc = xm2 * cw_ref[0:1, :] + xm1 * cw_ref[1:2, :] + x * cw_ref[2:3, :] + xp1 * cw_ref[3:4, :] + cb_ref[...]
    xcb = xc.astype(BF16)

    nb = L // SUBLANES
    pitch = nb + SUBLANES
    n_slab = LRU_W // LANES
    sub = lax.broadcasted_iota(I32, (SUBLANES, LANES), 0)
    hsum = [[None] * n_slab for _ in range(SUBLANES)]
    for d in range(2):
        r = 0.5 + 0.5 * jnp.tanh(0.5 * (_dot(xcb, bd_scr[2 * d]) + ba_ref[d]))
        ig = 0.5 + 0.5 * jnp.tanh(0.5 * (_dot(xcb, bd_scr[2 * d + 1]) + bx_ref[d]))
        lam = lam_ref[d]
        softplus = jnp.maximum(-lam, 0.0) + jnp.log1p(jnp.exp(-jnp.abs(lam)))
        a = jnp.exp(-LRU_C * r * softplus)
        u = jnp.sqrt(1.0 - a * a) * (ig * xc)
        a_scr, u_scr, h_scr, p_scr = scan_scr[4 * d:4 * d + 4]
        for s in range(SUBLANES):
            for k in range(n_slab):
                a_scr[k, s * pitch:s * pitch + nb, :] = a[s * nb:(s + 1) * nb, k * LANES:(k + 1) * LANES]
                u_scr[k, s * pitch:s * pitch + nb, :] = u[s * nb:(s + 1) * nb, k * LANES:(k + 1) * LANES]
        steps = range(nb) if d == 0 else range(nb - 1, -1, -1)
        for k in range(n_slab):
            h = jnp.zeros((SUBLANES, LANES), F32)
            prod = jnp.ones((SUBLANES, LANES), F32)
            for i in steps:
                rows = pl.ds(i, SUBLANES, stride=pitch)
                ai = a_scr[k, rows, :]
                h = ai * h + u_scr[k, rows, :]
                prod = ai * prod
                h_scr[k, rows, :] = h
                p_scr[k, rows, :] = prod
            if has_s0:
                h0 = jnp.broadcast_to(s0_ref[0, d:d + 1, k * LANES:(k + 1) * LANES], (SUBLANES, LANES))
            else:
                h0 = jnp.zeros((SUBLANES, LANES), F32)
            first = 0 if d == 0 else SUBLANES - 1
            carry = jnp.where(sub == first, h0, 0.0)
            for j in range(1, SUBLANES):
                s = j if d == 0 else SUBLANES - 1 - j
                moved = pltpu.roll(prod * carry + h, 1 if d == 0 else SUBLANES - 1, 0)
                carry = jnp.where(sub == s, moved, carry)
            if want_state:
                last = SUBLANES - 1 - first
                sn_ref[0, d:d + 1, k * LANES:(k + 1) * LANES] = (prod * carry + h)[last:last + 1, :]
            for s in range(SUBLANES):
                blk = h_scr[k, s * pitch:s * pitch + nb, :] + p_scr[k, s * pitch:s * pitch + nb, :] * carry[s:s + 1, :]
                hsum[s][k] = blk if hsum[s][k] is None else hsum[s][k] + blk

    for s in range(SUBLANES):
        ugv = ug_ref[0, s * nb:(s + 1) * nb, :]
        gelu = 0.5 * ugv * (1.0 + jnp.tanh(math.sqrt(2.0 / math.pi) * (ugv + 0.044715 * (ugv * ugv * ugv))))
        o_ref[0, s * nb:(s + 1) * nb, :] = (jnp.concatenate(hsum[s], axis=1) * gelu).astype(BF16)


def _lru(ux, ug, cw_l, cb_l, wa_l, ba_l, wx_l, bx_l, lam_l, s0, B, L, want_state):
    has_s0 = s0 is not None
    blk = pl.BlockSpec((1, L, LRU_W), lambda b: (b, 0, 0))
    vec2 = pl.BlockSpec((2, 1, LRU_W), lambda b: (0, 0, 0))
    wsp = pl.BlockSpec((2, LRU_BLOCKS, LRU_BW, LRU_BW), lambda b: (0, 0, 0, 0))
    in_specs = [blk, blk,
                pl.BlockSpec((CONV_W, LRU_W), lambda b: (0, 0)),
                pl.BlockSpec((1, LRU_W), lambda b: (0, 0)),
                wsp, vec2, wsp, vec2, vec2]
    args = [ux.reshape(B, L, LRU_W), ug.reshape(B, L, LRU_W), cw_l, cb_l.reshape(1, LRU_W),
            wa_l, ba_l.reshape(2, 1, LRU_W), wx_l, bx_l.reshape(2, 1, LRU_W), lam_l.reshape(2, 1, LRU_W)]
    st_spec = pl.BlockSpec((1, 2, LRU_W), lambda b: (b, 0, 0))
    if has_s0:
        in_specs.append(st_spec)
        args.append(s0)
    out_specs = [blk]
    out_shape = [jax.ShapeDtypeStruct((B, L, LRU_W), BF16)]
    if want_state:
        out_specs.append(st_spec)
        out_shape.append(jax.ShapeDtypeStruct((B, 2, LRU_W), F32))
    res = pl.pallas_call(
        functools.partial(_lru_kernel, L=L, has_s0=has_s0, want_state=want_state),
        grid=(B,),
        in_specs=in_specs,
        out_specs=out_specs,
        out_shape=out_shape,
        scratch_shapes=[pltpu.VMEM((4, LRU_W, LRU_W), BF16)]
        + [pltpu.VMEM((LRU_W // LANES, L + SUBLANES * SUBLANES, LANES), F32)] * 8,
        compiler_params=_cparams("arbitrary"),
        name="rglru",
    )(*args)
    return res[0].reshape(B * L, LRU_W), (res[1] if want_state else None)


def _outproj_kernel(og_ref, of_ref, ol_ref, x_ref, mod_ref, n2_ref, wout_ref, wr_ref,
                    x1_ref, hx_ref, afft_ref, wsc, wrs):
    @pl.when(pl.program_id(0) == 0)
    def _():
        for r in range(0, MIX_W, 256):
            wsc[r:r + 256, :] = wout_ref[r:r + 256, :].astype(BF16)
        wrs[...] = jnp.concatenate([wr_ref[...], jnp.zeros((D_MODEL, LANES - N_EXPERTS), F32)], axis=1)

    m = mod_ref[0]
    y = (_dot(og_ref[...], wsc[0:GLA_V_W, :])
         + _dot(of_ref[...], wsc[GLA_V_W:GLA_V_W + FOURIER_W, :])
         + _dot(ol_ref[...], wsc[GLA_V_W + FOURIER_W:MIX_W, :]))
    x1 = x_ref[...] + m[:, 2 * D_MODEL:3 * D_MODEL] * y
    x1_ref[...] = x1
    h2 = _rms(x1) * n2_ref[...] * (1.0 + m[:, 4 * D_MODEL:5 * D_MODEL]) + m[:, 3 * D_MODEL:4 * D_MODEL]
    h_hi = h2.astype(BF16)
    h_hi32 = h_hi.astype(F32)
    tm = h2.shape[0]
    for s in range(ROW_SUB):
        hx_ref[pl.ds(s, tm, stride=ROW_SUB), :] = h_hi32[:, s * LANES:(s + 1) * LANES]
    h_lo = (h2 - h_hi32).astype(BF16)
    w_hi, w_lo = _split(wrs[...])
    logits = _dot(h_hi, w_hi) + _dot(h_lo, w_hi) + _dot(h_hi, w_lo)
    lane = lax.broadcasted_iota(I32, logits.shape, 1)
    logits = jnp.where(lane < N_EXPERTS, logits, -jnp.inf)
    ex = jnp.exp(logits - jnp.max(logits, axis=-1, keepdims=True))
    aff = ex / jnp.sum(ex, axis=-1, keepdims=True)
    afft_ref[...] = aff.T[0:N_EXPERTS, :]


def _outproj(o_gla, o_fft, o_lru, x, mod_l, norm2_l, w_out, w_router_l, l, B, L):
    T = B * L
    tm = OUT_TILE
    tpb = L // tm
    per_batch = mod_l.shape[0] > 1
    bidx = (lambda i: (i // tpb, 0, 0)) if per_batch else (lambda i: (0, 0, 0))
    row = lambda i: (i, 0)
    return pl.pallas_call(
        _outproj_kernel,
        grid=(T // tm,),
        in_specs=[pl.BlockSpec((tm, GLA_V_W), row), pl.BlockSpec((tm, FOURIER_W), row),
                  pl.BlockSpec((tm, LRU_W), row), pl.BlockSpec((tm, D_MODEL), row),
                  pl.BlockSpec((1, 1, N_MOD * D_MODEL), bidx),
                  pl.BlockSpec((1, D_MODEL), lambda i: (0, 0)),
                  pl.BlockSpec((None, MIX_W, D_MODEL), lambda i: (l, 0, 0)),
                  pl.BlockSpec((D_MODEL, N_EXPERTS), lambda i: (0, 0))],
        out_specs=[pl.BlockSpec((tm, D_MODEL), row),
                   pl.BlockSpec((tm * ROW_SUB, LANES), lambda i: (i, 0)),
                   pl.BlockSpec((N_EXPERTS, tm), lambda i: (0, i))],
        out_shape=[jax.ShapeDtypeStruct((T, D_MODEL), F32),
                   jax.ShapeDtypeStruct((T * ROW_SUB, LANES), F32),
                   jax.ShapeDtypeStruct((N_EXPERTS, T), F32)],
        scratch_shapes=[pltpu.VMEM((MIX_W, D_MODEL), BF16), pltpu.VMEM((D_MODEL, LANES), F32)],
        compiler_params=_cparams("arbitrary"),
        name="outproj",
    )(o_gla, o_fft, o_lru, x, mod_l, norm2_l.reshape(1, D_MODEL), w_out, w_router_l)


def _prefix_lanes(x):
    T = x.shape[1]
    w = 256
    nb = T // w
    stacked = jnp.concatenate([x[:, j * w:(j + 1) * w] for j in range(nb)], axis=0)
    upper = jnp.where(lax.broadcasted_iota(I32, (w, w), 0) <= lax.broadcasted_iota(I32, (w, w), 1), 1.0, 0.0)
    pe = _dot(stacked, upper.astype(BF16))
    carry = jnp.zeros((N_EXPERTS, 1), F32)
    outs = []
    for j in range(nb):
        blk = pe[j * N_EXPERTS:(j + 1) * N_EXPERTS, :]
        outs.append(blk + carry)
        carry = carry + blk[:, w - 1:w]
    return jnp.concatenate(outs, axis=1)


def _topk_kernel(aff_ref, out_ref, gate_ref, slot_ref, bnd_ref, *, T, C):
    n_a = C // DIGIT
    aff = aff_ref[...]
    bits = jnp.zeros((N_EXPERTS, 1), I32)
    for bit in range(30, -1, -1):
        cand = bits | (1 << bit)
        cnt = jnp.sum(jnp.where(aff >= pltpu.bitcast(cand, F32), 1.0, 0.0), axis=1, keepdims=True)
        bits = jnp.where(cnt >= C, cand, bits)
    thr = pltpu.bitcast(bits, F32)
    gt = aff > thr
    eq = aff == thr
    eqf = jnp.where(eq, 1.0, 0.0)
    need = C - jnp.sum(jnp.where(gt, 1.0, 0.0), axis=1, keepdims=True)
    eq_before = _prefix_lanes(eqf.astype(BF16)) - eqf
    sel = gt | (eq & (eq_before < need))
    self32 = jnp.where(sel, 1.0, 0.0)
    cnt = _prefix_lanes(self32.astype(BF16))
    slot_ref[...] = jnp.where(sel, cnt - 1.0, -1.0)
    tok = lax.broadcasted_iota(I32, (1, T), 1)
    lane = lax.broadcasted_iota(I32, (1, LANES), 1)
    bnd = jnp.zeros((N_EXPERTS, LANES), F32)
    for j in range(1, T // COMBINE_TILE + 1):
        before = jnp.sum(jnp.where(tok < j * COMBINE_TILE, self32, 0.0), axis=1, keepdims=True)
        bnd = jnp.where(lane == j, before, bnd)
    bnd_ref[...] = bnd

    p_dig = jnp.floor(cnt * (1.0 / DIGIT))
    q_dig = cnt - DIGIT * p_dig
    a_col = lax.broadcasted_iota(I32, (n_a, 1), 0).astype(F32)
    b_col = lax.broadcasted_iota(I32, (DIGIT, 1), 0).astype(F32)
    slot = cnt - 1.0
    ps_dig = jnp.where(sel, jnp.floor(slot * (1.0 / DIGIT)), -1.0)
    qs_dig = slot - DIGIT * jnp.floor(slot * (1.0 / DIGIT))
    aff_hi = aff.astype(BF16).astype(F32)
    aff_lo = aff - aff_hi
    kc = min(T, 2048)
    acc = jnp.zeros((N_EXPERTS * n_a, N_EXPERTS * DIGIT), F32)
    gacc = jnp.zeros((N_EXPERTS * n_a, N_EXPERTS * DIGIT), F32)
    for c0 in range(0, T, kc):
        tk = slice(c0, c0 + kc)
        u = jnp.concatenate([jnp.where(p_dig[e:e + 1, tk] == a_col, 1.0, 0.0).astype(BF16)
                             for e in range(N_EXPERTS)], axis=0)
        v = jnp.concatenate([jnp.where(q_dig[e:e + 1, tk] <= b_col, 1.0, 0.0).astype(BF16)
                             for e in range(N_EXPERTS)], axis=0)
        acc = acc + _dot_nt(u, v)
        us = jnp.concatenate([jnp.where(ps_dig[e:e + 1, tk] == a_col, 1.0, 0.0).astype(BF16)
                              for e in range(N_EXPERTS)], axis=0)
        for part in (aff_hi, aff_lo):
            vs = jnp.concatenate([jnp.where(qs_dig[e:e + 1, tk] == b_col, part[e:e + 1, tk], 0.0).astype(BF16)
                                  for e in range(N_EXPERTS)], axis=0)
            gacc = gacc + _dot_nt(us, vs)
    below = jnp.concatenate([jnp.sum(jnp.where(p_dig[e:e + 1, :] < a_col, 1.0, 0.0), axis=1, keepdims=True)
                             for e in range(N_EXPERTS)], axis=0)
    r_i = lax.broadcasted_iota(I32, acc.shape, 0) // n_a
    c_i = lax.broadcasted_iota(I32, acc.shape, 1) // DIGIT

    def own_block(x):
        x = jnp.where(r_i == c_i, x, 0.0)
        x = x[:, 0:256] + x[:, 256:512]
        x = x[:, 0:LANES] + x[:, LANES:2 * LANES]
        x = x + pltpu.roll(x, 64, 1)
        return x + pltpu.roll(x, 32, 1)

    out_ref[...] = own_block(acc) + below
    gate_ref[...] = own_block(gacc)


def _expert_choice(aff_t, T, C):
    n_a = C // DIGIT
    res = pl.pallas_call(
        functools.partial(_topk_kernel, T=T, C=C),
        grid=(1,),
        in_specs=[pl.BlockSpec((N_EXPERTS, T), lambda i: (0, 0))],
        out_specs=[pl.BlockSpec((N_EXPERTS * n_a, LANES), lambda i: (0, 0)),
                   pl.BlockSpec((N_EXPERTS * n_a, LANES), lambda i: (0, 0)),
                   pl.BlockSpec((N_EXPERTS, T), lambda i: (0, 0)),
                   pl.BlockSpec((N_EXPERTS, LANES), lambda i: (0, 0))],
        out_shape=[jax.ShapeDtypeStruct((N_EXPERTS * n_a, LANES), F32),
                   jax.ShapeDtypeStruct((N_EXPERTS * n_a, LANES), F32),
                   jax.ShapeDtypeStruct((N_EXPERTS, T), F32),
                   jax.ShapeDtypeStruct((N_EXPERTS, LANES), F32)],
        compiler_params=_cparams("arbitrary"),
        name="expert_choice",
    )(aff_t)
    idx = res[0][:, 0:DIGIT].astype(I32).reshape(N_EXPERTS, C)
    gate = res[1][:, 0:DIGIT].reshape(N_EXPERTS, 1, C)
    bnd = res[3][:, 0:T // COMBINE_TILE + 1].astype(I32)
    return idx, gate, res[2], bnd


def _ffn_kernel(idx_ref, hx_hbm, gate_ref, wg_ref, wu_ref, wd_ref, y_ref, xbuf, wgb, wub, wdb, sems, *, C):
    e = pl.program_id(0)
    buf = e % 2

    def gather(expert, into):
        def body(i, carry):
            for k in range(GATHER_UNROLL):
                c = i * GATHER_UNROLL + k
                src = hx_hbm.at[pl.ds(pl.multiple_of(idx_ref[expert, c] * ROW_SUB, ROW_SUB), ROW_SUB)]
                dst = xbuf.at[into, pl.ds(pl.multiple_of(c * ROW_SUB, ROW_SUB), ROW_SUB)]
                pltpu.make_async_copy(src, dst, sems.at[into]).start(priority=k % 2)
            return carry
        lax.fori_loop(0, C // GATHER_UNROLL, body, 0)

    @pl.when(e == 0)
    def _():
        gather(0, 0)

    @pl.when(e + 1 < pl.num_programs(0))
    def _():
        gather(e + 1, 1 - buf)

    for r in range(0, D_MODEL, 256):
        wgb[r:r + 256, :] = wg_ref[r:r + 256, :].astype(BF16)
        wub[r:r + 256, :] = wu_ref[r:r + 256, :].astype(BF16)
        wdb[r:r + 256, :] = wd_ref[r:r + 256, :].astype(BF16)
    pltpu.make_async_copy(hx_hbm.at[pl.ds(0, C * ROW_SUB)], xbuf.at[buf], sems.at[buf]).wait()

    step = min(MOE_ROWS, C)
    for r0 in range(0, C, step):
        x = jnp.concatenate([xbuf[buf, pl.ds(r0 * ROW_SUB + s, step, stride=ROW_SUB), :] for s in range(ROW_SUB)],
                            axis=1).astype(BF16)
        g = _dot(x, wgb[...])
        u = _dot(x, wub[...])
        hid = (g * _sigmoid(g) * u).astype(BF16)
        gate = jnp.broadcast_to(gate_ref[:, r0:r0 + step], (LANES, step)).T[:, 0:1]
        y_ref[pl.ds(r0, step), :] = (_dot(hid, wdb[...]) * gate).astype(BF16)


def _expert_ffn(idx, hx, gate, wg, wu, wd, l, C):
    wspec = pl.BlockSpec((None, None, D_MODEL, EXPERT_FF), lambda e, idx_ref: (l, e, 0, 0))
    return pl.pallas_call(
        functools.partial(_ffn_kernel, C=C),
        grid_spec=pltpu.PrefetchScalarGridSpec(
            num_scalar_prefetch=1,
            grid=(N_EXPERTS,),
            in_specs=[pl.BlockSpec(memory_space=pl.ANY),
                      pl.BlockSpec((None, 1, C), lambda e, idx_ref: (e, 0, 0)), wspec, wspec,
                      pl.BlockSpec((None, None, EXPERT_FF, D_MODEL), lambda e, idx_ref: (l, e, 0, 0))],
            out_specs=pl.BlockSpec((None, C, D_MODEL), lambda e, idx_ref: (e, 0, 0)),
            scratch_shapes=[pltpu.VMEM((2, C * ROW_SUB, LANES), F32),
                            pltpu.VMEM((D_MODEL, EXPERT_FF), BF16),
                            pltpu.VMEM((D_MODEL, EXPERT_FF), BF16),
                            pltpu.VMEM((EXPERT_FF, D_MODEL), BF16),
                            pltpu.SemaphoreType.DMA((2,))]),
        out_shape=jax.ShapeDtypeStruct((N_EXPERTS, C, D_MODEL), BF16),
        compiler_params=pltpu.CompilerParams(dimension_semantics=("arbitrary",), vmem_limit_bytes=VMEM_LIMIT,
                                             disable_bounds_checks=True),
        name="expert_ffn",
    )(idx, hx, gate, wg, wu, wd)


def _combine_kernel(bnd_ref, slot_ref, y_hbm, x1_ref, gmod_ref, fn_ref, o_ref, ybuf, acc, sems, *, C, final):
    j = pl.program_id(0)
    cur = j % 2
    win = COMBINE_WIN

    def window(tile, r):
        lows = [(bnd_ref[e, tile] & (-BF16_ROWS)) + r * win for e in range(N_EXPERTS)]
        fetch = [pl.multiple_of(jnp.minimum(lo, C - win), BF16_ROWS) for lo in lows]
        return lows, fetch

    def copies(fetch, into):
        return [pltpu.make_async_copy(y_hbm.at[e, pl.ds(fetch[e], win)], ybuf.at[into, pl.ds(e * win, win)],
                                      sems.at[into]) for e in range(N_EXPERTS)]

    @pl.when(j == 0)
    def _():
        for cp in copies(window(0, 0)[1], 0):
            cp.start()

    @pl.when(j + 1 < pl.num_programs(0))
    def _():
        for cp in copies(window(j + 1, 0)[1], 1 - cur):
            cp.start()

    w_col = lax.broadcasted_iota(I32, (win, 1), 0)

    def gathered(lows, fetch, into):
        parts = []
        for e in range(N_EXPERTS):
            row = fetch[e] + w_col
            want = jnp.where((row >= lows[e]) & (row < lows[e] + win), row, -2).astype(F32)
            parts.append(jnp.where(slot_ref[e:e + 1, :] == want, 1.0, 0.0).astype(BF16))
        return _dot_tn(jnp.concatenate(parts, axis=0), ybuf[into])

    lows0, fetch0 = window(j, 0)
    for cp in copies(fetch0, cur):
        cp.wait()
    acc[...] = gathered(lows0, fetch0, cur)

    rounds = jnp.int32(1)
    for e in range(N_EXPERTS):
        rounds = jnp.maximum(rounds, lax.div(bnd_ref[e, j + 1] - lows0[e] + (win - 1), jnp.int32(win)))

    def extra_round(r, carry):
        lows, fetch = window(j, r)
        cps = copies(fetch, cur)
        for cp in cps:
            cp.start()
        for cp in cps:
            cp.wait()
        acc[...] += gathered(lows, fetch, cur)
        return carry

    lax.fori_loop(1, rounds, extra_round, 0)
    x = x1_ref[...] + gmod_ref[0][:, 5 * D_MODEL:6 * D_MODEL] * acc[...]
    o_ref[...] = _rms(x) * fn_ref[...] if final else x


def _combine(bnd, slot, y, x1, gmod, final_norm, final, B, L, C):
    T = B * L
    tt = COMBINE_TILE
    tpb = L // tt if L >= tt else None
    per_batch = gmod.shape[0] > 1
    if per_batch:
        bidx = lambda j, b: (j // tpb, 0, 0)
    else:
        bidx = lambda j, b: (0, 0, 0)
    return pl.pallas_call(
        functools.partial(_combine_kernel, C=C, final=final),
        grid_spec=pltpu.PrefetchScalarGridSpec(
            num_scalar_prefetch=1,
            grid=(T // tt,),
            in_specs=[pl.BlockSpec((N_EXPERTS, tt), lambda j, b: (0, j)),
                      pl.BlockSpec(memory_space=pl.ANY),
                      pl.BlockSpec((tt, D_MODEL), lambda j, b: (j, 0)),
                      pl.BlockSpec((1, 1, N_MOD * D_MODEL), bidx),
                      pl.BlockSpec((1, D_MODEL), lambda j, b: (0, 0))],
            out_specs=pl.BlockSpec((tt, D_MODEL), lambda j, b: (j, 0)),
            scratch_shapes=[pltpu.VMEM((2, N_EXPERTS * COMBINE_WIN, D_MODEL), BF16),
                            pltpu.VMEM((tt, D_MODEL), F32),
                            pltpu.SemaphoreType.DMA((2,))]),
        out_shape=jax.ShapeDtypeStruct((T, D_MODEL), F32),
        compiler_params=_cparams("arbitrary"),
        name="combine_final" if final else "combine",
    )(bnd, slot, y, x1, gmod, final_norm.reshape(1, D_MODEL))


def _grid_position_embedding(n_tokens):
    rows = n_tokens // GRID_W
    r, col = jnp.meshgrid(jnp.arange(rows, dtype=F32), jnp.arange(GRID_W, dtype=F32), indexing="ij")
    n_freq = D_MODEL // 4
    omega = 1.0 / (POS_BASE ** (jnp.arange(n_freq, dtype=F32) / n_freq))
    ar = r.reshape(-1)[:, None] * omega
    ac = col.reshape(-1)[:, None] * omega
    return jnp.concatenate([jnp.sin(ar), jnp.cos(ar), jnp.sin(ac), jnp.cos(ac)], axis=-1)


def _trunk(x_in, pos, mod_g, gla_s0, lru_s0, want_state, w):
    B, L, _ = x_in.shape
    T = B * L
    C = CAPACITY_FACTOR * T // N_EXPERTS
    x = x_in.reshape(T, D_MODEL)
    gla_states = []
    lru_states = []
    for l in range(DEPTH):
        mod_l = mod_g[l]
        mode = "pos" if (l == 0 and pos is not None) else "plain"
        x, qk, v, og, uf, ux, ug, lr = _inproj(mode, x, pos, mod_l, w["norm1"][l], w["w_in"], l, B, L)
        o_gla, gs = _gla(qk, v, og, lr, w["gla_w_decay"][l], w["gla_b_decay"][l], w["gla_norm"][l],
                         None if gla_s0 is None else gla_s0[:, l], B, L, want_state)
        o_fft = _fft(uf, B, L)
        o_lru, ls = _lru(ux, ug, w["lru_conv_w"][l], w["lru_conv_b"][l], w["lru_wa"][l], w["lru_ba"][l],
                         w["lru_wx"][l], w["lru_bx"][l], w["lru_lambda"][l],
                         None if lru_s0 is None else lru_s0[:, l], B, L, want_state)
        x1, hx, aff_t = _outproj(o_gla, o_fft, o_lru, x, mod_l, w["norm2"][l], w["w_out"], w["w_router"][l],
                                 l, B, L)
        idx, gate, slot, bnd = _expert_choice(aff_t, T, C)
        y = _expert_ffn(idx, hx, gate, w["w_expert_gate"], w["w_expert_up"], w["w_expert_down"], l, C)
        x = _combine(bnd, slot, y, x1, mod_l, w["final_norm"], l == DEPTH - 1, B, L, C)
        gla_states.append(gs)
        lru_states.append(ls)
    return x.reshape(B, L, D_MODEL), gla_states, lru_states


def kernel(x_prompt, x_sample, state_gla, state_rglru, c, c_ctx, w_mod, b_mod, norm1, norm2, w_in, gla_w_decay, gla_b_decay, gla_norm, lru_conv_w, lru_conv_b, lru_wa, lru_ba, lru_wx, lru_bx, lru_lambda, w_out, w_router, w_expert_gate, w_expert_up, w_expert_down, final_norm):
    w = dict(norm1=norm1, norm2=norm2, w_in=w_in, gla_w_decay=gla_w_decay, gla_b_decay=gla_b_decay,
             gla_norm=gla_norm, lru_conv_w=lru_conv_w, lru_conv_b=lru_conv_b, lru_wa=lru_wa, lru_ba=lru_ba,
             lru_wx=lru_wx, lru_bx=lru_bx, lru_lambda=lru_lambda, w_out=w_out, w_router=w_router,
             w_expert_gate=w_expert_gate, w_expert_up=w_expert_up, w_expert_down=w_expert_down,
             final_norm=final_norm)
    n_lat = c.shape[0]
    cond = jnp.concatenate([c_ctx[None, :], c, jnp.zeros((SUBLANES - 1 - n_lat, D_MODEL), F32)], axis=0)
    mod = _modulation(cond, w_mod, b_mod)
    mod_ctx = mod[:, 0:1].reshape(DEPTH, 1, 1, N_MOD * D_MODEL)
    mod_lat = mod[:, 1:1 + n_lat].reshape(DEPTH, n_lat, 1, N_MOD * D_MODEL)

    y_prompt, gla_states, lru_states = _trunk(x_prompt, None, mod_ctx, None, None, True, w)
    pos = _grid_position_embedding(x_sample.shape[1])
    y_sample, _, _ = _trunk(x_sample, pos, mod_lat, state_gla, state_rglru, False, w)
    new_state_gla = jnp.stack(gla_states, axis=1)
    new_state_rglru = jnp.stack(lru_states, axis=1)
    return (y_prompt, y_sample, new_state_gla, new_state_rglru)
```

```python
import functools
import math

import numpy as np
import jax
import jax.numpy as jnp
from jax import lax
from jax.experimental import pallas as pl
from jax.experimental.pallas import tpu as pltpu

F32 = jnp.float32
BF16 = jnp.bfloat16
I32 = jnp.int32

D_MODEL = 1024
DEPTH = 4
GRID_W = 64
N_MOD = 6
RMS_EPS = 1e-6
POS_BASE = 10000.0
GLA_HEADS = 4
GLA_DK = 64
GLA_DV = 128
GLA_RANK = 16
GLA_GATE_NORM = 16.0
FOURIER_GROUPS = 4
FOURIER_GW = 64
FOURIER_W = FOURIER_GROUPS * FOURIER_GW
LRU_BLOCKS = 4
LRU_BW = 64
LRU_W = LRU_BLOCKS * LRU_BW
LRU_C = 8.0
CONV_W = 4
N_EXPERTS = 16
EXPERT_FF = 1024
CAPACITY_FACTOR = 2
GLA_QK_W = GLA_HEADS * GLA_DK
GLA_V_W = GLA_HEADS * GLA_DV
GLA_LR_W = 2 * GLA_RANK
MIX_W = GLA_V_W + FOURIER_W + LRU_W
IN_W = 2 * GLA_QK_W + 2 * GLA_V_W + GLA_LR_W + FOURIER_W + 2 * LRU_W
_C_OG_END = 2 * GLA_QK_W + 2 * GLA_V_W
_C_LR_END = _C_OG_END + GLA_LR_W

LANES = 128
SUBLANES = 8
VMEM_LIMIT = 56 * 1024 * 1024

TOKEN_TILE = 512
OUT_TILE = 256
GLA_CH = 256
ROW_SUB = D_MODEL // LANES
MOE_ROWS = 256
GATHER_UNROLL = 8
DIGIT = 32
COMBINE_TILE = 512
COMBINE_CHUNK = 64
COMBINE_DEPTH = 1024
BF16_ROWS = 16


def _cparams(*sem):
    return pltpu.CompilerParams(dimension_semantics=sem, vmem_limit_bytes=VMEM_LIMIT)


def _dot(a, b):
    return jnp.dot(a, b, preferred_element_type=F32)


def _dot_nt(a, b):
    return lax.dot_general(a, b, (((1,), (1,)), ((), ())), preferred_element_type=F32)


def _dot_tn(a, b):
    return lax.dot_general(a, b, (((0,), (0,)), ((), ())), preferred_element_type=F32)


def _split(x):
    hi = x.astype(BF16)
    lo = (x - hi.astype(F32)).astype(BF16)
    return hi, lo


def _sigmoid(x):
    return 1.0 / (1.0 + jnp.exp(-x))


def _rms(x):
    return x * lax.rsqrt(jnp.mean(x * x, axis=-1, keepdims=True) + RMS_EPS)


def _mod_kernel(cond_ref, w_ref, b_ref, o_ref):
    a = cond_ref[...]
    a = a * _sigmoid(a)
    o_ref[0] = _dot(a.astype(BF16), w_ref[0].astype(BF16)) + b_ref[0]


def _modulation(cond, w_mod, b_mod):
    tn = 1536
    nw = N_MOD * D_MODEL
    return pl.pallas_call(
        _mod_kernel,
        grid=(DEPTH, nw // tn),
        in_specs=[pl.BlockSpec((SUBLANES, D_MODEL), lambda l, j: (0, 0)),
                  pl.BlockSpec((1, D_MODEL, tn), lambda l, j: (l, 0, j)),
                  pl.BlockSpec((1, 1, tn), lambda l, j: (l, 0, j))],
        out_specs=pl.BlockSpec((1, SUBLANES, tn), lambda l, j: (l, 0, j)),
        out_shape=jax.ShapeDtypeStruct((DEPTH, SUBLANES, nw), F32),
        compiler_params=_cparams("arbitrary", "arbitrary"),
        name="modulation",
    )(cond, w_mod, b_mod.reshape(DEPTH, 1, nw))


def _inproj_kernel(*refs, mode):
    if mode == "plain":
        x_ref, mod_ref, n1_ref, w_ref = refs[:4]
        outs = refs[4:]
        x = x_ref[...]
    else:
        x_ref, pos_ref, mod_ref, n1_ref, w_ref, xo_ref = refs[:6]
        outs = refs[6:]
        x = x_ref[...] + pos_ref[...]
        xo_ref[...] = x
    qk_ref, v_ref, og_ref, uf_ref, ux_ref, ug_ref, lr_ref, wsc = outs

    @pl.when(pl.program_id(0) == 0)
    def _():
        for r in range(0, D_MODEL, 256):
            wsc[r:r + 256, 0:_C_OG_END] = w_ref[r:r + 256, 0:_C_OG_END].astype(BF16)
            wsc[r:r + 256, _C_OG_END:_C_OG_END + 768] = w_ref[r:r + 256, _C_LR_END:IN_W].astype(BF16)
            lrw = w_ref[r:r + 256, _C_OG_END:_C_LR_END].astype(BF16)
            wsc[r:r + 256, _C_OG_END + 768:_C_OG_END + 896] = jnp.concatenate(
                [lrw, jnp.zeros((256, LANES - GLA_LR_W), BF16)], axis=1)

    m = mod_ref[0]
    h = _rms(x) * n1_ref[...] * (1.0 + m[:, D_MODEL:2 * D_MODEL]) + m[:, 0:D_MODEL]
    hb = h.astype(BF16)
    qk_ref[...] = _dot(hb, wsc[:, 0:512])
    v_ref[...] = _dot(hb, wsc[:, 512:1024])
    og_ref[...] = _dot(hb, wsc[:, 1024:1536])
    uf_ref[...] = _dot(hb, wsc[:, 1536:1792])
    ux_ref[...] = _dot(hb, wsc[:, 1792:2048])
    ug_ref[...] = _dot(hb, wsc[:, 2048:2304])
    lr_ref[...] = _dot(hb, wsc[:, 2304:2432])


def _inproj(mode, x, extra, mod_l, norm1_l, w_in, l, B, L):
    T = B * L
    tm = TOKEN_TILE
    tpb = L // tm
    per_batch = mod_l.shape[0] > 1
    bidx = (lambda i: (i // tpb, 0, 0)) if per_batch else (lambda i: (0, 0, 0))
    row = lambda i: (i, 0)
    in_specs = [pl.BlockSpec((tm, D_MODEL), row)]
    args = [x]
    if mode == "pos":
        in_specs.append(pl.BlockSpec((tm, D_MODEL), lambda i: (i % tpb, 0)))
        args.append(extra)
    in_specs += [pl.BlockSpec((1, 1, N_MOD * D_MODEL), bidx),
                 pl.BlockSpec((1, D_MODEL), lambda i: (0, 0)),
                 pl.BlockSpec((None, D_MODEL, IN_W), lambda i: (l, 0, 0))]
    args += [mod_l, norm1_l.reshape(1, D_MODEL), w_in]
    widths = [512, 512, 512, 256, 256, 256, LANES]
    out_specs = [pl.BlockSpec((tm, w), row) for w in widths]
    out_shape = [jax.ShapeDtypeStruct((T, w), F32) for w in widths]
    if mode != "plain":
        out_specs = [pl.BlockSpec((tm, D_MODEL), row)] + out_specs
        out_shape = [jax.ShapeDtypeStruct((T, D_MODEL), F32)] + out_shape
    res = pl.pallas_call(
        functools.partial(_inproj_kernel, mode=mode),
        grid=(T // tm,),
        in_specs=in_specs,
        out_specs=out_specs,
        out_shape=out_shape,
        scratch_shapes=[pltpu.VMEM((D_MODEL, 2432), BF16)],
        compiler_params=_cparams("arbitrary"),
        name="inproj_" + mode,
    )(*args)
    if mode == "plain":
        return (x,) + tuple(res)
    return tuple(res)


def _gla_kernel(*refs, L, has_s0, want_state):
    qk_ref, v_ref, og_ref, lr_ref, wdec_ref, bdec_ref, gn_ref = refs[:7]
    p = 7
    s0_ref = None
    if has_s0:
        s0_ref = refs[p]
        p += 1
    o_ref = refs[p]
    p += 1
    sn_ref = None
    if want_state:
        sn_ref = refs[p]
        p += 1
    g_scr, oacc, s_scr = refs[p:p + 3]

    ch = GLA_CH
    n_chunks = L // ch
    kw = GLA_QK_W
    vw = GLA_V_W

    z16 = jnp.zeros((GLA_RANK, kw), F32)
    wc = jnp.concatenate([
        jnp.concatenate([wdec_ref[0], z16], axis=1),
        jnp.concatenate([z16, wdec_ref[1]], axis=1),
        jnp.zeros((LANES - GLA_LR_W, 2 * kw), F32)], axis=0).astype(BF16)
    bias = jnp.concatenate([bdec_ref[0], bdec_ref[1]], axis=1)
    z = _dot(lr_ref[0].astype(BF16), wc) + bias
    g_scr[...] = (jnp.minimum(z, 0.0) - jnp.log1p(jnp.exp(-jnp.abs(z)))) * (1.0 / GLA_GATE_NORM)

    row = lax.broadcasted_iota(I32, (ch, ch), 0)
    col = lax.broadcasted_iota(I32, (ch, ch), 1)
    lane_head = lax.broadcasted_iota(I32, (1, kw), 1) // GLA_DK
    blockdiag = (lax.broadcasted_iota(I32, (kw, vw), 0) // GLA_DK) == (lax.broadcasted_iota(I32, (kw, vw), 1) // GLA_DV)
    ones_t = jnp.ones((ch, LANES), BF16)
    gn = gn_ref[...]

    def finish(o, r0):
        parts = []
        for h in range(GLA_HEADS):
            parts.append(_rms(o[:, h * GLA_DV:(h + 1) * GLA_DV]) * gn)
        ogv = og_ref[0, r0:r0 + ch, :]
        return (jnp.concatenate(parts, axis=1) * (ogv * _sigmoid(ogv))).astype(BF16)

    for d in range(2):
        allowed = (col <= row) if d == 0 else (col >= row)
        tri = jnp.where(allowed, 1.0, 0.0).astype(BF16)
        if has_s0:
            s_scr[...] = jnp.zeros((kw, vw), F32)
            for h in range(GLA_HEADS):
                s_scr[h * GLA_DK:(h + 1) * GLA_DK, h * GLA_DV:(h + 1) * GLA_DV] = s0_ref[0, d, h]
        for i in range(n_chunks):
            n = i if d == 0 else n_chunks - 1 - i
            r0 = n * ch
            state_is_zero = (i == 0) and not has_s0
            gch = g_scr[r0:r0 + ch, d * kw:(d + 1) * kw]
            g_hi, g_lo = _split(gch)
            b = _dot(tri, g_hi) + _dot(tri, g_lo)
            b_last = b[ch - 1:ch, :] if d == 0 else b[0:1, :]
            bc = b - b[ch // 2:ch // 2 + 1, :]
            qch = qk_ref[0, r0:r0 + ch, 0:kw] * (GLA_DK ** -0.5)
            kch = qk_ref[0, r0:r0 + ch, kw:2 * kw]
            vb = v_ref[0, r0:r0 + ch, :].astype(BF16)
            q_s = (qch * jnp.exp(bc)).astype(BF16)
            k_s = (kch * jnp.exp(-bc)).astype(BF16)
            zero_q = jnp.zeros_like(q_s)
            qbig = jnp.concatenate([jnp.where(lane_head == h, q_s, zero_q) for h in range(GLA_HEADS)], axis=0)
            scores = _dot_nt(qbig, k_s)
            parts = []
            for h in range(GLA_HEADS):
                ph = jnp.where(allowed, scores[h * ch:(h + 1) * ch, :], 0.0).astype(BF16)
                parts.append(_dot(ph, vb[:, h * GLA_DV:(h + 1) * GLA_DV]))
            o = jnp.concatenate(parts, axis=1)
            if not state_is_zero:
                q_t = (qch * jnp.exp(b)).astype(BF16)
                o = o + _dot(q_t, s_scr[...].astype(BF16))
            if (i < n_chunks - 1) or want_state:
                k_d = (kch * jnp.exp(b_last - b)).astype(BF16)
                ds = jnp.where(blockdiag, _dot_tn(k_d, vb), 0.0)
                if state_is_zero:
                    s_scr[...] = ds
                else:
                    dcol = _dot_tn(g_hi, ones_t) + _dot_tn(g_lo, ones_t)
                    dec = jnp.exp(dcol)
                    s_scr[...] = s_scr[...] * jnp.concatenate([dec] * (vw // LANES), axis=1) + ds
            if d == 0:
                oacc[r0:r0 + ch, :] = o
            else:
                o_ref[0, r0:r0 + ch, :] = finish(o + oacc[r0:r0 + ch, :], r0)
        if want_state:
            for h in range(GLA_HEADS):
                sn_ref[0, d, h] = s_scr[h * GLA_DK:(h + 1) * GLA_DK, h * GLA_DV:(h + 1) * GLA_DV]


def _gla(qk, v, og, lr, wdec_l, bdec_l, gn_l, s0, B, L, want_state):
    has_s0 = s0 is not None
    blk = lambda w: pl.BlockSpec((1, L, w), lambda b: (b, 0, 0))
    in_specs = [blk(512), blk(512), blk(512), blk(LANES),
                pl.BlockSpec((2, GLA_RANK, GLA_QK_W), lambda b: (0, 0, 0)),
                pl.BlockSpec((2, 1, GLA_QK_W), lambda b: (0, 0, 0)),
                pl.BlockSpec((1, GLA_DV), lambda b: (0, 0))]
    args = [qk.reshape(B, L, 512), v.reshape(B, L, 512), og.reshape(B, L, 512), lr.reshape(B, L, LANES),
            wdec_l, bdec_l.reshape(2, 1, GLA_QK_W), gn_l.reshape(1, GLA_DV)]
    st_spec = pl.BlockSpec((1, 2, GLA_HEADS, GLA_DK, GLA_DV), lambda b: (b, 0, 0, 0, 0))
    if has_s0:
        in_specs.append(st_spec)
        args.append(s0)
    out_specs = [pl.BlockSpec((1, L, GLA_V_W), lambda b: (b, 0, 0))]
    out_shape = [jax.ShapeDtypeStruct((B, L, GLA_V_W), BF16)]
    if want_state:
        out_specs.append(st_spec)
        out_shape.append(jax.ShapeDtypeStruct((B, 2, GLA_HEADS, GLA_DK, GLA_DV), F32))
    res = pl.pallas_call(
        functools.partial(_gla_kernel, L=L, has_s0=has_s0, want_state=want_state),
        grid=(B,),
        in_specs=in_specs,
        out_specs=out_specs,
        out_shape=out_shape,
        scratch_shapes=[pltpu.VMEM((L, 2 * GLA_QK_W), F32),
                        pltpu.VMEM((L, GLA_V_W), F32),
                        pltpu.VMEM((GLA_QK_W, GLA_V_W), F32)],
        compiler_params=_cparams("arbitrary"),
        name="gla",
    )(*args)
    o = res[0].reshape(B * L, GLA_V_W)
    return o, (res[1] if want_state else None)


def _fft_tables(L):
    m = np.arange(L, dtype=np.int64)
    ang = 2.0 * np.pi * ((m[:, None] * m[None, :]) % L) / L
    cc = np.concatenate([np.cos(ang), -np.sin(ang)], axis=1)
    c = np.arange(FOURIER_GW, dtype=np.int64)
    angc = 2.0 * np.pi * ((c[:, None] * c[None, :]) % FOURIER_GW) / FOURIER_GW
    scale = 1.0 / math.sqrt(L * FOURIER_GW)
    eye = np.eye(FOURIER_GROUPS)
    bdc = np.kron(eye, np.cos(angc) * scale)
    bds = np.kron(eye, np.sin(angc) * scale)
    return (jnp.asarray(cc, dtype=F32), jnp.asarray(bdc, dtype=F32), jnp.asarray(bds, dtype=F32))


def _fft_kernel(u_ref, cc_ref, bdc_ref, bds_ref, o_ref):
    u_hi, u_lo = _split(u_ref[0])
    bdc = bdc_ref[...].astype(BF16)
    bds = bds_ref[...].astype(BF16)
    uc = _dot(u_hi, bdc) + _dot(u_lo, bdc)
    us = _dot(u_hi, bds) + _dot(u_lo, bds)
    w_hi, w_lo = _split(jnp.concatenate([uc, us], axis=0))
    cc = cc_ref[...].astype(BF16)
    o_ref[0] = (_dot(cc, w_hi) + _dot(cc, w_lo)).astype(BF16)


def _fft(uf, B, L):
    cc, bdc, bds = _fft_tables(L)
    res = pl.pallas_call(
        _fft_kernel,
        grid=(B,),
        in_specs=[pl.BlockSpec((1, L, FOURIER_W), lambda b: (b, 0, 0)),
                  pl.BlockSpec((L, 2 * L), lambda b: (0, 0)),
                  pl.BlockSpec((FOURIER_W, FOURIER_W), lambda b: (0, 0)),
                  pl.BlockSpec((FOURIER_W, FOURIER_W), lambda b: (0, 0))],
        out_specs=pl.BlockSpec((1, L, FOURIER_W), lambda b: (b, 0, 0)),
        out_shape=jax.ShapeDtypeStruct((B, L, FOURIER_W), BF16),
        compiler_params=_cparams("arbitrary"),
        name="fourier",
    )(uf.reshape(B, L, FOURIER_W), cc, bdc, bds)
    return res.reshape(B * L, FOURIER_W)


def _lru_kernel(*refs, L, has_s0, want_state):
    ux_ref, ug_ref, cw_ref, cb_ref, wa_ref, ba_ref, wx_ref, bx_ref, lam_ref = refs[:9]
    p = 9
    s0_ref = None
    if has_s0:
        s0_ref = refs[p]
        p += 1
    o_ref = refs[p]
    p += 1
    sn_ref = None
    if want_state:
        sn_ref = refs[p]
        p += 1
    bd_scr = refs[p]
    scan_scr = refs[p + 1:p + 9]

    @pl.when(pl.program_id(0) == 0)
    def _():
        r = lax.broadcasted_iota(I32, (LRU_BW, LRU_W), 0)
        c = lax.broadcasted_iota(I32, (LRU_BW, LRU_W), 1)
        for d in range(2):
            for gi, w_ref in enumerate((wa_ref, wx_ref)):
                pieces = []
                for h in range(LRU_BLOCKS):
                    place = jnp.where(c == r + h * LRU_BW, 1.0, 0.0).astype(BF16)
                    pieces.append(_dot(w_ref[d, h].astype(BF16), place))
                bd_scr[2 * d + gi] = jnp.concatenate(pieces, axis=0).astype(BF16)

    t = lax.broadcasted_iota(I32, (L, 1), 0)
    x = ux_ref[0]
    xm2 = jnp.where(t >= 2, pltpu.roll(x, 2, 0), 0.0)
    xm1 = jnp.where(t >= 1, pltpu.roll(x, 1, 0), 0.0)
    xp1 = jnp.where(t <= L - 2, pltpu.roll(x, L - 1, 0), 0.0)
    xc = xm2 * cw_ref[0:1, :] + xm1 * cw_ref[1:2, :] + x * cw_ref[2:3, :] + xp1 * cw_ref[3:4, :] + cb_ref[...]
    xcb = xc.astype(BF16)

    nb = L // SUBLANES
    pitch = nb + SUBLANES
    n_slab = LRU_W // LANES
    sub = lax.broadcasted_iota(I32, (SUBLANES, LANES), 0)
    hsum = [[None] * n_slab for _ in range(SUBLANES)]
    for d in range(2):
        r = 0.5 + 0.5 * jnp.tanh(0.5 * (_dot(xcb, bd_scr[2 * d]) + ba_ref[d]))
        ig = 0.5 + 0.5 * jnp.tanh(0.5 * (_dot(xcb, bd_scr[2 * d + 1]) + bx_ref[d]))
        lam = lam_ref[d]
        softplus = jnp.maximum(-lam, 0.0) + jnp.log1p(jnp.exp(-jnp.abs(lam)))
        a = jnp.exp(-LRU_C * r * softplus)
        u = jnp.sqrt(1.0 - a * a) * (ig * xc)
        a_scr, u_scr, h_scr, p_scr = scan_scr[4 * d:4 * d + 4]
        for s in range(SUBLANES):
            for k in range(n_slab):
                a_scr[k, s * pitch:s * pitch + nb, :] = a[s * nb:(s + 1) * nb, k * LANES:(k + 1) * LANES]
                u_scr[k, s * pitch:s * pitch + nb, :] = u[s * nb:(s + 1) * nb, k * LANES:(k + 1) * LANES]
        steps = range(nb) if d == 0 else range(nb - 1, -1, -1)
        for k in range(n_slab):
            h = jnp.zeros((SUBLANES, LANES), F32)
            prod = jnp.ones((SUBLANES, LANES), F32)
            for i in steps:
                rows = pl.ds(i, SUBLANES, stride=pitch)
                ai = a_scr[k, rows, :]
                h = ai * h + u_scr[k, rows, :]
                prod = ai * prod
                h_scr[k, rows, :] = h
                p_scr[k, rows, :] = prod
            if has_s0:
                h0 = jnp.broadcast_to(s0_ref[0, d:d + 1, k * LANES:(k + 1) * LANES], (SUBLANES, LANES))
            else:
                h0 = jnp.zeros((SUBLANES, LANES), F32)
            first = 0 if d == 0 else SUBLANES - 1
            carry = jnp.where(sub == first, h0, 0.0)
            for j in range(1, SUBLANES):
                s = j if d == 0 else SUBLANES - 1 - j
                moved = pltpu.roll(prod * carry + h, 1 if d == 0 else SUBLANES - 1, 0)
                carry = jnp.where(sub == s, moved, carry)
            if want_state:
                last = SUBLANES - 1 - first
                sn_ref[0, d:d + 1, k * LANES:(k + 1) * LANES] = (prod * carry + h)[last:last + 1, :]
            for s in range(SUBLANES):
                blk = h_scr[k, s * pitch:s * pitch + nb, :] + p_scr[k, s * pitch:s * pitch + nb, :] * carry[s:s + 1, :]
                hsum[s][k] = blk if hsum[s][k] is None else hsum[s][k] + blk

    for s in range(SUBLANES):
        ugv = ug_ref[0, s * nb:(s + 1) * nb, :]
        gelu = 0.5 * ugv * (1.0 + jnp.tanh(math.sqrt(2.0 / math.pi) * (ugv + 0.044715 * (ugv * ugv * ugv))))
        o_ref[0, s * nb:(s + 1) * nb, :] = (jnp.concatenate(hsum[s], axis=1) * gelu).astype(BF16)


def _lru(ux, ug, cw_l, cb_l, wa_l, ba_l, wx_l, bx_l, lam_l, s0, B, L, want_state):
    has_s0 = s0 is not None
    blk = pl.BlockSpec((1, L, LRU_W), lambda b: (b, 0, 0))
    vec2 = pl.BlockSpec((2, 1, LRU_W), lambda b: (0, 0, 0))
    wsp = pl.BlockSpec((2, LRU_BLOCKS, LRU_BW, LRU_BW), lambda b: (0, 0, 0, 0))
    in_specs = [blk, blk,
                pl.BlockSpec((CONV_W, LRU_W), lambda b: (0, 0)),
                pl.BlockSpec((1, LRU_W), lambda b: (0, 0)),
                wsp, vec2, wsp, vec2, vec2]
    args = [ux.reshape(B, L, LRU_W), ug.reshape(B, L, LRU_W), cw_l, cb_l.reshape(1, LRU_W),
            wa_l, ba_l.reshape(2, 1, LRU_W), wx_l, bx_l.reshape(2, 1, LRU_W), lam_l.reshape(2, 1, LRU_W)]
    st_spec = pl.BlockSpec((1, 2, LRU_W), lambda b: (b, 0, 0))
    if has_s0:
        in_specs.append(st_spec)
        args.append(s0)
    out_specs = [blk]
    out_shape = [jax.ShapeDtypeStruct((B, L, LRU_W), BF16)]
    if want_state:
        out_specs.append(st_spec)
        out_shape.append(jax.ShapeDtypeStruct((B, 2, LRU_W), F32))
    res = pl.pallas_call(
        functools.partial(_lru_kernel, L=L, has_s0=has_s0, want_state=want_state),
        grid=(B,),
        in_specs=in_specs,
        out_specs=out_specs,
        out_shape=out_shape,
        scratch_shapes=[pltpu.VMEM((4, LRU_W, LRU_W), BF16)]
        + [pltpu.VMEM((LRU_W // LANES, L + SUBLANES * SUBLANES, LANES), F32)] * 8,
        compiler_params=_cparams("arbitrary"),
        name="rglru",
    )(*args)
    return res[0].reshape(B * L, LRU_W), (res[1] if want_state else None)


def _outproj_kernel(og_ref, of_ref, ol_ref, x_ref, mod_ref, n2_ref, wout_ref, wr_ref,
                    x1_ref, hx_ref, afft_ref, wsc, wrs):
    @pl.when(pl.program_id(0) == 0)
    def _():
        for r in range(0, MIX_W, 256):
            wsc[r:r + 256, :] = wout_ref[r:r + 256, :].astype(BF16)
        wrs[...] = jnp.concatenate([wr_ref[...], jnp.zeros((D_MODEL, LANES - N_EXPERTS), F32)], axis=1)

    m = mod_ref[0]
    y = (_dot(og_ref[...], wsc[0:GLA_V_W, :])
         + _dot(of_ref[...], wsc[GLA_V_W:GLA_V_W + FOURIER_W, :])
         + _dot(ol_ref[...], wsc[GLA_V_W + FOURIER_W:MIX_W, :]))
    x1 = x_ref[...] + m[:, 2 * D_MODEL:3 * D_MODEL] * y
    x1_ref[...] = x1
    h2 = _rms(x1) * n2_ref[...] * (1.0 + m[:, 4 * D_MODEL:5 * D_MODEL]) + m[:, 3 * D_MODEL:4 * D_MODEL]
    h_hi = h2.astype(BF16)
    h_hi32 = h_hi.astype(F32)
    tm = h2.shape[0]
    for s in range(ROW_SUB):
        hx_ref[pl.ds(s, tm, stride=ROW_SUB), :] = h_hi32[:, s * LANES:(s + 1) * LANES]
    h_lo = (h2 - h_hi32).astype(BF16)
    w_hi, w_lo = _split(wrs[...])
    logits = _dot(h_hi, w_hi) + _dot(h_lo, w_hi) + _dot(h_hi, w_lo)
    lane = lax.broadcasted_iota(I32, logits.shape, 1)
    logits = jnp.where(lane < N_EXPERTS, logits, -jnp.inf)
    ex = jnp.exp(logits - jnp.max(logits, axis=-1, keepdims=True))
    aff = ex / jnp.sum(ex, axis=-1, keepdims=True)
    afft_ref[...] = aff.T[0:N_EXPERTS, :]


def _outproj(o_gla, o_fft, o_lru, x, mod_l, norm2_l, w_out, w_router_l, l, B, L):
    T = B * L
    tm = OUT_TILE
    tpb = L // tm
    per_batch = mod_l.shape[0] > 1
    bidx = (lambda i: (i // tpb, 0, 0)) if per_batch else (lambda i: (0, 0, 0))
    row = lambda i: (i, 0)
    return pl.pallas_call(
        _outproj_kernel,
        grid=(T // tm,),
        in_specs=[pl.BlockSpec((tm, GLA_V_W), row), pl.BlockSpec((tm, FOURIER_W), row),
                  pl.BlockSpec((tm, LRU_W), row), pl.BlockSpec((tm, D_MODEL), row),
                  pl.BlockSpec((1, 1, N_MOD * D_MODEL), bidx),
                  pl.BlockSpec((1, D_MODEL), lambda i: (0, 0)),
                  pl.BlockSpec((None, MIX_W, D_MODEL), lambda i: (l, 0, 0)),
                  pl.BlockSpec((D_MODEL, N_EXPERTS), lambda i: (0, 0))],
        out_specs=[pl.BlockSpec((tm, D_MODEL), row),
                   pl.BlockSpec((tm * ROW_SUB, LANES), lambda i: (i, 0)),
                   pl.BlockSpec((N_EXPERTS, tm), lambda i: (0, i))],
        out_shape=[jax.ShapeDtypeStruct((T, D_MODEL), F32),
                   jax.ShapeDtypeStruct((T * ROW_SUB, LANES), F32),
                   jax.ShapeDtypeStruct((N_EXPERTS, T), F32)],
        scratch_shapes=[pltpu.VMEM((MIX_W, D_MODEL), BF16), pltpu.VMEM((D_MODEL, LANES), F32)],
        compiler_params=_cparams("arbitrary"),
        name="outproj",
    )(o_gla, o_fft, o_lru, x, mod_l, norm2_l.reshape(1, D_MODEL), w_out, w_router_l)


def _prefix_lanes(x):
    T = x.shape[1]
    w = 256
    nb = T // w
    stacked = jnp.concatenate([x[:, j * w:(j + 1) * w] for j in range(nb)], axis=0)
    upper = jnp.where(lax.broadcasted_iota(I32, (w, w), 0) <= lax.broadcasted_iota(I32, (w, w), 1), 1.0, 0.0)
    pe = _dot(stacked, upper.astype(BF16))
    carry = jnp.zeros((N_EXPERTS, 1), F32)
    outs = []
    for j in range(nb):
        blk = pe[j * N_EXPERTS:(j + 1) * N_EXPERTS, :]
        outs.append(blk + carry)
        carry = carry + blk[:, w - 1:w]
    return jnp.concatenate(outs, axis=1)


def _topk_kernel(aff_ref, out_ref, gate_ref, slot_ref, bnd_ref, *, T, C):
    n_a = C // DIGIT
    aff = aff_ref[...]
    bits = jnp.zeros((N_EXPERTS, 1), I32)
    for bit in range(30, -1, -1):
        cand = bits | (1 << bit)
        cnt = jnp.sum(jnp.where(aff >= pltpu.bitcast(cand, F32), 1.0, 0.0), axis=1, keepdims=True)
        bits = jnp.where(cnt >= C, cand, bits)
    thr = pltpu.bitcast(bits, F32)
    gt = aff > thr
    eq = aff == thr
    eqf = jnp.where(eq, 1.0, 0.0)
    need = C - jnp.sum(jnp.where(gt, 1.0, 0.0), axis=1, keepdims=True)
    eq_before = _prefix_lanes(eqf.astype(BF16)) - eqf
    sel = gt | (eq & (eq_before < need))
    self32 = jnp.where(sel, 1.0, 0.0)
    cnt = _prefix_lanes(self32.astype(BF16))
    slot_ref[...] = jnp.where(sel, cnt - 1.0, -1.0)
    tok = lax.broadcasted_iota(I32, (1, T), 1)
    lane = lax.broadcasted_iota(I32, (1, LANES), 1)
    bnd = jnp.zeros((N_EXPERTS, LANES), F32)
    for j in range(1, T // COMBINE_TILE + 1):
        before = jnp.sum(jnp.where(tok < j * COMBINE_TILE, self32, 0.0), axis=1, keepdims=True)
        bnd = jnp.where(lane == j, before, bnd)
    bnd_ref[...] = bnd

    p_dig = jnp.floor(cnt * (1.0 / DIGIT))
    q_dig = cnt - DIGIT * p_dig
    a_col = lax.broadcasted_iota(I32, (n_a, 1), 0).astype(F32)
    b_col = lax.broadcasted_iota(I32, (DIGIT, 1), 0).astype(F32)
    slot = cnt - 1.0
    ps_dig = jnp.where(sel, jnp.floor(slot * (1.0 / DIGIT)), -1.0)
    qs_dig = slot - DIGIT * jnp.floor(slot * (1.0 / DIGIT))
    aff_hi = aff.astype(BF16).astype(F32)
    aff_lo = aff - aff_hi
    kc = min(T, 2048)
    acc = jnp.zeros((N_EXPERTS * n_a, N_EXPERTS * DIGIT), F32)
    gacc = jnp.zeros((N_EXPERTS * n_a, N_EXPERTS * DIGIT), F32)
    for c0 in range(0, T, kc):
        tk = slice(c0, c0 + kc)
        u = jnp.concatenate([jnp.where(p_dig[e:e + 1, tk] == a_col, 1.0, 0.0).astype(BF16)
                             for e in range(N_EXPERTS)], axis=0)
        v = jnp.concatenate([jnp.where(q_dig[e:e + 1, tk] <= b_col, 1.0, 0.0).astype(BF16)
                             for e in range(N_EXPERTS)], axis=0)
        acc = acc + _dot_nt(u, v)
        us = jnp.concatenate([jnp.where(ps_dig[e:e + 1, tk] == a_col, 1.0, 0.0).astype(BF16)
                              for e in range(N_EXPERTS)], axis=0)
        for part in (aff_hi, aff_lo):
            vs = jnp.concatenate([jnp.where(qs_dig[e:e + 1, tk] == b_col, part[e:e + 1, tk], 0.0).astype(BF16)
                                  for e in range(N_EXPERTS)], axis=0)
            gacc = gacc + _dot_nt(us, vs)
    below = jnp.concatenate([jnp.sum(jnp.where(p_dig[e:e + 1, :] < a_col, 1.0, 0.0), axis=1, keepdims=True)
                             for e in range(N_EXPERTS)], axis=0)
    r_i = lax.broadcasted_iota(I32, acc.shape, 0) // n_a
    c_i = lax.broadcasted_iota(I32, acc.shape, 1) // DIGIT

    def own_block(x):
        x = jnp.where(r_i == c_i, x, 0.0)
        x = x[:, 0:256] + x[:, 256:512]
        x = x[:, 0:LANES] + x[:, LANES:2 * LANES]
        x = x + pltpu.roll(x, 64, 1)
        return x + pltpu.roll(x, 32, 1)

    out_ref[...] = own_block(acc) + below
    gate_ref[...] = own_block(gacc)


def _expert_choice(aff_t, T, C):
    n_a = C // DIGIT
    res = pl.pallas_call(
        functools.partial(_topk_kernel, T=T, C=C),
        grid=(1,),
        in_specs=[pl.BlockSpec((N_EXPERTS, T), lambda i: (0, 0))],
        out_specs=[pl.BlockSpec((N_EXPERTS * n_a, LANES), lambda i: (0, 0)),
                   pl.BlockSpec((N_EXPERTS * n_a, LANES), lambda i: (0, 0)),
                   pl.BlockSpec((N_EXPERTS, T), lambda i: (0, 0)),
                   pl.BlockSpec((N_EXPERTS, LANES), lambda i: (0, 0))],
        out_shape=[jax.ShapeDtypeStruct((N_EXPERTS * n_a, LANES), F32),
                   jax.ShapeDtypeStruct((N_EXPERTS * n_a, LANES), F32),
                   jax.ShapeDtypeStruct((N_EXPERTS, T), F32),
                   jax.ShapeDtypeStruct((N_EXPERTS, LANES), F32)],
        compiler_params=_cparams("arbitrary"),
        name="expert_choice",
    )(aff_t)
    idx = res[0][:, 0:DIGIT].astype(I32).reshape(N_EXPERTS, C)
    gate = res[1][:, 0:DIGIT].reshape(N_EXPERTS, 1, C)
    bnd = res[3][:, 0:T // COMBINE_TILE + 1].astype(I32)
    return idx, gate, res[2], bnd


def _ffn_kernel(idx_ref, hx_hbm, gate_ref, wg_ref, wu_ref, wd_ref, y_ref, xbuf, wgb, wub, wdb, sems, *, C):
    e = pl.program_id(0)
    last = pl.num_programs(0) - 1
    buf = e % 2
    nxt = 1 - buf
    following = jnp.where(e == last, 0, e + 1)

    def fetch_row(expert, c, into, k):
        src = hx_hbm.at[pl.ds(pl.multiple_of(idx_ref[expert, c] * ROW_SUB, ROW_SUB), ROW_SUB)]
        dst = xbuf.at[into, pl.ds(pl.multiple_of(c * ROW_SUB, ROW_SUB), ROW_SUB)]
        pltpu.make_async_copy(src, dst, sems.at[into]).start(priority=k % 2)

    @pl.when(e == 0)
    def _():
        def body(i, carry):
            for k in range(GATHER_UNROLL):
                fetch_row(0, i * GATHER_UNROLL + k, 0, k)
            return carry
        lax.fori_loop(0, C // GATHER_UNROLL, body, 0)

    for r in range(0, D_MODEL, 256):
        wgb[r:r + 256, :] = wg_ref[r:r + 256, :].astype(BF16)
        wub[r:r + 256, :] = wu_ref[r:r + 256, :].astype(BF16)
        wdb[r:r + 256, :] = wd_ref[r:r + 256, :].astype(BF16)
    pltpu.make_async_copy(hx_hbm.at[pl.ds(0, C * ROW_SUB)], xbuf.at[buf], sems.at[buf]).wait()

    step = min(MOE_ROWS, C)
    for r0 in range(0, C, step):
        for k in range(step):
            fetch_row(following, r0 + k, nxt, k)
        x = jnp.concatenate([xbuf[buf, pl.ds(r0 * ROW_SUB + s, step, stride=ROW_SUB), :] for s in range(ROW_SUB)],
                            axis=1).astype(BF16)
        g = _dot(x, wgb[...])
        u = _dot(x, wub[...])
        hid = (g * _sigmoid(g) * u).astype(BF16)
        gate = jnp.broadcast_to(gate_ref[:, r0:r0 + step], (LANES, step)).T[:, 0:1]
        y_ref[pl.ds(r0, step), :] = (_dot(hid, wdb[...]) * gate).astype(BF16)

    @pl.when(e == last)
    def _():
        pltpu.make_async_copy(hx_hbm.at[pl.ds(0, C * ROW_SUB)], xbuf.at[nxt], sems.at[nxt]).wait()


def _expert_ffn(idx, hx, gate, wg, wu, wd, l, C):
    wspec = pl.BlockSpec((None, None, D_MODEL, EXPERT_FF), lambda e, idx_ref: (l, e, 0, 0))
    return pl.pallas_call(
        functools.partial(_ffn_kernel, C=C),
        grid_spec=pltpu.PrefetchScalarGridSpec(
            num_scalar_prefetch=1,
            grid=(N_EXPERTS,),
            in_specs=[pl.BlockSpec(memory_space=pl.ANY),
                      pl.BlockSpec((None, 1, C), lambda e, idx_ref: (e, 0, 0)), wspec, wspec,
                      pl.BlockSpec((None, None, EXPERT_FF, D_MODEL), lambda e, idx_ref: (l, e, 0, 0))],
            out_specs=pl.BlockSpec((None, C, D_MODEL), lambda e, idx_ref: (e, 0, 0)),
            scratch_shapes=[pltpu.VMEM((2, C * ROW_SUB, LANES), F32),
                            pltpu.VMEM((D_MODEL, EXPERT_FF), BF16),
                            pltpu.VMEM((D_MODEL, EXPERT_FF), BF16),
                            pltpu.VMEM((EXPERT_FF, D_MODEL), BF16),
                            pltpu.SemaphoreType.DMA((2,))]),
        out_shape=jax.ShapeDtypeStruct((N_EXPERTS, C, D_MODEL), BF16),
        compiler_params=pltpu.CompilerParams(dimension_semantics=("arbitrary",), vmem_limit_bytes=VMEM_LIMIT,
                                             disable_bounds_checks=True),
        name="expert_ffn",
    )(idx, hx, gate, wg, wu, wd)


def _combine_kernel(bnd_ref, slot_ref, y_hbm, x1_ref, gmod_ref, fn_ref, o_ref, ybuf, onehot, acc, sem, *, C, final):
    j = pl.program_id(0)
    ch = COMBINE_CHUNK
    group = COMBINE_DEPTH // ch

    @pl.when(j == 0)
    def _():
        ybuf[...] = jnp.zeros(ybuf.shape, BF16)

    w_col = lax.broadcasted_iota(I32, (ch, 1), 0)
    total = jnp.int32(0)
    for e in range(N_EXPERTS):
        first = bnd_ref[e, j] & (-BF16_ROWS)
        n_chunks = lax.div(bnd_ref[e, j + 1] - first + (ch - 1), jnp.int32(ch))

        def fetch_chunk(c, carry, e=e, first=first, base=total):
            want_lo = first + c * ch
            src_row = pl.multiple_of(jnp.minimum(want_lo, C - ch), BF16_ROWS)
            dst_row = pl.multiple_of((base + c) * ch, ch)
            pltpu.make_async_copy(y_hbm.at[e, pl.ds(src_row, ch)], ybuf.at[pl.ds(dst_row, ch)], sem).start()
            row = src_row + w_col
            want = jnp.where(row >= want_lo, row, -2).astype(F32)
            onehot[pl.ds(dst_row, ch), :] = jnp.where(slot_ref[e:e + 1, :] == want, 1.0, 0.0).astype(BF16)
            return carry

        lax.fori_loop(0, n_chunks, fetch_chunk, 0)
        total = total + n_chunks

    n_groups = lax.div(total + (group - 1), jnp.int32(group))

    def clear_chunk(k, carry):
        onehot[pl.ds(pl.multiple_of(k * ch, ch), ch), :] = jnp.zeros((ch, onehot.shape[1]), BF16)
        return carry

    lax.fori_loop(total, n_groups * group, clear_chunk, 0)

    def wait_chunk(k, carry):
        pltpu.make_async_copy(y_hbm.at[0, pl.ds(0, ch)], ybuf.at[pl.ds(0, ch)], sem).wait()
        return carry

    lax.fori_loop(0, total, wait_chunk, 0)
    acc[...] = jnp.zeros(acc.shape, F32)

    def add_group(g, carry):
        rows = pl.ds(pl.multiple_of(g * COMBINE_DEPTH, COMBINE_DEPTH), COMBINE_DEPTH)
        acc[...] += _dot_tn(onehot[rows, :], ybuf[rows, :])
        return carry

    lax.fori_loop(0, n_groups, add_group, 0)
    x = x1_ref[...] + gmod_ref[0][:, 5 * D_MODEL:6 * D_MODEL] * acc[...]
    o_ref[...] = _rms(x) * fn_ref[...] if final else x


def _combine(bnd, slot, y, x1, gmod, final_norm, final, B, L, C):
    T = B * L
    tt = COMBINE_TILE
    tpb = L // tt if L >= tt else None
    per_batch = gmod.shape[0] > 1
    if per_batch:
        bidx = lambda j, b: (j // tpb, 0, 0)
    else:
        bidx = lambda j, b: (0, 0, 0)
    max_chunks = N_EXPERTS * (tt // COMBINE_CHUNK + 2)
    max_rows = pl.cdiv(max_chunks * COMBINE_CHUNK, COMBINE_DEPTH) * COMBINE_DEPTH
    return pl.pallas_call(
        functools.partial(_combine_kernel, C=C, final=final),
        grid_spec=pltpu.PrefetchScalarGridSpec(
            num_scalar_prefetch=1,
            grid=(T // tt,),
            in_specs=[pl.BlockSpec((N_EXPERTS, tt), lambda j, b: (0, j)),
                      pl.BlockSpec(memory_space=pl.ANY),
                      pl.BlockSpec((tt, D_MODEL), lambda j, b: (j, 0)),
                      pl.BlockSpec((1, 1, N_MOD * D_MODEL), bidx),
                      pl.BlockSpec((1, D_MODEL), lambda j, b: (0, 0))],
            out_specs=pl.BlockSpec((tt, D_MODEL), lambda j, b: (j, 0)),
            scratch_shapes=[pltpu.VMEM((max_rows, D_MODEL), BF16),
                            pltpu.VMEM((max_rows, tt), BF16),
                            pltpu.VMEM((tt, D_MODEL), F32),
                            pltpu.SemaphoreType.DMA(())]),
        out_shape=jax.ShapeDtypeStruct((T, D_MODEL), F32),
        compiler_params=_cparams("arbitrary"),
        name="combine_final" if final else "combine",
    )(bnd, slot, y, x1, gmod, final_norm.reshape(1, D_MODEL))


def _grid_position_embedding(n_tokens):
    rows = n_tokens // GRID_W
    r, col = jnp.meshgrid(jnp.arange(rows, dtype=F32), jnp.arange(GRID_W, dtype=F32), indexing="ij")
    n_freq = D_MODEL // 4
    omega = 1.0 / (POS_BASE ** (jnp.arange(n_freq, dtype=F32) / n_freq))
    ar = r.reshape(-1)[:, None] * omega
    ac = col.reshape(-1)[:, None] * omega
    return jnp.concatenate([jnp.sin(ar), jnp.cos(ar), jnp.sin(ac), jnp.cos(ac)], axis=-1)


def _trunk(x_in, pos, mod_g, gla_s0, lru_s0, want_state, w):
    B, L, _ = x_in.shape
    T = B * L
    C = CAPACITY_FACTOR * T // N_EXPERTS
    x = x_in.reshape(T, D_MODEL)
    gla_states = []
    lru_states = []
    for l in range(DEPTH):
        mod_l = mod_g[l]
        mode = "pos" if (l == 0 and pos is not None) else "plain"
        x, qk, v, og, uf, ux, ug, lr = _inproj(mode, x, pos, mod_l, w["norm1"][l], w["w_in"], l, B, L)
        o_gla, gs = _gla(qk, v, og, lr, w["gla_w_decay"][l], w["gla_b_decay"][l], w["gla_norm"][l],
                         None if gla_s0 is None else gla_s0[:, l], B, L, want_state)
        o_fft = _fft(uf, B, L)
        o_lru, ls = _lru(ux, ug, w["lru_conv_w"][l], w["lru_conv_b"][l], w["lru_wa"][l], w["lru_ba"][l],
                         w["lru_wx"][l], w["lru_bx"][l], w["lru_lambda"][l],
                         None if lru_s0 is None else lru_s0[:, l], B, L, want_state)
        x1, hx, aff_t = _outproj(o_gla, o_fft, o_lru, x, mod_l, w["norm2"][l], w["w_out"], w["w_router"][l],
                                 l, B, L)
        idx, gate, slot, bnd = _expert_choice(aff_t, T, C)
        y = _expert_ffn(idx, hx, gate, w["w_expert_gate"], w["w_expert_up"], w["w_expert_down"], l, C)
        x = _combine(bnd, slot, y, x1, mod_l, w["final_norm"], l == DEPTH - 1, B, L, C)
        gla_states.append(gs)
        lru_states.append(ls)
    return x.reshape(B, L, D_MODEL), gla_states, lru_states


def kernel(x_prompt, x_sample, state_gla, state_rglru, c, c_ctx, w_mod, b_mod, norm1, norm2, w_in, gla_w_decay, gla_b_decay, gla_norm, lru_conv_w, lru_conv_b, lru_wa, lru_ba, lru_wx, lru_bx, lru_lambda, w_out, w_router, w_expert_gate, w_expert_up, w_expert_down, final_norm):
    w = dict(norm1=norm1, norm2=norm2, w_in=w_in, gla_w_decay=gla_w_decay, gla_b_decay=gla_b_decay,
             gla_norm=gla_norm, lru_conv_w=lru_conv_w, lru_conv_b=lru_conv_b, lru_wa=lru_wa, lru_ba=lru_ba,
             lru_wx=lru_wx, lru_bx=lru_bx, lru_lambda=lru_lambda, w_out=w_out, w_router=w_router,
             w_expert_gate=w_expert_gate, w_expert_up=w_expert_up, w_expert_down=w_expert_down,
             final_norm=final_norm)
    n_lat = c.shape[0]
    cond = jnp.concatenate([c_ctx[None, :], c, jnp.zeros((SUBLANES - 1 - n_lat, D_MODEL), F32)], axis=0)
    mod = _modulation(cond, w_mod, b_mod)
    mod_ctx = mod[:, 0:1].reshape(DEPTH, 1, 1, N_MOD * D_MODEL)
    mod_lat = mod[:, 1:1 + n_lat].reshape(DEPTH, n_lat, 1, N_MOD * D_MODEL)

    y_prompt, gla_states, lru_states = _trunk(x_prompt, None, mod_ctx, None, None, True, w)
    pos = _grid_position_embedding(x_sample.shape[1])
    y_sample, _, _ = _trunk(x_sample, pos, mod_lat, state_gla, state_rglru, False, w)
    new_state_gla = jnp.stack(gla_states, axis=1)
    new_state_rglru = jnp.stack(lru_states, axis=1)
    return (y_prompt, y_sample, new_state_gla, new_state_rglru)
```

```python
import functools
import math

import numpy as np
import jax
import jax.numpy as jnp
from jax import lax
from jax.experimental import pallas as pl
from jax.experimental.pallas import tpu as pltpu

F32 = jnp.float32
BF16 = jnp.bfloat16
I32 = jnp.int32

D_MODEL = 1024
DEPTH = 4
GRID_W = 64
N_MOD = 6
RMS_EPS = 1e-6
POS_BASE = 10000.0
GLA_HEADS = 4
GLA_DK = 64
GLA_DV = 128
GLA_RANK = 16
GLA_GATE_NORM = 16.0
FOURIER_GROUPS = 4
FOURIER_GW = 64
FOURIER_W = FOURIER_GROUPS * FOURIER_GW
LRU_BLOCKS = 4
LRU_BW = 64
LRU_W = LRU_BLOCKS * LRU_BW
LRU_C = 8.0
CONV_W = 4
N_EXPERTS = 16
EXPERT_FF = 1024
CAPACITY_FACTOR = 2
GLA_QK_W = GLA_HEADS * GLA_DK
GLA_V_W = GLA_HEADS * GLA_DV
GLA_LR_W = 2 * GLA_RANK
MIX_W = GLA_V_W + FOURIER_W + LRU_W
IN_W = 2 * GLA_QK_W + 2 * GLA_V_W + GLA_LR_W + FOURIER_W + 2 * LRU_W
_C_OG_END = 2 * GLA_QK_W + 2 * GLA_V_W
_C_LR_END = _C_OG_END + GLA_LR_W

LANES = 128
SUBLANES = 8
VMEM_LIMIT = 56 * 1024 * 1024

TOKEN_TILE = 512
OUT_TILE = 256
GLA_CH = 256
ROW_SUB = D_MODEL // LANES
MOE_ROWS = 256
GATHER_UNROLL = 8
DIGIT = 32
COMBINE_TILE = 256
COMBINE_CHUNK = 64
COMBINE_DEPTH = 512
BF16_ROWS = 16


def _cparams(*sem):
    return pltpu.CompilerParams(dimension_semantics=sem, vmem_limit_bytes=VMEM_LIMIT)


def _dot(a, b):
    return jnp.dot(a, b, preferred_element_type=F32)


def _dot_nt(a, b):
    return lax.dot_general(a, b, (((1,), (1,)), ((), ())), preferred_element_type=F32)


def _dot_tn(a, b):
    return lax.dot_general(a, b, (((0,), (0,)), ((), ())), preferred_element_type=F32)


def _split(x):
    hi = x.astype(BF16)
    lo = (x - hi.astype(F32)).astype(BF16)
    return hi, lo


def _sigmoid(x):
    return 1.0 / (1.0 + jnp.exp(-x))


def _rms(x):
    return x * lax.rsqrt(jnp.mean(x * x, axis=-1, keepdims=True) + RMS_EPS)


def _mod_kernel(cond_ref, w_ref, b_ref, o_ref):
    a = cond_ref[...]
    a = a * _sigmoid(a)
    o_ref[0] = _dot(a.astype(BF16), w_ref[0].astype(BF16)) + b_ref[0]


def _modulation(cond, w_mod, b_mod):
    tn = 1536
    nw = N_MOD * D_MODEL
    return pl.pallas_call(
        _mod_kernel,
        grid=(DEPTH, nw // tn),
        in_specs=[pl.BlockSpec((SUBLANES, D_MODEL), lambda l, j: (0, 0)),
                  pl.BlockSpec((1, D_MODEL, tn), lambda l, j: (l, 0, j)),
                  pl.BlockSpec((1, 1, tn), lambda l, j: (l, 0, j))],
        out_specs=pl.BlockSpec((1, SUBLANES, tn), lambda l, j: (l, 0, j)),
        out_shape=jax.ShapeDtypeStruct((DEPTH, SUBLANES, nw), F32),
        compiler_params=_cparams("arbitrary", "arbitrary"),
        name="modulation",
    )(cond, w_mod, b_mod.reshape(DEPTH, 1, nw))


def _inproj_kernel(*refs, mode):
    if mode == "plain":
        x_ref, mod_ref, n1_ref, w_ref = refs[:4]
        outs = refs[4:]
        x = x_ref[...]
    else:
        x_ref, pos_ref, mod_ref, n1_ref, w_ref, xo_ref = refs[:6]
        outs = refs[6:]
        x = x_ref[...] + pos_ref[...]
        xo_ref[...] = x
    qk_ref, v_ref, og_ref, uf_ref, ux_ref, ug_ref, lr_ref, wsc = outs

    @pl.when(pl.program_id(0) == 0)
    def _():
        for r in range(0, D_MODEL, 256):
            wsc[r:r + 256, 0:_C_OG_END] = w_ref[r:r + 256, 0:_C_OG_END].astype(BF16)
            wsc[r:r + 256, _C_OG_END:_C_OG_END + 768] = w_ref[r:r + 256, _C_LR_END:IN_W].astype(BF16)
            lrw = w_ref[r:r + 256, _C_OG_END:_C_LR_END].astype(BF16)
            wsc[r:r + 256, _C_OG_END + 768:_C_OG_END + 896] = jnp.concatenate(
                [lrw, jnp.zeros((256, LANES - GLA_LR_W), BF16)], axis=1)

    m = mod_ref[0]
    h = _rms(x) * n1_ref[...] * (1.0 + m[:, D_MODEL:2 * D_MODEL]) + m[:, 0:D_MODEL]
    hb = h.astype(BF16)
    qk_ref[...] = _dot(hb, wsc[:, 0:512])
    v_ref[...] = _dot(hb, wsc[:, 512:1024])
    og_ref[...] = _dot(hb, wsc[:, 1024:1536])
    uf_ref[...] = _dot(hb, wsc[:, 1536:1792])
    ux_ref[...] = _dot(hb, wsc[:, 1792:2048])
    ug_ref[...] = _dot(hb, wsc[:, 2048:2304])
    lr_ref[...] = _dot(hb, wsc[:, 2304:2432])


def _inproj(mode, x, extra, mod_l, norm1_l, w_in, l, B, L):
    T = B * L
    tm = TOKEN_TILE
    tpb = L // tm
    per_batch = mod_l.shape[0] > 1
    bidx = (lambda i: (i // tpb, 0, 0)) if per_batch else (lambda i: (0, 0, 0))
    row = lambda i: (i, 0)
    in_specs = [pl.BlockSpec((tm, D_MODEL), row)]
    args = [x]
    if mode == "pos":
        in_specs.append(pl.BlockSpec((tm, D_MODEL), lambda i: (i % tpb, 0)))
        args.append(extra)
    in_specs += [pl.BlockSpec((1, 1, N_MOD * D_MODEL), bidx),
                 pl.BlockSpec((1, D_MODEL), lambda i: (0, 0)),
                 pl.BlockSpec((None, D_MODEL, IN_W), lambda i: (l, 0, 0))]
    args += [mod_l, norm1_l.reshape(1, D_MODEL), w_in]
    widths = [512, 512, 512, 256, 256, 256, LANES]
    out_specs = [pl.BlockSpec((tm, w), row) for w in widths]
    out_shape = [jax.ShapeDtypeStruct((T, w), F32) for w in widths]
    if mode != "plain":
        out_specs = [pl.BlockSpec((tm, D_MODEL), row)] + out_specs
        out_shape = [jax.ShapeDtypeStruct((T, D_MODEL), F32)] + out_shape
    res = pl.pallas_call(
        functools.partial(_inproj_kernel, mode=mode),
        grid=(T // tm,),
        in_specs=in_specs,
        out_specs=out_specs,
        out_shape=out_shape,
        scratch_shapes=[pltpu.VMEM((D_MODEL, 2432), BF16)],
        compiler_params=_cparams("arbitrary"),
        name="inproj_" + mode,
    )(*args)
    if mode == "plain":
        return (x,) + tuple(res)
    return tuple(res)


def _gla_kernel(*refs, L, has_s0, want_state):
    qk_ref, v_ref, og_ref, lr_ref, wdec_ref, bdec_ref, gn_ref = refs[:7]
    p = 7
    s0_ref = None
    if has_s0:
        s0_ref = refs[p]
        p += 1
    o_ref = refs[p]
    p += 1
    sn_ref = None
    if want_state:
        sn_ref = refs[p]
        p += 1
    g_scr, oacc, s_scr = refs[p:p + 3]

    ch = GLA_CH
    n_chunks = L // ch
    kw = GLA_QK_W
    vw = GLA_V_W

    z16 = jnp.zeros((GLA_RANK, kw), F32)
    wc = jnp.concatenate([
        jnp.concatenate([wdec_ref[0], z16], axis=1),
        jnp.concatenate([z16, wdec_ref[1]], axis=1),
        jnp.zeros((LANES - GLA_LR_W, 2 * kw), F32)], axis=0).astype(BF16)
    bias = jnp.concatenate([bdec_ref[0], bdec_ref[1]], axis=1)
    z = _dot(lr_ref[0].astype(BF16), wc) + bias
    g_scr[...] = (jnp.minimum(z, 0.0) - jnp.log1p(jnp.exp(-jnp.abs(z)))) * (1.0 / GLA_GATE_NORM)

    row = lax.broadcasted_iota(I32, (ch, ch), 0)
    col = lax.broadcasted_iota(I32, (ch, ch), 1)
    lane_head = lax.broadcasted_iota(I32, (1, kw), 1) // GLA_DK
    blockdiag = (lax.broadcasted_iota(I32, (kw, vw), 0) // GLA_DK) == (lax.broadcasted_iota(I32, (kw, vw), 1) // GLA_DV)
    ones_t = jnp.ones((ch, LANES), BF16)
    gn = gn_ref[...]

    def finish(o, r0):
        parts = []
        for h in range(GLA_HEADS):
            parts.append(_rms(o[:, h * GLA_DV:(h + 1) * GLA_DV]) * gn)
        ogv = og_ref[0, r0:r0 + ch, :]
        return (jnp.concatenate(parts, axis=1) * (ogv * _sigmoid(ogv))).astype(BF16)

    for d in range(2):
        allowed = (col <= row) if d == 0 else (col >= row)
        tri = jnp.where(allowed, 1.0, 0.0).astype(BF16)
        if has_s0:
            s_scr[...] = jnp.zeros((kw, vw), F32)
            for h in range(GLA_HEADS):
                s_scr[h * GLA_DK:(h + 1) * GLA_DK, h * GLA_DV:(h + 1) * GLA_DV] = s0_ref[0, d, h]
        for i in range(n_chunks):
            n = i if d == 0 else n_chunks - 1 - i
            r0 = n * ch
            state_is_zero = (i == 0) and not has_s0
            gch = g_scr[r0:r0 + ch, d * kw:(d + 1) * kw]
            g_hi, g_lo = _split(gch)
            b = _dot(tri, g_hi) + _dot(tri, g_lo)
            b_last = b[ch - 1:ch, :] if d == 0 else b[0:1, :]
            bc = b - b[ch // 2:ch // 2 + 1, :]
            qch = qk_ref[0, r0:r0 + ch, 0:kw] * (GLA_DK ** -0.5)
            kch = qk_ref[0, r0:r0 + ch, kw:2 * kw]
            vb = v_ref[0, r0:r0 + ch, :].astype(BF16)
            q_s = (qch * jnp.exp(bc)).astype(BF16)
            k_s = (kch * jnp.exp(-bc)).astype(BF16)
            zero_q = jnp.zeros_like(q_s)
            qbig = jnp.concatenate([jnp.where(lane_head == h, q_s, zero_q) for h in range(GLA_HEADS)], axis=0)
            scores = _dot_nt(qbig, k_s)
            parts = []
            for h in range(GLA_HEADS):
                ph = jnp.where(allowed, scores[h * ch:(h + 1) * ch, :], 0.0).astype(BF16)
                parts.append(_dot(ph, vb[:, h * GLA_DV:(h + 1) * GLA_DV]))
            o = jnp.concatenate(parts, axis=1)
            if not state_is_zero:
                q_t = (qch * jnp.exp(b)).astype(BF16)
                o = o + _dot(q_t, s_scr[...].astype(BF16))
            if (i < n_chunks - 1) or want_state:
                k_d = (kch * jnp.exp(b_last - b)).astype(BF16)
                ds = jnp.where(blockdiag, _dot_tn(k_d, vb), 0.0)
                if state_is_zero:
                    s_scr[...] = ds
                else:
                    dcol = _dot_tn(g_hi, ones_t) + _dot_tn(g_lo, ones_t)
                    dec = jnp.exp(dcol)
                    s_scr[...] = s_scr[...] * jnp.concatenate([dec] * (vw // LANES), axis=1) + ds
            if d == 0:
                oacc[r0:r0 + ch, :] = o
            else:
                o_ref[0, r0:r0 + ch, :] = finish(o + oacc[r0:r0 + ch, :], r0)
        if want_state:
            for h in range(GLA_HEADS):
                sn_ref[0, d, h] = s_scr[h * GLA_DK:(h + 1) * GLA_DK, h * GLA_DV:(h + 1) * GLA_DV]


def _gla(qk, v, og, lr, wdec_l, bdec_l, gn_l, s0, B, L, want_state):
    has_s0 = s0 is not None
    blk = lambda w: pl.BlockSpec((1, L, w), lambda b: (b, 0, 0))
    in_specs = [blk(512), blk(512), blk(512), blk(LANES),
                pl.BlockSpec((2, GLA_RANK, GLA_QK_W), lambda b: (0, 0, 0)),
                pl.BlockSpec((2, 1, GLA_QK_W), lambda b: (0, 0, 0)),
                pl.BlockSpec((1, GLA_DV), lambda b: (0, 0))]
    args = [qk.reshape(B, L, 512), v.reshape(B, L, 512), og.reshape(B, L, 512), lr.reshape(B, L, LANES),
            wdec_l, bdec_l.reshape(2, 1, GLA_QK_W), gn_l.reshape(1, GLA_DV)]
    st_spec = pl.BlockSpec((1, 2, GLA_HEADS, GLA_DK, GLA_DV), lambda b: (b, 0, 0, 0, 0))
    if has_s0:
        in_specs.append(st_spec)
        args.append(s0)
    out_specs = [pl.BlockSpec((1, L, GLA_V_W), lambda b: (b, 0, 0))]
    out_shape = [jax.ShapeDtypeStruct((B, L, GLA_V_W), BF16)]
    if want_state:
        out_specs.append(st_spec)
        out_shape.append(jax.ShapeDtypeStruct((B, 2, GLA_HEADS, GLA_DK, GLA_DV), F32))
    res = pl.pallas_call(
        functools.partial(_gla_kernel, L=L, has_s0=has_s0, want_state=want_state),
        grid=(B,),
        in_specs=in_specs,
        out_specs=out_specs,
        out_shape=out_shape,
        scratch_shapes=[pltpu.VMEM((L, 2 * GLA_QK_W), F32),
                        pltpu.VMEM((L, GLA_V_W), F32),
                        pltpu.VMEM((GLA_QK_W, GLA_V_W), F32)],
        compiler_params=_cparams("arbitrary"),
        name="gla",
    )(*args)
    o = res[0].reshape(B * L, GLA_V_W)
    return o, (res[1] if want_state else None)


def _fft_tables(L):
    m = np.arange(L, dtype=np.int64)
    ang = 2.0 * np.pi * ((m[:, None] * m[None, :]) % L) / L
    cc = np.concatenate([np.cos(ang), -np.sin(ang)], axis=1)
    c = np.arange(FOURIER_GW, dtype=np.int64)
    angc = 2.0 * np.pi * ((c[:, None] * c[None, :]) % FOURIER_GW) / FOURIER_GW
    scale = 1.0 / math.sqrt(L * FOURIER_GW)
    eye = np.eye(FOURIER_GROUPS)
    bdc = np.kron(eye, np.cos(angc) * scale)
    bds = np.kron(eye, np.sin(angc) * scale)
    return (jnp.asarray(cc, dtype=F32), jnp.asarray(bdc, dtype=F32), jnp.asarray(bds, dtype=F32))


def _fft_kernel(u_ref, cc_ref, bdc_ref, bds_ref, o_ref):
    u_hi, u_lo = _split(u_ref[0])
    bdc = bdc_ref[...].astype(BF16)
    bds = bds_ref[...].astype(BF16)
    uc = _dot(u_hi, bdc) + _dot(u_lo, bdc)
    us = _dot(u_hi, bds) + _dot(u_lo, bds)
    w_hi, w_lo = _split(jnp.concatenate([uc, us], axis=0))
    cc = cc_ref[...].astype(BF16)
    o_ref[0] = (_dot(cc, w_hi) + _dot(cc, w_lo)).astype(BF16)


def _fft(uf, B, L):
    cc, bdc, bds = _fft_tables(L)
    res = pl.pallas_call(
        _fft_kernel,
        grid=(B,),
        in_specs=[pl.BlockSpec((1, L, FOURIER_W), lambda b: (b, 0, 0)),
                  pl.BlockSpec((L, 2 * L), lambda b: (0, 0)),
                  pl.BlockSpec((FOURIER_W, FOURIER_W), lambda b: (0, 0)),
                  pl.BlockSpec((FOURIER_W, FOURIER_W), lambda b: (0, 0))],
        out_specs=pl.BlockSpec((1, L, FOURIER_W), lambda b: (b, 0, 0)),
        out_shape=jax.ShapeDtypeStruct((B, L, FOURIER_W), BF16),
        compiler_params=_cparams("arbitrary"),
        name="fourier",
    )(uf.reshape(B, L, FOURIER_W), cc, bdc, bds)
    return res.reshape(B * L, FOURIER_W)


def _lru_kernel(*refs, L, has_s0, want_state):
    ux_ref, ug_ref, cw_ref, cb_ref, wa_ref, ba_ref, wx_ref, bx_ref, lam_ref = refs[:9]
    p = 9
    s0_ref = None
    if has_s0:
        s0_ref = refs[p]
        p += 1
    o_ref = refs[p]
    p += 1
    sn_ref = None
    if want_state:
        sn_ref = refs[p]
        p += 1
    bd_scr = refs[p]
    scan_scr = refs[p + 1:p + 9]

    @pl.when(pl.program_id(0) == 0)
    def _():
        r = lax.broadcasted_iota(I32, (LRU_BW, LRU_W), 0)
        c = lax.broadcasted_iota(I32, (LRU_BW, LRU_W), 1)
        for d in range(2):
            for gi, w_ref in enumerate((wa_ref, wx_ref)):
                pieces = []
                for h in range(LRU_BLOCKS):
                    place = jnp.where(c == r + h * LRU_BW, 1.0, 0.0).astype(BF16)
                    pieces.append(_dot(w_ref[d, h].astype(BF16), place))
                bd_scr[2 * d + gi] = jnp.concatenate(pieces, axis=0).astype(BF16)

    t = lax.broadcasted_iota(I32, (L, 1), 0)
    x = ux_ref[0]
    xm2 = jnp.where(t >= 2, pltpu.roll(x, 2, 0), 0.0)
    xm1 = jnp.where(t >= 1, pltpu.roll(x, 1, 0), 0.0)
    xp1 = jnp.where(t <= L - 2, pltpu.roll(x, L - 1, 0), 0.0)
    xc = xm2 * cw_ref[0:1, :] + xm1 * cw_ref[1:2, :] + x * cw_ref[2:3, :] + xp1 * cw_ref[3:4, :] + cb_ref[...]
    xcb = xc.astype(BF16)

    nb = L // SUBLANES
    pitch = nb + SUBLANES
    n_slab = LRU_W // LANES
    sub = lax.broadcasted_iota(I32, (SUBLANES, LANES), 0)
    hsum = [[None] * n_slab for _ in range(SUBLANES)]
    for d in range(2):
        r = 0.5 + 0.5 * jnp.tanh(0.5 * (_dot(xcb, bd_scr[2 * d]) + ba_ref[d]))
        ig = 0.5 + 0.5 * jnp.tanh(0.5 * (_dot(xcb, bd_scr[2 * d + 1]) + bx_ref[d]))
        lam = lam_ref[d]
        softplus = jnp.maximum(-lam, 0.0) + jnp.log1p(jnp.exp(-jnp.abs(lam)))
        a = jnp.exp(-LRU_C * r * softplus)
        u = jnp.sqrt(1.0 - a * a) * (ig * xc)
        a_scr, u_scr, h_scr, p_scr = scan_scr[4 * d:4 * d + 4]
        for s in range(SUBLANES):
            for k in range(n_slab):
                a_scr[k, s * pitch:s * pitch + nb, :] = a[s * nb:(s + 1) * nb, k * LANES:(k + 1) * LANES]
                u_scr[k, s * pitch:s * pitch + nb, :] = u[s * nb:(s + 1) * nb, k * LANES:(k + 1) * LANES]
        steps = range(nb) if d == 0 else range(nb - 1, -1, -1)
        for k in range(n_slab):
            h = jnp.zeros((SUBLANES, LANES), F32)
            prod = jnp.ones((SUBLANES, LANES), F32)
            for i in steps:
                rows = pl.ds(i, SUBLANES, stride=pitch)
                ai = a_scr[k, rows, :]
                h = ai * h + u_scr[k, rows, :]
                prod = ai * prod
                h_scr[k, rows, :] = h
                p_scr[k, rows, :] = prod
            if has_s0:
                h0 = jnp.broadcast_to(s0_ref[0, d:d + 1, k * LANES:(k + 1) * LANES], (SUBLANES, LANES))
            else:
                h0 = jnp.zeros((SUBLANES, LANES), F32)
            first = 0 if d == 0 else SUBLANES - 1
            carry = jnp.where(sub == first, h0, 0.0)
            for j in range(1, SUBLANES):
                s = j if d == 0 else SUBLANES - 1 - j
                moved = pltpu.roll(prod * carry + h, 1 if d == 0 else SUBLANES - 1, 0)
                carry = jnp.where(sub == s, moved, carry)
            if want_state:
                last = SUBLANES - 1 - first
                sn_ref[0, d:d + 1, k * LANES:(k + 1) * LANES] = (prod * carry + h)[last:last + 1, :]
            for s in range(SUBLANES):
                blk = h_scr[k, s * pitch:s * pitch + nb, :] + p_scr[k, s * pitch:s * pitch + nb, :] * carry[s:s + 1, :]
                hsum[s][k] = blk if hsum[s][k] is None else hsum[s][k] + blk

    for s in range(SUBLANES):
        ugv = ug_ref[0, s * nb:(s + 1) * nb, :]
        gelu = 0.5 * ugv * (1.0 + jnp.tanh(math.sqrt(2.0 / math.pi) * (ugv + 0.044715 * (ugv * ugv * ugv))))
        o_ref[0, s * nb:(s + 1) * nb, :] = (jnp.concatenate(hsum[s], axis=1) * gelu).astype(BF16)


def _lru(ux, ug, cw_l, cb_l, wa_l, ba_l, wx_l, bx_l, lam_l, s0, B, L, want_state):
    has_s0 = s0 is not None
    blk = pl.BlockSpec((1, L, LRU_W), lambda b: (b, 0, 0))
    vec2 = pl.BlockSpec((2, 1, LRU_W), lambda b: (0, 0, 0))
    wsp = pl.BlockSpec((2, LRU_BLOCKS, LRU_BW, LRU_BW), lambda b: (0, 0, 0, 0))
    in_specs = [blk, blk,
                pl.BlockSpec((CONV_W, LRU_W), lambda b: (0, 0)),
                pl.BlockSpec((1, LRU_W), lambda b: (0, 0)),
                wsp, vec2, wsp, vec2, vec2]
    args = [ux.reshape(B, L, LRU_W), ug.reshape(B, L, LRU_W), cw_l, cb_l.reshape(1, LRU_W),
            wa_l, ba_l.reshape(2, 1, LRU_W), wx_l, bx_l.reshape(2, 1, LRU_W), lam_l.reshape(2, 1, LRU_W)]
    st_spec = pl.BlockSpec((1, 2, LRU_W), lambda b: (b, 0, 0))
    if has_s0:
        in_specs.append(st_spec)
        args.append(s0)
    out_specs = [blk]
    out_shape = [jax.ShapeDtypeStruct((B, L, LRU_W), BF16)]
    if want_state:
        out_specs.append(st_spec)
        out_shape.append(jax.ShapeDtypeStruct((B, 2, LRU_W), F32))
    res = pl.pallas_call(
        functools.partial(_lru_kernel, L=L, has_s0=has_s0, want_state=want_state),
        grid=(B,),
        in_specs=in_specs,
        out_specs=out_specs,
        out_shape=out_shape,
        scratch_shapes=[pltpu.VMEM((4, LRU_W, LRU_W), BF16)]
        + [pltpu.VMEM((LRU_W // LANES, L + SUBLANES * SUBLANES, LANES), F32)] * 8,
        compiler_params=_cparams("arbitrary"),
        name="rglru",
    )(*args)
    return res[0].reshape(B * L, LRU_W), (res[1] if want_state else None)


def _outproj_kernel(og_ref, of_ref, ol_ref, x_ref, mod_ref, n2_ref, wout_ref, wr_ref,
                    x1_ref, hx_ref, afft_ref, wsc, wrs):
    @pl.when(pl.program_id(0) == 0)
    def _():
        for r in range(0, MIX_W, 256):
            wsc[r:r + 256, :] = wout_ref[r:r + 256, :].astype(BF16)
        wrs[...] = jnp.concatenate([wr_ref[...], jnp.zeros((D_MODEL, LANES - N_EXPERTS), F32)], axis=1)

    m = mod_ref[0]
    y = (_dot(og_ref[...], wsc[0:GLA_V_W, :])
         + _dot(of_ref[...], wsc[GLA_V_W:GLA_V_W + FOURIER_W, :])
         + _dot(ol_ref[...], wsc[GLA_V_W + FOURIER_W:MIX_W, :]))
    x1 = x_ref[...] + m[:, 2 * D_MODEL:3 * D_MODEL] * y
    x1_ref[...] = x1
    h2 = _rms(x1) * n2_ref[...] * (1.0 + m[:, 4 * D_MODEL:5 * D_MODEL]) + m[:, 3 * D_MODEL:4 * D_MODEL]
    h_hi = h2.astype(BF16)
    h_hi32 = h_hi.astype(F32)
    tm = h2.shape[0]
    for s in range(ROW_SUB):
        hx_ref[pl.ds(s, tm, stride=ROW_SUB), :] = h_hi32[:, s * LANES:(s + 1) * LANES]
    h_lo = (h2 - h_hi32).astype(BF16)
    w_hi, w_lo = _split(wrs[...])
    logits = _dot(h_hi, w_hi) + _dot(h_lo, w_hi) + _dot(h_hi, w_lo)
    lane = lax.broadcasted_iota(I32, logits.shape, 1)
    logits = jnp.where(lane < N_EXPERTS, logits, -jnp.inf)
    ex = jnp.exp(logits - jnp.max(logits, axis=-1, keepdims=True))
    aff = ex / jnp.sum(ex, axis=-1, keepdims=True)
    afft_ref[...] = aff.T[0:N_EXPERTS, :]


def _outproj(o_gla, o_fft, o_lru, x, mod_l, norm2_l, w_out, w_router_l, l, B, L):
    T = B * L
    tm = OUT_TILE
    tpb = L // tm
    per_batch = mod_l.shape[0] > 1
    bidx = (lambda i: (i // tpb, 0, 0)) if per_batch else (lambda i: (0, 0, 0))
    row = lambda i: (i, 0)
    return pl.pallas_call(
        _outproj_kernel,
        grid=(T // tm,),
        in_specs=[pl.BlockSpec((tm, GLA_V_W), row), pl.BlockSpec((tm, FOURIER_W), row),
                  pl.BlockSpec((tm, LRU_W), row), pl.BlockSpec((tm, D_MODEL), row),
                  pl.BlockSpec((1, 1, N_MOD * D_MODEL), bidx),
                  pl.BlockSpec((1, D_MODEL), lambda i: (0, 0)),
                  pl.BlockSpec((None, MIX_W, D_MODEL), lambda i: (l, 0, 0)),
                  pl.BlockSpec((D_MODEL, N_EXPERTS), lambda i: (0, 0))],
        out_specs=[pl.BlockSpec((tm, D_MODEL), row),
                   pl.BlockSpec((tm * ROW_SUB, LANES), lambda i: (i, 0)),
                   pl.BlockSpec((N_EXPERTS, tm), lambda i: (0, i))],
        out_shape=[jax.ShapeDtypeStruct((T, D_MODEL), F32),
                   jax.ShapeDtypeStruct((T * ROW_SUB, LANES), F32),
                   jax.ShapeDtypeStruct((N_EXPERTS, T), F32)],
        scratch_shapes=[pltpu.VMEM((MIX_W, D_MODEL), BF16), pltpu.VMEM((D_MODEL, LANES), F32)],
        compiler_params=_cparams("arbitrary"),
        name="outproj",
    )(o_gla, o_fft, o_lru, x, mod_l, norm2_l.reshape(1, D_MODEL), w_out, w_router_l)


def _prefix_lanes(x):
    T = x.shape[1]
    w = 256
    nb = T // w
    stacked = jnp.concatenate([x[:, j * w:(j + 1) * w] for j in range(nb)], axis=0)
    upper = jnp.where(lax.broadcasted_iota(I32, (w, w), 0) <= lax.broadcasted_iota(I32, (w, w), 1), 1.0, 0.0)
    pe = _dot(stacked, upper.astype(BF16))
    carry = jnp.zeros((N_EXPERTS, 1), F32)
    outs = []
    for j in range(nb):
        blk = pe[j * N_EXPERTS:(j + 1) * N_EXPERTS, :]
        outs.append(blk + carry)
        carry = carry + blk[:, w - 1:w]
    return jnp.concatenate(outs, axis=1)


def _topk_kernel(aff_ref, out_ref, gate_ref, slot_ref, bnd_ref, *, T, C):
    n_a = C // DIGIT
    aff = aff_ref[...]
    bits = jnp.zeros((N_EXPERTS, 1), I32)
    for bit in range(30, -1, -1):
        cand = bits | (1 << bit)
        cnt = jnp.sum(jnp.where(aff >= pltpu.bitcast(cand, F32), 1.0, 0.0), axis=1, keepdims=True)
        bits = jnp.where(cnt >= C, cand, bits)
    thr = pltpu.bitcast(bits, F32)
    gt = aff > thr
    eq = aff == thr
    eqf = jnp.where(eq, 1.0, 0.0)
    need = C - jnp.sum(jnp.where(gt, 1.0, 0.0), axis=1, keepdims=True)
    eq_before = _prefix_lanes(eqf.astype(BF16)) - eqf
    sel = gt | (eq & (eq_before < need))
    self32 = jnp.where(sel, 1.0, 0.0)
    cnt = _prefix_lanes(self32.astype(BF16))
    slot_ref[...] = jnp.where(sel, cnt - 1.0, -1.0)
    tok = lax.broadcasted_iota(I32, (1, T), 1)
    lane = lax.broadcasted_iota(I32, (1, LANES), 1)
    bnd = jnp.zeros((N_EXPERTS, LANES), F32)
    for j in range(1, T // COMBINE_TILE + 1):
        before = jnp.sum(jnp.where(tok < j * COMBINE_TILE, self32, 0.0), axis=1, keepdims=True)
        bnd = jnp.where(lane == j, before, bnd)
    bnd_ref[...] = bnd

    p_dig = jnp.floor(cnt * (1.0 / DIGIT))
    q_dig = cnt - DIGIT * p_dig
    a_col = lax.broadcasted_iota(I32, (n_a, 1), 0).astype(F32)
    b_col = lax.broadcasted_iota(I32, (DIGIT, 1), 0).astype(F32)
    slot = cnt - 1.0
    ps_dig = jnp.where(sel, jnp.floor(slot * (1.0 / DIGIT)), -1.0)
    qs_dig = slot - DIGIT * jnp.floor(slot * (1.0 / DIGIT))
    aff_hi = aff.astype(BF16).astype(F32)
    aff_lo = aff - aff_hi
    kc = min(T, 2048)
    acc = jnp.zeros((N_EXPERTS * n_a, N_EXPERTS * DIGIT), F32)
    gacc = jnp.zeros((N_EXPERTS * n_a, N_EXPERTS * DIGIT), F32)
    for c0 in range(0, T, kc):
        tk = slice(c0, c0 + kc)
        u = jnp.concatenate([jnp.where(p_dig[e:e + 1, tk] == a_col, 1.0, 0.0).astype(BF16)
                             for e in range(N_EXPERTS)], axis=0)
        v = jnp.concatenate([jnp.where(q_dig[e:e + 1, tk] <= b_col, 1.0, 0.0).astype(BF16)
                             for e in range(N_EXPERTS)], axis=0)
        acc = acc + _dot_nt(u, v)
        us = jnp.concatenate([jnp.where(ps_dig[e:e + 1, tk] == a_col, 1.0, 0.0).astype(BF16)
                              for e in range(N_EXPERTS)], axis=0)
        for part in (aff_hi, aff_lo):
            vs = jnp.concatenate([jnp.where(qs_dig[e:e + 1, tk] == b_col, part[e:e + 1, tk], 0.0).astype(BF16)
                                  for e in range(N_EXPERTS)], axis=0)
            gacc = gacc + _dot_nt(us, vs)
    below = jnp.concatenate([jnp.sum(jnp.where(p_dig[e:e + 1, :] < a_col, 1.0, 0.0), axis=1, keepdims=True)
                             for e in range(N_EXPERTS)], axis=0)
    r_i = lax.broadcasted_iota(I32, acc.shape, 0) // n_a
    c_i = lax.broadcasted_iota(I32, acc.shape, 1) // DIGIT

    def own_block(x):
        x = jnp.where(r_i == c_i, x, 0.0)
        x = x[:, 0:256] + x[:, 256:512]
        x = x[:, 0:LANES] + x[:, LANES:2 * LANES]
        x = x + pltpu.roll(x, 64, 1)
        return x + pltpu.roll(x, 32, 1)

    out_ref[...] = own_block(acc) + below
    gate_ref[...] = own_block(gacc)


def _expert_choice(aff_t, T, C):
    n_a = C // DIGIT
    res = pl.pallas_call(
        functools.partial(_topk_kernel, T=T, C=C),
        grid=(1,),
        in_specs=[pl.BlockSpec((N_EXPERTS, T), lambda i: (0, 0))],
        out_specs=[pl.BlockSpec((N_EXPERTS * n_a, LANES), lambda i: (0, 0)),
                   pl.BlockSpec((N_EXPERTS * n_a, LANES), lambda i: (0, 0)),
                   pl.BlockSpec((N_EXPERTS, T), lambda i: (0, 0)),
                   pl.BlockSpec((N_EXPERTS, LANES), lambda i: (0, 0))],
        out_shape=[jax.ShapeDtypeStruct((N_EXPERTS * n_a, LANES), F32),
                   jax.ShapeDtypeStruct((N_EXPERTS * n_a, LANES), F32),
                   jax.ShapeDtypeStruct((N_EXPERTS, T), F32),
                   jax.ShapeDtypeStruct((N_EXPERTS, LANES), F32)],
        compiler_params=_cparams("arbitrary"),
        name="expert_choice",
    )(aff_t)
    idx = res[0][:, 0:DIGIT].astype(I32).reshape(N_EXPERTS, C)
    gate = res[1][:, 0:DIGIT].reshape(N_EXPERTS, 1, C)
    bnd = res[3][:, 0:T // COMBINE_TILE + 1].astype(I32)
    return idx, gate, res[2], bnd


def _ffn_kernel(idx_ref, hx_hbm, gate_ref, wg_ref, wu_ref, wd_ref, y_ref, xbuf, wgb, wub, wdb, sems, *, C):
    e = pl.program_id(0)
    last = pl.num_programs(0) - 1
    buf = e % 2
    nxt = 1 - buf
    following = jnp.where(e == last, 0, e + 1)

    def fetch_row(expert, c, into, k):
        src = hx_hbm.at[pl.ds(pl.multiple_of(idx_ref[expert, c] * ROW_SUB, ROW_SUB), ROW_SUB)]
        dst = xbuf.at[into, pl.ds(pl.multiple_of(c * ROW_SUB, ROW_SUB), ROW_SUB)]
        pltpu.make_async_copy(src, dst, sems.at[into]).start(priority=k % 2)

    @pl.when(e == 0)
    def _():
        def body(i, carry):
            for k in range(GATHER_UNROLL):
                fetch_row(0, i * GATHER_UNROLL + k, 0, k)
            return carry
        lax.fori_loop(0, C // GATHER_UNROLL, body, 0)

    for r in range(0, D_MODEL, 256):
        wgb[r:r + 256, :] = wg_ref[r:r + 256, :].astype(BF16)
        wub[r:r + 256, :] = wu_ref[r:r + 256, :].astype(BF16)
        wdb[r:r + 256, :] = wd_ref[r:r + 256, :].astype(BF16)
    pltpu.make_async_copy(hx_hbm.at[pl.ds(0, C * ROW_SUB)], xbuf.at[buf], sems.at[buf]).wait()

    step = min(MOE_ROWS, C)
    for r0 in range(0, C, step):
        for k in range(step):
            fetch_row(following, r0 + k, nxt, k)
        x = jnp.concatenate([xbuf[buf, pl.ds(r0 * ROW_SUB + s, step, stride=ROW_SUB), :] for s in range(ROW_SUB)],
                            axis=1).astype(BF16)
        g = _dot(x, wgb[...])
        u = _dot(x, wub[...])
        hid = (g * _sigmoid(g) * u).astype(BF16)
        gate = jnp.broadcast_to(gate_ref[:, r0:r0 + step], (LANES, step)).T[:, 0:1]
        y_ref[pl.ds(r0, step), :] = (_dot(hid, wdb[...]) * gate).astype(BF16)

    @pl.when(e == last)
    def _():
        pltpu.make_async_copy(hx_hbm.at[pl.ds(0, C * ROW_SUB)], xbuf.at[nxt], sems.at[nxt]).wait()


def _expert_ffn(idx, hx, gate, wg, wu, wd, l, C):
    wspec = pl.BlockSpec((None, None, D_MODEL, EXPERT_FF), lambda e, idx_ref: (l, e, 0, 0))
    return pl.pallas_call(
        functools.partial(_ffn_kernel, C=C),
        grid_spec=pltpu.PrefetchScalarGridSpec(
            num_scalar_prefetch=1,
            grid=(N_EXPERTS,),
            in_specs=[pl.BlockSpec(memory_space=pl.ANY),
                      pl.BlockSpec((None, 1, C), lambda e, idx_ref: (e, 0, 0)), wspec, wspec,
                      pl.BlockSpec((None, None, EXPERT_FF, D_MODEL), lambda e, idx_ref: (l, e, 0, 0))],
            out_specs=pl.BlockSpec((None, C, D_MODEL), lambda e, idx_ref: (e, 0, 0)),
            scratch_shapes=[pltpu.VMEM((2, C * ROW_SUB, LANES), F32),
                            pltpu.VMEM((D_MODEL, EXPERT_FF), BF16),
                            pltpu.VMEM((D_MODEL, EXPERT_FF), BF16),
                            pltpu.VMEM((EXPERT_FF, D_MODEL), BF16),
                            pltpu.SemaphoreType.DMA((2,))]),
        out_shape=jax.ShapeDtypeStruct((N_EXPERTS, C, D_MODEL), BF16),
        compiler_params=pltpu.CompilerParams(dimension_semantics=("arbitrary",), vmem_limit_bytes=VMEM_LIMIT,
                                             disable_bounds_checks=True),
        name="expert_ffn",
    )(idx, hx, gate, wg, wu, wd)


def _combine_kernel(bnd_ref, slot_ref, y_hbm, x1_ref, gmod_ref, fn_ref, o_ref, ybuf, onehot, acc, sems, *, C, final):
    j = pl.program_id(0)
    cur = j % 2
    ch = COMBINE_CHUNK
    group = COMBINE_DEPTH // ch
    tt = x1_ref.shape[0]
    w_col = lax.broadcasted_iota(I32, (ch, 1), 0)

    def chunks_of(tile, e):
        first = bnd_ref[e, tile] & (-BF16_ROWS)
        return first, lax.div(bnd_ref[e, tile + 1] - first + (ch - 1), jnp.int32(ch))

    def padded(total):
        return lax.div(total + (group - 1), jnp.int32(group)) * group

    def stage(tile, into):
        tok0 = pl.multiple_of(tile * tt, tt)
        total = jnp.int32(0)
        for e in range(N_EXPERTS):
            first, n_chunks = chunks_of(tile, e)

            def fetch_chunk(c, carry, e=e, first=first, base=total):
                want_lo = first + c * ch
                src_row = pl.multiple_of(jnp.minimum(want_lo, C - ch), BF16_ROWS)
                dst_row = pl.multiple_of((base + c) * ch, ch)
                pltpu.make_async_copy(y_hbm.at[e, pl.ds(src_row, ch)], ybuf.at[into, pl.ds(dst_row, ch)],
                                      sems.at[into]).start()
                row = src_row + w_col
                want = jnp.where(row >= want_lo, row, -2).astype(F32)
                hit = slot_ref[e:e + 1, pl.ds(tok0, tt)] == want
                onehot[into, pl.ds(dst_row, ch), :] = jnp.where(hit, 1.0, 0.0).astype(BF16)
                return carry

            lax.fori_loop(0, n_chunks, fetch_chunk, 0)
            total = total + n_chunks

        def clear_chunk(k, carry):
            rows = pl.ds(pl.multiple_of(k * ch, ch), ch)
            onehot[into, rows, :] = jnp.zeros((ch, tt), BF16)
            ybuf[into, rows, :] = jnp.zeros((ch, D_MODEL), BF16)
            return carry

        lax.fori_loop(total, padded(total), clear_chunk, 0)

    @pl.when(j == 0)
    def _():
        stage(0, 0)

    @pl.when(j + 1 < pl.num_programs(0))
    def _():
        stage(j + 1, 1 - cur)

    total = jnp.int32(0)
    for e in range(N_EXPERTS):
        total = total + chunks_of(j, e)[1]

    def wait_chunk(k, carry):
        pltpu.make_async_copy(y_hbm.at[0, pl.ds(0, ch)], ybuf.at[cur, pl.ds(0, ch)], sems.at[cur]).wait()
        return carry

    lax.fori_loop(0, total, wait_chunk, 0)
    acc[...] = jnp.zeros(acc.shape, F32)

    def add_group(g, carry):
        rows = pl.ds(pl.multiple_of(g * COMBINE_DEPTH, COMBINE_DEPTH), COMBINE_DEPTH)
        acc[...] += _dot_tn(onehot[cur, rows, :], ybuf[cur, rows, :])
        return carry

    lax.fori_loop(0, lax.div(padded(total), jnp.int32(group)), add_group, 0)
    x = x1_ref[...] + gmod_ref[0][:, 5 * D_MODEL:6 * D_MODEL] * acc[...]
    o_ref[...] = _rms(x) * fn_ref[...] if final else x


def _combine(bnd, slot, y, x1, gmod, final_norm, final, B, L, C):
    T = B * L
    tt = COMBINE_TILE
    tpb = L // tt if L >= tt else None
    per_batch = gmod.shape[0] > 1
    if per_batch:
        bidx = lambda j, b: (j // tpb, 0, 0)
    else:
        bidx = lambda j, b: (0, 0, 0)
    max_chunks = N_EXPERTS * (tt // COMBINE_CHUNK + 2)
    max_rows = pl.cdiv(max_chunks * COMBINE_CHUNK, COMBINE_DEPTH) * COMBINE_DEPTH
    return pl.pallas_call(
        functools.partial(_combine_kernel, C=C, final=final),
        grid_spec=pltpu.PrefetchScalarGridSpec(
            num_scalar_prefetch=1,
            grid=(T // tt,),
            in_specs=[pl.BlockSpec((N_EXPERTS, T), lambda j, b: (0, 0)),
                      pl.BlockSpec(memory_space=pl.ANY),
                      pl.BlockSpec((tt, D_MODEL), lambda j, b: (j, 0)),
                      pl.BlockSpec((1, 1, N_MOD * D_MODEL), bidx),
                      pl.BlockSpec((1, D_MODEL), lambda j, b: (0, 0))],
            out_specs=pl.BlockSpec((tt, D_MODEL), lambda j, b: (j, 0)),
            scratch_shapes=[pltpu.VMEM((2, max_rows, D_MODEL), BF16),
                            pltpu.VMEM((2, max_rows, tt), BF16),
                            pltpu.VMEM((tt, D_MODEL), F32),
                            pltpu.SemaphoreType.DMA((2,))]),
        out_shape=jax.ShapeDtypeStruct((T, D_MODEL), F32),
        compiler_params=_cparams("arbitrary"),
        name="combine_final" if final else "combine",
    )(bnd, slot, y, x1, gmod, final_norm.reshape(1, D_MODEL))


def _grid_position_embedding(n_tokens):
    rows = n_tokens // GRID_W
    r, col = jnp.meshgrid(jnp.arange(rows, dtype=F32), jnp.arange(GRID_W, dtype=F32), indexing="ij")
    n_freq = D_MODEL // 4
    omega = 1.0 / (POS_BASE ** (jnp.arange(n_freq, dtype=F32) / n_freq))
    ar = r.reshape(-1)[:, None] * omega
    ac = col.reshape(-1)[:, None] * omega
    return jnp.concatenate([jnp.sin(ar), jnp.cos(ar), jnp.sin(ac), jnp.cos(ac)], axis=-1)


def _trunk(x_in, pos, mod_g, gla_s0, lru_s0, want_state, w):
    B, L, _ = x_in.shape
    T = B * L
    C = CAPACITY_FACTOR * T // N_EXPERTS
    x = x_in.reshape(T, D_MODEL)
    gla_states = []
    lru_states = []
    for l in range(DEPTH):
        mod_l = mod_g[l]
        mode = "pos" if (l == 0 and pos is not None) else "plain"
        x, qk, v, og, uf, ux, ug, lr = _inproj(mode, x, pos, mod_l, w["norm1"][l], w["w_in"], l, B, L)
        o_gla, gs = _gla(qk, v, og, lr, w["gla_w_decay"][l], w["gla_b_decay"][l], w["gla_norm"][l],
                         None if gla_s0 is None else gla_s0[:, l], B, L, want_state)
        o_fft = _fft(uf, B, L)
        o_lru, ls = _lru(ux, ug, w["lru_conv_w"][l], w["lru_conv_b"][l], w["lru_wa"][l], w["lru_ba"][l],
                         w["lru_wx"][l], w["lru_bx"][l], w["lru_lambda"][l],
                         None if lru_s0 is None else lru_s0[:, l], B, L, want_state)
        x1, hx, aff_t = _outproj(o_gla, o_fft, o_lru, x, mod_l, w["norm2"][l], w["w_out"], w["w_router"][l],
                                 l, B, L)
        idx, gate, slot, bnd = _expert_choice(aff_t, T, C)
        y = _expert_ffn(idx, hx, gate, w["w_expert_gate"], w["w_expert_up"], w["w_expert_down"], l, C)
        x = _combine(bnd, slot, y, x1, mod_l, w["final_norm"], l == DEPTH - 1, B, L, C)
        gla_states.append(gs)
        lru_states.append(ls)
    return x.reshape(B, L, D_MODEL), gla_states, lru_states


def kernel(x_prompt, x_sample, state_gla, state_rglru, c, c_ctx, w_mod, b_mod, norm1, norm2, w_in, gla_w_decay, gla_b_decay, gla_norm, lru_conv_w, lru_conv_b, lru_wa, lru_ba, lru_wx, lru_bx, lru_lambda, w_out, w_router, w_expert_gate, w_expert_up, w_expert_down, final_norm):
    w = dict(norm1=norm1, norm2=norm2, w_in=w_in, gla_w_decay=gla_w_decay, gla_b_decay=gla_b_decay,
             gla_norm=gla_norm, lru_conv_w=lru_conv_w, lru_conv_b=lru_conv_b, lru_wa=lru_wa, lru_ba=lru_ba,
             lru_wx=lru_wx, lru_bx=lru_bx, lru_lambda=lru_lambda, w_out=w_out, w_router=w_router,
             w_expert_gate=w_expert_gate, w_expert_up=w_expert_up, w_expert_down=w_expert_down,
             final_norm=final_norm)
    n_lat = c.shape[0]
    cond = jnp.concatenate([c_ctx[None, :], c, jnp.zeros((SUBLANES - 1 - n_lat, D_MODEL), F32)], axis=0)
    mod = _modulation(cond, w_mod, b_mod)
    mod_ctx = mod[:, 0:1].reshape(DEPTH, 1, 1, N_MOD * D_MODEL)
    mod_lat = mod[:, 1:1 + n_lat].reshape(DEPTH, n_lat, 1, N_MOD * D_MODEL)

    y_prompt, gla_states, lru_states = _trunk(x_prompt, None, mod_ctx, None, None, True, w)
    pos = _grid_position_embedding(x_sample.shape[1])
    y_sample, _, _ = _trunk(x_sample, pos, mod_lat, state_gla, state_rglru, False, w)
    new_state_gla = jnp.stack(gla_states, axis=1)
    new_state_rglru = jnp.stack(lru_states, axis=1)
    return (y_prompt, y_sample, new_state_gla, new_state_rglru)
```

```python
import functools
import math

import numpy as np
import jax
import jax.numpy as jnp
from jax import lax
from jax.experimental import pallas as pl
from jax.experimental.pallas import tpu as pltpu

F32 = jnp.float32
BF16 = jnp.bfloat16
I32 = jnp.int32

D_MODEL = 1024
DEPTH = 4
GRID_W = 64
N_MOD = 6
RMS_EPS = 1e-6
POS_BASE = 10000.0
GLA_HEADS = 4
GLA_DK = 64
GLA_DV = 128
GLA_RANK = 16
GLA_GATE_NORM = 16.0
FOURIER_GROUPS = 4
FOURIER_GW = 64
FOURIER_W = FOURIER_GROUPS * FOURIER_GW
LRU_BLOCKS = 4
LRU_BW = 64
LRU_W = LRU_BLOCKS * LRU_BW
LRU_C = 8.0
CONV_W = 4
N_EXPERTS = 16
EXPERT_FF = 1024
CAPACITY_FACTOR = 2
GLA_QK_W = GLA_HEADS * GLA_DK
GLA_V_W = GLA_HEADS * GLA_DV
GLA_LR_W = 2 * GLA_RANK
MIX_W = GLA_V_W + FOURIER_W + LRU_W
IN_W = 2 * GLA_QK_W + 2 * GLA_V_W + GLA_LR_W + FOURIER_W + 2 * LRU_W
_C_OG_END = 2 * GLA_QK_W + 2 * GLA_V_W
_C_LR_END = _C_OG_END + GLA_LR_W

LANES = 128
SUBLANES = 8
VMEM_LIMIT = 56 * 1024 * 1024

TOKEN_TILE = 512
OUT_TILE = 256
GLA_CH = 256
ROW_SUB = D_MODEL // LANES
MOE_ROWS = 256
GATHER_UNROLL = 8
DIGIT = 32
COMBINE_TILE = 256
COMBINE_CHUNK = 64
COMBINE_DEPTH = 512
BF16_ROWS = 16


def _cparams(*sem):
    return pltpu.CompilerParams(dimension_semantics=sem, vmem_limit_bytes=VMEM_LIMIT)


def _dot(a, b):
    return jnp.dot(a, b, preferred_element_type=F32)


def _dot_nt(a, b):
    return lax.dot_general(a, b, (((1,), (1,)), ((), ())), preferred_element_type=F32)


def _dot_tn(a, b):
    return lax.dot_general(a, b, (((0,), (0,)), ((), ())), preferred_element_type=F32)


def _split(x):
    hi = x.astype(BF16)
    lo = (x - hi.astype(F32)).astype(BF16)
    return hi, lo


def _sigmoid(x):
    return 1.0 / (1.0 + jnp.exp(-x))


def _rms(x):
    return x * lax.rsqrt(jnp.mean(x * x, axis=-1, keepdims=True) + RMS_EPS)


def _mod_kernel(cond_ref, w_ref, b_ref, o_ref):
    a = cond_ref[...]
    a = a * _sigmoid(a)
    o_ref[0] = _dot(a.astype(BF16), w_ref[0].astype(BF16)) + b_ref[0]


def _modulation(cond, w_mod, b_mod):
    tn = 1536
    nw = N_MOD * D_MODEL
    return pl.pallas_call(
        _mod_kernel,
        grid=(DEPTH, nw // tn),
        in_specs=[pl.BlockSpec((SUBLANES, D_MODEL), lambda l, j: (0, 0)),
                  pl.BlockSpec((1, D_MODEL, tn), lambda l, j: (l, 0, j)),
                  pl.BlockSpec((1, 1, tn), lambda l, j: (l, 0, j))],
        out_specs=pl.BlockSpec((1, SUBLANES, tn), lambda l, j: (l, 0, j)),
        out_shape=jax.ShapeDtypeStruct((DEPTH, SUBLANES, nw), F32),
        compiler_params=_cparams("arbitrary", "arbitrary"),
        name="modulation",
    )(cond, w_mod, b_mod.reshape(DEPTH, 1, nw))


def _inproj_kernel(*refs, mode):
    if mode == "plain":
        x_ref, mod_ref, n1_ref, w_ref = refs[:4]
        outs = refs[4:]
        x = x_ref[...]
    else:
        x_ref, pos_ref, mod_ref, n1_ref, w_ref, xo_ref = refs[:6]
        outs = refs[6:]
        x = x_ref[...] + pos_ref[...]
        xo_ref[...] = x
    qk_ref, v_ref, og_ref, uf_ref, ux_ref, ug_ref, lr_ref, wsc = outs

    @pl.when(pl.program_id(0) == 0)
    def _():
        for r in range(0, D_MODEL, 256):
            wsc[r:r + 256, 0:_C_OG_END] = w_ref[r:r + 256, 0:_C_OG_END].astype(BF16)
            wsc[r:r + 256, _C_OG_END:_C_OG_END + 768] = w_ref[r:r + 256, _C_LR_END:IN_W].astype(BF16)
            lrw = w_ref[r:r + 256, _C_OG_END:_C_LR_END].astype(BF16)
            wsc[r:r + 256, _C_OG_END + 768:_C_OG_END + 896] = jnp.concatenate(
                [lrw, jnp.zeros((256, LANES - GLA_LR_W), BF16)], axis=1)

    m = mod_ref[0]
    h = _rms(x) * n1_ref[...] * (1.0 + m[:, D_MODEL:2 * D_MODEL]) + m[:, 0:D_MODEL]
    hb = h.astype(BF16)
    qk_ref[...] = _dot(hb, wsc[:, 0:512])
    v_ref[...] = _dot(hb, wsc[:, 512:1024])
    og_ref[...] = _dot(hb, wsc[:, 1024:1536])
    uf_ref[...] = _dot(hb, wsc[:, 1536:1792])
    ux_ref[...] = _dot(hb, wsc[:, 1792:2048])
    ug_ref[...] = _dot(hb, wsc[:, 2048:2304])
    lr_ref[...] = _dot(hb, wsc[:, 2304:2432])


def _inproj(mode, x, extra, mod_l, norm1_l, w_in, l, B, L):
    T = B * L
    tm = TOKEN_TILE
    tpb = L // tm
    per_batch = mod_l.shape[0] > 1
    bidx = (lambda i: (i // tpb, 0, 0)) if per_batch else (lambda i: (0, 0, 0))
    row = lambda i: (i, 0)
    in_specs = [pl.BlockSpec((tm, D_MODEL), row)]
    args = [x]
    if mode == "pos":
        in_specs.append(pl.BlockSpec((tm, D_MODEL), lambda i: (i % tpb, 0)))
        args.append(extra)
    in_specs += [pl.BlockSpec((1, 1, N_MOD * D_MODEL), bidx),
                 pl.BlockSpec((1, D_MODEL), lambda i: (0, 0)),
                 pl.BlockSpec((None, D_MODEL, IN_W), lambda i: (l, 0, 0))]
    args += [mod_l, norm1_l.reshape(1, D_MODEL), w_in]
    widths = [512, 512, 512, 256, 256, 256, LANES]
    out_specs = [pl.BlockSpec((tm, w), row) for w in widths]
    out_shape = [jax.ShapeDtypeStruct((T, w), F32) for w in widths]
    if mode != "plain":
        out_specs = [pl.BlockSpec((tm, D_MODEL), row)] + out_specs
        out_shape = [jax.ShapeDtypeStruct((T, D_MODEL), F32)] + out_shape
    res = pl.pallas_call(
        functools.partial(_inproj_kernel, mode=mode),
        grid=(T // tm,),
        in_specs=in_specs,
        out_specs=out_specs,
        out_shape=out_shape,
        scratch_shapes=[pltpu.VMEM((D_MODEL, 2432), BF16)],
        compiler_params=_cparams("arbitrary"),
        name="inproj_" + mode,
    )(*args)
    if mode == "plain":
        return (x,) + tuple(res)
    return tuple(res)


def _gla_kernel(*refs, L, has_s0, want_state):
    qk_ref, v_ref, og_ref, lr_ref, wdec_ref, bdec_ref, gn_ref = refs[:7]
    p = 7
    s0_ref = None
    if has_s0:
        s0_ref = refs[p]
        p += 1
    o_ref = refs[p]
    p += 1
    sn_ref = None
    if want_state:
        sn_ref = refs[p]
        p += 1
    g_scr, oacc, s_scr = refs[p:p + 3]

    ch = GLA_CH
    n_chunks = L // ch
    kw = GLA_QK_W
    vw = GLA_V_W

    z16 = jnp.zeros((GLA_RANK, kw), F32)
    wc = jnp.concatenate([
        jnp.concatenate([wdec_ref[0], z16], axis=1),
        jnp.concatenate([z16, wdec_ref[1]], axis=1),
        jnp.zeros((LANES - GLA_LR_W, 2 * kw), F32)], axis=0).astype(BF16)
    bias = jnp.concatenate([bdec_ref[0], bdec_ref[1]], axis=1)
    z = _dot(lr_ref[0].astype(BF16), wc) + bias
    g_scr[...] = (jnp.minimum(z, 0.0) - jnp.log1p(jnp.exp(-jnp.abs(z)))) * (1.0 / GLA_GATE_NORM)

    row = lax.broadcasted_iota(I32, (ch, ch), 0)
    col = lax.broadcasted_iota(I32, (ch, ch), 1)
    lane_head = lax.broadcasted_iota(I32, (1, kw), 1) // GLA_DK
    blockdiag = (lax.broadcasted_iota(I32, (kw, vw), 0) // GLA_DK) == (lax.broadcasted_iota(I32, (kw, vw), 1) // GLA_DV)
    ones_t = jnp.ones((ch, LANES), BF16)
    gn = gn_ref[...]

    def finish(o, r0):
        parts = []
        for h in range(GLA_HEADS):
            parts.append(_rms(o[:, h * GLA_DV:(h + 1) * GLA_DV]) * gn)
        ogv = og_ref[0, r0:r0 + ch, :]
        return (jnp.concatenate(parts, axis=1) * (ogv * _sigmoid(ogv))).astype(BF16)

    for d in range(2):
        allowed = (col <= row) if d == 0 else (col >= row)
        tri = jnp.where(allowed, 1.0, 0.0).astype(BF16)
        if has_s0:
            s_scr[...] = jnp.zeros((kw, vw), F32)
            for h in range(GLA_HEADS):
                s_scr[h * GLA_DK:(h + 1) * GLA_DK, h * GLA_DV:(h + 1) * GLA_DV] = s0_ref[0, d, h]
        for i in range(n_chunks):
            n = i if d == 0 else n_chunks - 1 - i
            r0 = n * ch
            state_is_zero = (i == 0) and not has_s0
            gch = g_scr[r0:r0 + ch, d * kw:(d + 1) * kw]
            g_hi, g_lo = _split(gch)
            b = _dot(tri, g_hi) + _dot(tri, g_lo)
            b_last = b[ch - 1:ch, :] if d == 0 else b[0:1, :]
            bc = b - b[ch // 2:ch // 2 + 1, :]
            qch = qk_ref[0, r0:r0 + ch, 0:kw] * (GLA_DK ** -0.5)
            kch = qk_ref[0, r0:r0 + ch, kw:2 * kw]
            vb = v_ref[0, r0:r0 + ch, :].astype(BF16)
            q_s = (qch * jnp.exp(bc)).astype(BF16)
            k_s = (kch * jnp.exp(-bc)).astype(BF16)
            zero_q = jnp.zeros_like(q_s)
            qbig = jnp.concatenate([jnp.where(lane_head == h, q_s, zero_q) for h in range(GLA_HEADS)], axis=0)
            scores = _dot_nt(qbig, k_s)
            parts = []
            for h in range(GLA_HEADS):
                ph = jnp.where(allowed, scores[h * ch:(h + 1) * ch, :], 0.0).astype(BF16)
                parts.append(_dot(ph, vb[:, h * GLA_DV:(h + 1) * GLA_DV]))
            o = jnp.concatenate(parts, axis=1)
            if not state_is_zero:
                q_t = (qch * jnp.exp(b)).astype(BF16)
                o = o + _dot(q_t, s_scr[...].astype(BF16))
            if (i < n_chunks - 1) or want_state:
                k_d = (kch * jnp.exp(b_last - b)).astype(BF16)
                ds = jnp.where(blockdiag, _dot_tn(k_d, vb), 0.0)
                if state_is_zero:
                    s_scr[...] = ds
                else:
                    dcol = _dot_tn(g_hi, ones_t) + _dot_tn(g_lo, ones_t)
                    dec = jnp.exp(dcol)
                    s_scr[...] = s_scr[...] * jnp.concatenate([dec] * (vw // LANES), axis=1) + ds
            if d == 0:
                oacc[r0:r0 + ch, :] = o
            else:
                o_ref[0, r0:r0 + ch, :] = finish(o + oacc[r0:r0 + ch, :], r0)
        if want_state:
            for h in range(GLA_HEADS):
                sn_ref[0, d, h] = s_scr[h * GLA_DK:(h + 1) * GLA_DK, h * GLA_DV:(h + 1) * GLA_DV]


def _gla(qk, v, og, lr, wdec_l, bdec_l, gn_l, s0, B, L, want_state):
    has_s0 = s0 is not None
    blk = lambda w: pl.BlockSpec((1, L, w), lambda b: (b, 0, 0))
    in_specs = [blk(512), blk(512), blk(512), blk(LANES),
                pl.BlockSpec((2, GLA_RANK, GLA_QK_W), lambda b: (0, 0, 0)),
                pl.BlockSpec((2, 1, GLA_QK_W), lambda b: (0, 0, 0)),
                pl.BlockSpec((1, GLA_DV), lambda b: (0, 0))]
    args = [qk.reshape(B, L, 512), v.reshape(B, L, 512), og.reshape(B, L, 512), lr.reshape(B, L, LANES),
            wdec_l, bdec_l.reshape(2, 1, GLA_QK_W), gn_l.reshape(1, GLA_DV)]
    st_spec = pl.BlockSpec((1, 2, GLA_HEADS, GLA_DK, GLA_DV), lambda b: (b, 0, 0, 0, 0))
    if has_s0:
        in_specs.append(st_spec)
        args.append(s0)
    out_specs = [pl.BlockSpec((1, L, GLA_V_W), lambda b: (b, 0, 0))]
    out_shape = [jax.ShapeDtypeStruct((B, L, GLA_V_W), BF16)]
    if want_state:
        out_specs.append(st_spec)
        out_shape.append(jax.ShapeDtypeStruct((B, 2, GLA_HEADS, GLA_DK, GLA_DV), F32))
    res = pl.pallas_call(
        functools.partial(_gla_kernel, L=L, has_s0=has_s0, want_state=want_state),
        grid=(B,),
        in_specs=in_specs,
        out_specs=out_specs,
        out_shape=out_shape,
        scratch_shapes=[pltpu.VMEM((L, 2 * GLA_QK_W), F32),
                        pltpu.VMEM((L, GLA_V_W), F32),
                        pltpu.VMEM((GLA_QK_W, GLA_V_W), F32)],
        compiler_params=_cparams("arbitrary"),
        name="gla",
    )(*args)
    o = res[0].reshape(B * L, GLA_V_W)
    return o, (res[1] if want_state else None)


def _fft_tables(L):
    m = np.arange(L, dtype=np.int64)
    ang = 2.0 * np.pi * ((m[:, None] * m[None, :]) % L) / L
    cc = np.concatenate([np.cos(ang), -np.sin(ang)], axis=1)
    c = np.arange(FOURIER_GW, dtype=np.int64)
    angc = 2.0 * np.pi * ((c[:, None] * c[None, :]) % FOURIER_GW) / FOURIER_GW
    scale = 1.0 / math.sqrt(L * FOURIER_GW)
    eye = np.eye(FOURIER_GROUPS)
    bdc = np.kron(eye, np.cos(angc) * scale)
    bds = np.kron(eye, np.sin(angc) * scale)
    return (jnp.asarray(cc, dtype=F32), jnp.asarray(bdc, dtype=F32), jnp.asarray(bds, dtype=F32))


def _fft_kernel(u_ref, cc_ref, bdc_ref, bds_ref, o_ref):
    u_hi, u_lo = _split(u_ref[0])
    bdc = bdc_ref[...].astype(BF16)
    bds = bds_ref[...].astype(BF16)
    uc = _dot(u_hi, bdc) + _dot(u_lo, bdc)
    us = _dot(u_hi, bds) + _dot(u_lo, bds)
    w_hi, w_lo = _split(jnp.concatenate([uc, us], axis=0))
    cc = cc_ref[...].astype(BF16)
    o_ref[0] = (_dot(cc, w_hi) + _dot(cc, w_lo)).astype(BF16)


def _fft(uf, B, L):
    cc, bdc, bds = _fft_tables(L)
    res = pl.pallas_call(
        _fft_kernel,
        grid=(B,),
        in_specs=[pl.BlockSpec((1, L, FOURIER_W), lambda b: (b, 0, 0)),
                  pl.BlockSpec((L, 2 * L), lambda b: (0, 0)),
                  pl.BlockSpec((FOURIER_W, FOURIER_W), lambda b: (0, 0)),
                  pl.BlockSpec((FOURIER_W, FOURIER_W), lambda b: (0, 0))],
        out_specs=pl.BlockSpec((1, L, FOURIER_W), lambda b: (b, 0, 0)),
        out_shape=jax.ShapeDtypeStruct((B, L, FOURIER_W), BF16),
        compiler_params=_cparams("arbitrary"),
        name="fourier",
    )(uf.reshape(B, L, FOURIER_W), cc, bdc, bds)
    return res.reshape(B * L, FOURIER_W)


def _lru_kernel(*refs, L, has_s0, want_state):
    ux_ref, ug_ref, cw_ref, cb_ref, wa_ref, ba_ref, wx_ref, bx_ref, lam_ref = refs[:9]
    p = 9
    s0_ref = None
    if has_s0:
        s0_ref = refs[p]
        p += 1
    o_ref = refs[p]
    p += 1
    sn_ref = None
    if want_state:
        sn_ref = refs[p]
        p += 1
    bd_scr = refs[p]
    scan_scr = refs[p + 1:p + 9]

    @pl.when(pl.program_id(0) == 0)
    def _():
        r = lax.broadcasted_iota(I32, (LRU_BW, LRU_W), 0)
        c = lax.broadcasted_iota(I32, (LRU_BW, LRU_W), 1)
        for d in range(2):
            for gi, w_ref in enumerate((wa_ref, wx_ref)):
                pieces = []
                for h in range(LRU_BLOCKS):
                    place = jnp.where(c == r + h * LRU_BW, 1.0, 0.0).astype(BF16)
                    pieces.append(_dot(w_ref[d, h].astype(BF16), place))
                bd_scr[2 * d + gi] = jnp.concatenate(pieces, axis=0).astype(BF16)

    t = lax.broadcasted_iota(I32, (L, 1), 0)
    x = ux_ref[0]
    xm2 = jnp.where(t >= 2, pltpu.roll(x, 2, 0), 0.0)
    xm1 = jnp.where(t >= 1, pltpu.roll(x, 1, 0), 0.0)
    xp1 = jnp.where(t <= L - 2, pltpu.roll(x, L - 1, 0), 0.0)
    xc = xm2 * cw_ref[0:1, :] + xm1 * cw_ref[1:2, :] + x * cw_ref[2:3, :] + xp1 * cw_ref[3:4, :] + cb_ref[...]
    xcb = xc.astype(BF16)

    nb = L // SUBLANES
    pitch = nb + SUBLANES
    n_slab = LRU_W // LANES
    sub = lax.broadcasted_iota(I32, (SUBLANES, LANES), 0)
    hsum = [[None] * n_slab for _ in range(SUBLANES)]
    for d in range(2):
        r = 0.5 + 0.5 * jnp.tanh(0.5 * (_dot(xcb, bd_scr[2 * d]) + ba_ref[d]))
        ig = 0.5 + 0.5 * jnp.tanh(0.5 * (_dot(xcb, bd_scr[2 * d + 1]) + bx_ref[d]))
        lam = lam_ref[d]
        softplus = jnp.maximum(-lam, 0.0) + jnp.log1p(jnp.exp(-jnp.abs(lam)))
        a = jnp.exp(-LRU_C * r * softplus)
        u = jnp.sqrt(1.0 - a * a) * (ig * xc)
        a_scr, u_scr, h_scr, p_scr = scan_scr[4 * d:4 * d + 4]
        for s in range(SUBLANES):
            for k in range(n_slab):
                a_scr[k, s * pitch:s * pitch + nb, :] = a[s * nb:(s + 1) * nb, k * LANES:(k + 1) * LANES]
                u_scr[k, s * pitch:s * pitch + nb, :] = u[s * nb:(s + 1) * nb, k * LANES:(k + 1) * LANES]
        steps = range(nb) if d == 0 else range(nb - 1, -1, -1)
        for k in range(n_slab):
            h = jnp.zeros((SUBLANES, LANES), F32)
            prod = jnp.ones((SUBLANES, LANES), F32)
            for i in steps:
                rows = pl.ds(i, SUBLANES, stride=pitch)
                ai = a_scr[k, rows, :]
                h = ai * h + u_scr[k, rows, :]
                prod = ai * prod
                h_scr[k, rows, :] = h
                p_scr[k, rows, :] = prod
            if has_s0:
                h0 = jnp.broadcast_to(s0_ref[0, d:d + 1, k * LANES:(k + 1) * LANES], (SUBLANES, LANES))
            else:
                h0 = jnp.zeros((SUBLANES, LANES), F32)
            first = 0 if d == 0 else SUBLANES - 1
            carry = jnp.where(sub == first, h0, 0.0)
            for j in range(1, SUBLANES):
                s = j if d == 0 else SUBLANES - 1 - j
                moved = pltpu.roll(prod * carry + h, 1 if d == 0 else SUBLANES - 1, 0)
                carry = jnp.where(sub == s, moved, carry)
            if want_state:
                last = SUBLANES - 1 - first
                sn_ref[0, d:d + 1, k * LANES:(k + 1) * LANES] = (prod * carry + h)[last:last + 1, :]
            for s in range(SUBLANES):
                blk = h_scr[k, s * pitch:s * pitch + nb, :] + p_scr[k, s * pitch:s * pitch + nb, :] * carry[s:s + 1, :]
                hsum[s][k] = blk if hsum[s][k] is None else hsum[s][k] + blk

    for s in range(SUBLANES):
        ugv = ug_ref[0, s * nb:(s + 1) * nb, :]
        gelu = 0.5 * ugv * (1.0 + jnp.tanh(math.sqrt(2.0 / math.pi) * (ugv + 0.044715 * (ugv * ugv * ugv))))
        o_ref[0, s * nb:(s + 1) * nb, :] = (jnp.concatenate(hsum[s], axis=1) * gelu).astype(BF16)


def _lru(ux, ug, cw_l, cb_l, wa_l, ba_l, wx_l, bx_l, lam_l, s0, B, L, want_state):
    has_s0 = s0 is not None
    blk = pl.BlockSpec((1, L, LRU_W), lambda b: (b, 0, 0))
    vec2 = pl.BlockSpec((2, 1, LRU_W), lambda b: (0, 0, 0))
    wsp = pl.BlockSpec((2, LRU_BLOCKS, LRU_BW, LRU_BW), lambda b: (0, 0, 0, 0))
    in_specs = [blk, blk,
                pl.BlockSpec((CONV_W, LRU_W), lambda b: (0, 0)),
                pl.BlockSpec((1, LRU_W), lambda b: (0, 0)),
                wsp, vec2, wsp, vec2, vec2]
    args = [ux.reshape(B, L, LRU_W), ug.reshape(B, L, LRU_W), cw_l, cb_l.reshape(1, LRU_W),
            wa_l, ba_l.reshape(2, 1, LRU_W), wx_l, bx_l.reshape(2, 1, LRU_W), lam_l.reshape(2, 1, LRU_W)]
    st_spec = pl.BlockSpec((1, 2, LRU_W), lambda b: (b, 0, 0))
    if has_s0:
        in_specs.append(st_spec)
        args.append(s0)
    out_specs = [blk]
    out_shape = [jax.ShapeDtypeStruct((B, L, LRU_W), BF16)]
    if want_state:
        out_specs.append(st_spec)
        out_shape.append(jax.ShapeDtypeStruct((B, 2, LRU_W), F32))
    res = pl.pallas_call(
        functools.partial(_lru_kernel, L=L, has_s0=has_s0, want_state=want_state),
        grid=(B,),
        in_specs=in_specs,
        out_specs=out_specs,
        out_shape=out_shape,
        scratch_shapes=[pltpu.VMEM((4, LRU_W, LRU_W), BF16)]
        + [pltpu.VMEM((LRU_W // LANES, L + SUBLANES * SUBLANES, LANES), F32)] * 8,
        compiler_params=_cparams("arbitrary"),
        name="rglru",
    )(*args)
    return res[0].reshape(B * L, LRU_W), (res[1] if want_state else None)


def _outproj_kernel(og_ref, of_ref, ol_ref, x_ref, mod_ref, n2_ref, wout_ref, wr_ref,
                    x1_ref, hx_ref, afft_ref, wsc, wrs):
    @pl.when(pl.program_id(0) == 0)
    def _():
        for r in range(0, MIX_W, 256):
            wsc[r:r + 256, :] = wout_ref[r:r + 256, :].astype(BF16)
        wrs[...] = jnp.concatenate([wr_ref[...], jnp.zeros((D_MODEL, LANES - N_EXPERTS), F32)], axis=1)

    m = mod_ref[0]
    y = (_dot(og_ref[...], wsc[0:GLA_V_W, :])
         + _dot(of_ref[...], wsc[GLA_V_W:GLA_V_W + FOURIER_W, :])
         + _dot(ol_ref[...], wsc[GLA_V_W + FOURIER_W:MIX_W, :]))
    x1 = x_ref[...] + m[:, 2 * D_MODEL:3 * D_MODEL] * y
    x1_ref[...] = x1
    h2 = _rms(x1) * n2_ref[...] * (1.0 + m[:, 4 * D_MODEL:5 * D_MODEL]) + m[:, 3 * D_MODEL:4 * D_MODEL]
    h_hi = h2.astype(BF16)
    h_hi32 = h_hi.astype(F32)
    tm = h2.shape[0]
    for s in range(ROW_SUB):
        hx_ref[pl.ds(s, tm, stride=ROW_SUB), :] = h_hi32[:, s * LANES:(s + 1) * LANES]
    h_lo = (h2 - h_hi32).astype(BF16)
    w_hi, w_lo = _split(wrs[...])
    logits = _dot(h_hi, w_hi) + _dot(h_lo, w_hi) + _dot(h_hi, w_lo)
    lane = lax.broadcasted_iota(I32, logits.shape, 1)
    logits = jnp.where(lane < N_EXPERTS, logits, -jnp.inf)
    ex = jnp.exp(logits - jnp.max(logits, axis=-1, keepdims=True))
    aff = ex / jnp.sum(ex, axis=-1, keepdims=True)
    afft_ref[...] = aff.T[0:N_EXPERTS, :]


def _outproj(o_gla, o_fft, o_lru, x, mod_l, norm2_l, w_out, w_router_l, l, B, L):
    T = B * L
    tm = OUT_TILE
    tpb = L // tm
    per_batch = mod_l.shape[0] > 1
    bidx = (lambda i: (i // tpb, 0, 0)) if per_batch else (lambda i: (0, 0, 0))
    row = lambda i: (i, 0)
    return pl.pallas_call(
        _outproj_kernel,
        grid=(T // tm,),
        in_specs=[pl.BlockSpec((tm, GLA_V_W), row), pl.BlockSpec((tm, FOURIER_W), row),
                  pl.BlockSpec((tm, LRU_W), row), pl.BlockSpec((tm, D_MODEL), row),
                  pl.BlockSpec((1, 1, N_MOD * D_MODEL), bidx),
                  pl.BlockSpec((1, D_MODEL), lambda i: (0, 0)),
                  pl.BlockSpec((None, MIX_W, D_MODEL), lambda i: (l, 0, 0)),
                  pl.BlockSpec((D_MODEL, N_EXPERTS), lambda i: (0, 0))],
        out_specs=[pl.BlockSpec((tm, D_MODEL), row),
                   pl.BlockSpec((tm * ROW_SUB, LANES), lambda i: (i, 0)),
                   pl.BlockSpec((N_EXPERTS, tm), lambda i: (0, i))],
        out_shape=[jax.ShapeDtypeStruct((T, D_MODEL), F32),
                   jax.ShapeDtypeStruct((T * ROW_SUB, LANES), F32),
                   jax.ShapeDtypeStruct((N_EXPERTS, T), F32)],
        scratch_shapes=[pltpu.VMEM((MIX_W, D_MODEL), BF16), pltpu.VMEM((D_MODEL, LANES), F32)],
        compiler_params=_cparams("arbitrary"),
        name="outproj",
    )(o_gla, o_fft, o_lru, x, mod_l, norm2_l.reshape(1, D_MODEL), w_out, w_router_l)


def _prefix_lanes(x):
    T = x.shape[1]
    w = 256
    nb = T // w
    stacked = jnp.concatenate([x[:, j * w:(j + 1) * w] for j in range(nb)], axis=0)
    upper = jnp.where(lax.broadcasted_iota(I32, (w, w), 0) <= lax.broadcasted_iota(I32, (w, w), 1), 1.0, 0.0)
    pe = _dot(stacked, upper.astype(BF16))
    carry = jnp.zeros((N_EXPERTS, 1), F32)
    outs = []
    for j in range(nb):
        blk = pe[j * N_EXPERTS:(j + 1) * N_EXPERTS, :]
        outs.append(blk + carry)
        carry = carry + blk[:, w - 1:w]
    return jnp.concatenate(outs, axis=1)


def _topk_kernel(aff_ref, out_ref, gate_ref, slot_ref, bnd_ref, *, T, C):
    n_a = C // DIGIT
    aff = aff_ref[...]
    bits = jnp.zeros((N_EXPERTS, 1), I32)
    for bit in range(30, -1, -1):
        cand = bits | (1 << bit)
        cnt = jnp.sum(jnp.where(aff >= pltpu.bitcast(cand, F32), 1.0, 0.0), axis=1, keepdims=True)
        bits = jnp.where(cnt >= C, cand, bits)
    thr = pltpu.bitcast(bits, F32)
    gt = aff > thr
    eq = aff == thr
    eqf = jnp.where(eq, 1.0, 0.0)
    need = C - jnp.sum(jnp.where(gt, 1.0, 0.0), axis=1, keepdims=True)
    eq_before = _prefix_lanes(eqf.astype(BF16)) - eqf
    sel = gt | (eq & (eq_before < need))
    self32 = jnp.where(sel, 1.0, 0.0)
    cnt = _prefix_lanes(self32.astype(BF16))
    slot_ref[...] = jnp.where(sel, cnt - 1.0, -1.0)
    tok = lax.broadcasted_iota(I32, (1, T), 1)
    lane = lax.broadcasted_iota(I32, (1, LANES), 1)
    bnd = jnp.zeros((N_EXPERTS, LANES), F32)
    for j in range(1, T // COMBINE_TILE + 1):
        before = jnp.sum(jnp.where(tok < j * COMBINE_TILE, self32, 0.0), axis=1, keepdims=True)
        bnd = jnp.where(lane == j, before, bnd)
    bnd_ref[...] = bnd

    p_dig = jnp.floor(cnt * (1.0 / DIGIT))
    q_dig = cnt - DIGIT * p_dig
    a_col = lax.broadcasted_iota(I32, (n_a, 1), 0).astype(F32)
    b_col = lax.broadcasted_iota(I32, (DIGIT, 1), 0).astype(F32)
    slot = cnt - 1.0
    ps_dig = jnp.where(sel, jnp.floor(slot * (1.0 / DIGIT)), -1.0)
    qs_dig = slot - DIGIT * jnp.floor(slot * (1.0 / DIGIT))
    aff_hi = aff.astype(BF16).astype(F32)
    aff_lo = aff - aff_hi
    kc = min(T, 2048)
    acc = jnp.zeros((N_EXPERTS * n_a, N_EXPERTS * DIGIT), F32)
    gacc = jnp.zeros((N_EXPERTS * n_a, N_EXPERTS * DIGIT), F32)
    for c0 in range(0, T, kc):
        tk = slice(c0, c0 + kc)
        u = jnp.concatenate([jnp.where(p_dig[e:e + 1, tk] == a_col, 1.0, 0.0).astype(BF16)
                             for e in range(N_EXPERTS)], axis=0)
        v = jnp.concatenate([jnp.where(q_dig[e:e + 1, tk] <= b_col, 1.0, 0.0).astype(BF16)
                             for e in range(N_EXPERTS)], axis=0)
        acc = acc + _dot_nt(u, v)
        us = jnp.concatenate([jnp.where(ps_dig[e:e + 1, tk] == a_col, 1.0, 0.0).astype(BF16)
                              for e in range(N_EXPERTS)], axis=0)
        for part in (aff_hi, aff_lo):
            vs = jnp.concatenate([jnp.where(qs_dig[e:e + 1, tk] == b_col, part[e:e + 1, tk], 0.0).astype(BF16)
                                  for e in range(N_EXPERTS)], axis=0)
            gacc = gacc + _dot_nt(us, vs)
    below = jnp.concatenate([jnp.sum(jnp.where(p_dig[e:e + 1, :] < a_col, 1.0, 0.0), axis=1, keepdims=True)
                             for e in range(N_EXPERTS)], axis=0)
    r_i = lax.broadcasted_iota(I32, acc.shape, 0) // n_a
    c_i = lax.broadcasted_iota(I32, acc.shape, 1) // DIGIT

    def own_block(x):
        x = jnp.where(r_i == c_i, x, 0.0)
        x = x[:, 0:256] + x[:, 256:512]
        x = x[:, 0:LANES] + x[:, LANES:2 * LANES]
        x = x + pltpu.roll(x, 64, 1)
        return x + pltpu.roll(x, 32, 1)

    out_ref[...] = own_block(acc) + below
    gate_ref[...] = own_block(gacc)


def _expert_choice(aff_t, T, C):
    n_a = C // DIGIT
    res = pl.pallas_call(
        functools.partial(_topk_kernel, T=T, C=C),
        grid=(1,),
        in_specs=[pl.BlockSpec((N_EXPERTS, T), lambda i: (0, 0))],
        out_specs=[pl.BlockSpec((N_EXPERTS * n_a, LANES), lambda i: (0, 0)),
                   pl.BlockSpec((N_EXPERTS * n_a, LANES), lambda i: (0, 0)),
                   pl.BlockSpec((N_EXPERTS, T), lambda i: (0, 0)),
                   pl.BlockSpec((N_EXPERTS, LANES), lambda i: (0, 0))],
        out_shape=[jax.ShapeDtypeStruct((N_EXPERTS * n_a, LANES), F32),
                   jax.ShapeDtypeStruct((N_EXPERTS * n_a, LANES), F32),
                   jax.ShapeDtypeStruct((N_EXPERTS, T), F32),
                   jax.ShapeDtypeStruct((N_EXPERTS, LANES), F32)],
        compiler_params=_cparams("arbitrary"),
        name="expert_choice",
    )(aff_t)
    idx = res[0][:, 0:DIGIT].astype(I32).reshape(N_EXPERTS, C)
    gate = res[1][:, 0:DIGIT].reshape(N_EXPERTS, 1, C)
    bnd = res[3][:, 0:T // COMBINE_TILE + 1].astype(I32)
    return idx, gate, res[2], bnd


def _ffn_kernel(*refs, caps):
    n = len(caps)
    idx_refs, hx_refs, gate_refs = refs[0:n], refs[n:2 * n], refs[2 * n:3 * n]
    wg_ref, wu_ref, wd_ref = refs[3 * n:3 * n + 3]
    y_refs = refs[3 * n + 3:4 * n + 3]
    xbufs = refs[4 * n + 3:5 * n + 3]
    wgb, wub, wdb, sems = refs[5 * n + 3:5 * n + 7]
    e = pl.program_id(0)
    last = pl.num_programs(0) - 1
    buf = e % 2
    nxt = 1 - buf
    following = jnp.where(e == last, 0, e + 1)

    def fetch_row(q, expert, c, into, k):
        src = hx_refs[q].at[pl.ds(pl.multiple_of(idx_refs[q][expert, c] * ROW_SUB, ROW_SUB), ROW_SUB)]
        dst = xbufs[q].at[into, pl.ds(pl.multiple_of(c * ROW_SUB, ROW_SUB), ROW_SUB)]
        pltpu.make_async_copy(src, dst, sems.at[2 * q + into]).start(priority=k % 2)

    def wait_rows(q, which):
        pltpu.make_async_copy(hx_refs[q].at[pl.ds(0, caps[q] * ROW_SUB)], xbufs[q].at[which],
                              sems.at[2 * q + which]).wait()

    @pl.when(e == 0)
    def _():
        for q in range(n):
            def body(i, carry, q=q):
                for k in range(GATHER_UNROLL):
                    fetch_row(q, 0, i * GATHER_UNROLL + k, 0, k)
                return carry
            lax.fori_loop(0, caps[q] // GATHER_UNROLL, body, 0)

    for r in range(0, D_MODEL, 256):
        wgb[r:r + 256, :] = wg_ref[r:r + 256, :].astype(BF16)
        wub[r:r + 256, :] = wu_ref[r:r + 256, :].astype(BF16)
        wdb[r:r + 256, :] = wd_ref[r:r + 256, :].astype(BF16)

    for q in range(n):
        wait_rows(q, buf)
        step = min(MOE_ROWS, caps[q])
        for r0 in range(0, caps[q], step):
            for k in range(step):
                fetch_row(q, following, r0 + k, nxt, k)
            x = jnp.concatenate([xbufs[q][buf, pl.ds(r0 * ROW_SUB + s, step, stride=ROW_SUB), :]
                                 for s in range(ROW_SUB)], axis=1).astype(BF16)
            g = _dot(x, wgb[...])
            u = _dot(x, wub[...])
            hid = (g * _sigmoid(g) * u).astype(BF16)
            gate = jnp.broadcast_to(gate_refs[q][:, r0:r0 + step], (LANES, step)).T[:, 0:1]
            y_refs[q][pl.ds(r0, step), :] = (_dot(hid, wdb[...]) * gate).astype(BF16)

    @pl.when(e == last)
    def _():
        for q in range(n):
            wait_rows(q, nxt)


def _expert_ffn(sets, wg, wu, wd, l):
    n = len(sets)
    caps = tuple(s[0].shape[1] for s in sets)
    imap = lambda e, *idx_refs: (l, e, 0, 0)
    wspec = pl.BlockSpec((None, None, D_MODEL, EXPERT_FF), imap)
    emap = lambda e, *idx_refs: (e, 0, 0)
    return pl.pallas_call(
        functools.partial(_ffn_kernel, caps=caps),
        grid_spec=pltpu.PrefetchScalarGridSpec(
            num_scalar_prefetch=n,
            grid=(N_EXPERTS,),
            in_specs=[pl.BlockSpec(memory_space=pl.ANY)] * n
            + [pl.BlockSpec((None, 1, c), emap) for c in caps]
            + [wspec, wspec, pl.BlockSpec((None, None, EXPERT_FF, D_MODEL), imap)],
            out_specs=[pl.BlockSpec((None, c, D_MODEL), emap) for c in caps],
            scratch_shapes=[pltpu.VMEM((2, c * ROW_SUB, LANES), F32) for c in caps]
            + [pltpu.VMEM((D_MODEL, EXPERT_FF), BF16),
               pltpu.VMEM((D_MODEL, EXPERT_FF), BF16),
               pltpu.VMEM((EXPERT_FF, D_MODEL), BF16),
               pltpu.SemaphoreType.DMA((2 * n,))]),
        out_shape=[jax.ShapeDtypeStruct((N_EXPERTS, c, D_MODEL), BF16) for c in caps],
        compiler_params=pltpu.CompilerParams(dimension_semantics=("arbitrary",), vmem_limit_bytes=VMEM_LIMIT,
                                             disable_bounds_checks=True),
        name="expert_ffn",
    )(*[s[0] for s in sets], *[s[1] for s in sets], *[s[2] for s in sets], wg, wu, wd)


def _combine_kernel(bnd_ref, slot_ref, y_hbm, x1_ref, gmod_ref, fn_ref, o_ref, ybuf, onehot, acc, sems, *, C, final):
    j = pl.program_id(0)
    cur = j % 2
    ch = COMBINE_CHUNK
    group = COMBINE_DEPTH // ch
    tt = x1_ref.shape[0]
    w_col = lax.broadcasted_iota(I32, (ch, 1), 0)

    def chunks_of(tile, e):
        first = bnd_ref[e, tile] & (-BF16_ROWS)
        return first, lax.div(bnd_ref[e, tile + 1] - first + (ch - 1), jnp.int32(ch))

    def padded(total):
        return lax.div(total + (group - 1), jnp.int32(group)) * group

    def stage(tile, into):
        tok0 = pl.multiple_of(tile * tt, tt)
        total = jnp.int32(0)
        for e in range(N_EXPERTS):
            first, n_chunks = chunks_of(tile, e)

            def fetch_chunk(c, carry, e=e, first=first, base=total):
                want_lo = first + c * ch
                src_row = pl.multiple_of(jnp.minimum(want_lo, C - ch), BF16_ROWS)
                dst_row = pl.multiple_of((base + c) * ch, ch)
                pltpu.make_async_copy(y_hbm.at[e, pl.ds(src_row, ch)], ybuf.at[into, pl.ds(dst_row, ch)],
                                      sems.at[into]).start()
                row = src_row + w_col
                want = jnp.where(row >= want_lo, row, -2).astype(F32)
                hit = slot_ref[e:e + 1, pl.ds(tok0, tt)] == want
                onehot[into, pl.ds(dst_row, ch), :] = jnp.where(hit, 1.0, 0.0).astype(BF16)
                return carry

            lax.fori_loop(0, n_chunks, fetch_chunk, 0)
            total = total + n_chunks

        def clear_chunk(k, carry):
            rows = pl.ds(pl.multiple_of(k * ch, ch), ch)
            onehot[into, rows, :] = jnp.zeros((ch, tt), BF16)
            ybuf[into, rows, :] = jnp.zeros((ch, D_MODEL), BF16)
            return carry

        lax.fori_loop(total, padded(total), clear_chunk, 0)

    @pl.when(j == 0)
    def _():
        stage(0, 0)

    @pl.when(j + 1 < pl.num_programs(0))
    def _():
        stage(j + 1, 1 - cur)

    total = jnp.int32(0)
    for e in range(N_EXPERTS):
        total = total + chunks_of(j, e)[1]

    def wait_chunk(k, carry):
        pltpu.make_async_copy(y_hbm.at[0, pl.ds(0, ch)], ybuf.at[cur, pl.ds(0, ch)], sems.at[cur]).wait()
        return carry

    lax.fori_loop(0, total, wait_chunk, 0)
    acc[...] = jnp.zeros(acc.shape, F32)

    def add_group(g, carry):
        rows = pl.ds(pl.multiple_of(g * COMBINE_DEPTH, COMBINE_DEPTH), COMBINE_DEPTH)
        acc[...] += _dot_tn(onehot[cur, rows, :], ybuf[cur, rows, :])
        return carry

    lax.fori_loop(0, lax.div(padded(total), jnp.int32(group)), add_group, 0)
    x = x1_ref[...] + gmod_ref[0][:, 5 * D_MODEL:6 * D_MODEL] * acc[...]
    o_ref[...] = _rms(x) * fn_ref[...] if final else x


def _combine(bnd, slot, y, x1, gmod, final_norm, final, B, L, C):
    T = B * L
    tt = COMBINE_TILE
    tpb = L // tt if L >= tt else None
    per_batch = gmod.shape[0] > 1
    if per_batch:
        bidx = lambda j, b: (j // tpb, 0, 0)
    else:
        bidx = lambda j, b: (0, 0, 0)
    max_chunks = N_EXPERTS * (tt // COMBINE_CHUNK + 2)
    max_rows = pl.cdiv(max_chunks * COMBINE_CHUNK, COMBINE_DEPTH) * COMBINE_DEPTH
    return pl.pallas_call(
        functools.partial(_combine_kernel, C=C, final=final),
        grid_spec=pltpu.PrefetchScalarGridSpec(
            num_scalar_prefetch=1,
            grid=(T // tt,),
            in_specs=[pl.BlockSpec((N_EXPERTS, T), lambda j, b: (0, 0)),
                      pl.BlockSpec(memory_space=pl.ANY),
                      pl.BlockSpec((tt, D_MODEL), lambda j, b: (j, 0)),
                      pl.BlockSpec((1, 1, N_MOD * D_MODEL), bidx),
                      pl.BlockSpec((1, D_MODEL), lambda j, b: (0, 0))],
            out_specs=pl.BlockSpec((tt, D_MODEL), lambda j, b: (j, 0)),
            scratch_shapes=[pltpu.VMEM((2, max_rows, D_MODEL), BF16),
                            pltpu.VMEM((2, max_rows, tt), BF16),
                            pltpu.VMEM((tt, D_MODEL), F32),
                            pltpu.SemaphoreType.DMA((2,))]),
        out_shape=jax.ShapeDtypeStruct((T, D_MODEL), F32),
        compiler_params=_cparams("arbitrary"),
        name="combine_final" if final else "combine",
    )(bnd, slot, y, x1, gmod, final_norm.reshape(1, D_MODEL))


def _grid_position_embedding(n_tokens):
    rows = n_tokens // GRID_W
    r, col = jnp.meshgrid(jnp.arange(rows, dtype=F32), jnp.arange(GRID_W, dtype=F32), indexing="ij")
    n_freq = D_MODEL // 4
    omega = 1.0 / (POS_BASE ** (jnp.arange(n_freq, dtype=F32) / n_freq))
    ar = r.reshape(-1)[:, None] * omega
    ac = col.reshape(-1)[:, None] * omega
    return jnp.concatenate([jnp.sin(ar), jnp.cos(ar), jnp.sin(ac), jnp.cos(ac)], axis=-1)


def _mixers_and_routing(x, pos, mod_l, gla_s0, lru_s0, want_state, w, l, B, L):
    T = B * L
    C = CAPACITY_FACTOR * T // N_EXPERTS
    mode = "pos" if pos is not None else "plain"
    x, qk, v, og, uf, ux, ug, lr = _inproj(mode, x, pos, mod_l, w["norm1"][l], w["w_in"], l, B, L)
    o_gla, gs = _gla(qk, v, og, lr, w["gla_w_decay"][l], w["gla_b_decay"][l], w["gla_norm"][l],
                     None if gla_s0 is None else gla_s0[:, l], B, L, want_state)
    o_fft = _fft(uf, B, L)
    o_lru, ls = _lru(ux, ug, w["lru_conv_w"][l], w["lru_conv_b"][l], w["lru_wa"][l], w["lru_ba"][l],
                     w["lru_wx"][l], w["lru_bx"][l], w["lru_lambda"][l],
                     None if lru_s0 is None else lru_s0[:, l], B, L, want_state)
    x1, hx, aff_t = _outproj(o_gla, o_fft, o_lru, x, mod_l, w["norm2"][l], w["w_out"], w["w_router"][l], l, B, L)
    idx, gate, slot, bnd = _expert_choice(aff_t, T, C)
    return dict(x1=x1, sets=(idx, hx, gate), slot=slot, bnd=bnd, gla_state=gs, lru_state=ls, C=C)


def kernel(x_prompt, x_sample, state_gla, state_rglru, c, c_ctx, w_mod, b_mod, norm1, norm2, w_in, gla_w_decay, gla_b_decay, gla_norm, lru_conv_w, lru_conv_b, lru_wa, lru_ba, lru_wx, lru_bx, lru_lambda, w_out, w_router, w_expert_gate, w_expert_up, w_expert_down, final_norm):
    w = dict(norm1=norm1, norm2=norm2, w_in=w_in, gla_w_decay=gla_w_decay, gla_b_decay=gla_b_decay,
             gla_norm=gla_norm, lru_conv_w=lru_conv_w, lru_conv_b=lru_conv_b, lru_wa=lru_wa, lru_ba=lru_ba,
             lru_wx=lru_wx, lru_bx=lru_bx, lru_lambda=lru_lambda, w_out=w_out, w_router=w_router,
             w_expert_gate=w_expert_gate, w_expert_up=w_expert_up, w_expert_down=w_expert_down,
             final_norm=final_norm)
    n_lat = c.shape[0]
    cond = jnp.concatenate([c_ctx[None, :], c, jnp.zeros((SUBLANES - 1 - n_lat, D_MODEL), F32)], axis=0)
    mod = _modulation(cond, w_mod, b_mod)
    mod_ctx = mod[:, 0:1].reshape(DEPTH, 1, 1, N_MOD * D_MODEL)
    mod_lat = mod[:, 1:1 + n_lat].reshape(DEPTH, n_lat, 1, N_MOD * D_MODEL)

    bp, lp, _ = x_prompt.shape
    bs, ls_, _ = x_sample.shape
    xp = x_prompt.reshape(bp * lp, D_MODEL)
    xs = x_sample.reshape(bs * ls_, D_MODEL)
    pos = _grid_position_embedding(ls_)
    gla_states = []
    lru_states = []
    for l in range(DEPTH):
        final = l == DEPTH - 1
        gp = _mixers_and_routing(xp, None, mod_ctx[l], None, None, True, w, l, bp, lp)
        gs = _mixers_and_routing(xs, pos if l == 0 else None, mod_lat[l], state_gla, state_rglru, False, w, l, bs, ls_)
        yp, ys = _expert_ffn((gp["sets"], gs["sets"]), w["w_expert_gate"], w["w_expert_up"], w["w_expert_down"], l)
        xp = _combine(gp["bnd"], gp["slot"], yp, gp["x1"], mod_ctx[l], final_norm, final, bp, lp, gp["C"])
        xs = _combine(gs["bnd"], gs["slot"], ys, gs["x1"], mod_lat[l], final_norm, final, bs, ls_, gs["C"])
        gla_states.append(gp["gla_state"])
        lru_states.append(gp["lru_state"])
    y_prompt = xp.reshape(bp, lp, D_MODEL)
    y_sample = xs.reshape(bs, ls_, D_MODEL)
    new_state_gla = jnp.stack(gla_states, axis=1)
    new_state_rglru = jnp.stack(lru_states, axis=1)
    return (y_prompt, y_sample, new_state_gla, new_state_rglru)
```

```python
import functools
import math

import numpy as np
import jax
import jax.numpy as jnp
from jax import lax
from jax.experimental import pallas as pl
from jax.experimental.pallas import tpu as pltpu

F32 = jnp.float32
BF16 = jnp.bfloat16
I32 = jnp.int32

D_MODEL = 1024
DEPTH = 4
GRID_W = 64
N_MOD = 6
RMS_EPS = 1e-6
POS_BASE = 10000.0
GLA_HEADS = 4
GLA_DK = 64
GLA_DV = 128
GLA_RANK = 16
GLA_GATE_NORM = 16.0
FOURIER_GROUPS = 4
FOURIER_GW = 64
FOURIER_W = FOURIER_GROUPS * FOURIER_GW
LRU_BLOCKS = 4
LRU_BW = 64
LRU_W = LRU_BLOCKS * LRU_BW
LRU_C = 8.0
CONV_W = 4
N_EXPERTS = 16
EXPERT_FF = 1024
CAPACITY_FACTOR = 2
GLA_QK_W = GLA_HEADS * GLA_DK
GLA_V_W = GLA_HEADS * GLA_DV
GLA_LR_W = 2 * GLA_RANK
MIX_W = GLA_V_W + FOURIER_W + LRU_W
IN_W = 2 * GLA_QK_W + 2 * GLA_V_W + GLA_LR_W + FOURIER_W + 2 * LRU_W
_C_OG_END = 2 * GLA_QK_W + 2 * GLA_V_W
_C_LR_END = _C_OG_END + GLA_LR_W

LANES = 128
SUBLANES = 8
VMEM_LIMIT = 56 * 1024 * 1024

TOKEN_TILE = 512
OUT_TILE = 256
GLA_CH = 256
GLA_STEP_TOKENS = 1024
ROW_SUB = D_MODEL // LANES
MOE_ROWS = 256
GATHER_UNROLL = 8
DIGIT = 32
COMBINE_TILE = 256
COMBINE_CHUNK = 64
COMBINE_DEPTH = 512
BF16_ROWS = 16


def _cparams(*sem):
    return pltpu.CompilerParams(dimension_semantics=sem, vmem_limit_bytes=VMEM_LIMIT)


def _dot(a, b):
    return jnp.dot(a, b, preferred_element_type=F32)


def _dot_nt(a, b):
    return lax.dot_general(a, b, (((1,), (1,)), ((), ())), preferred_element_type=F32)


def _dot_tn(a, b):
    return lax.dot_general(a, b, (((0,), (0,)), ((), ())), preferred_element_type=F32)


def _split(x):
    hi = x.astype(BF16)
    lo = (x - hi.astype(F32)).astype(BF16)
    return hi, lo


def _sigmoid(x):
    return 1.0 / (1.0 + jnp.exp(-x))


def _rms(x):
    return x * lax.rsqrt(jnp.mean(x * x, axis=-1, keepdims=True) + RMS_EPS)


def _mod_kernel(cond_ref, w_ref, b_ref, o_ref):
    a = cond_ref[...]
    a = a * _sigmoid(a)
    o_ref[0] = _dot(a.astype(BF16), w_ref[0].astype(BF16)) + b_ref[0]


def _modulation(cond, w_mod, b_mod):
    tn = 1536
    nw = N_MOD * D_MODEL
    return pl.pallas_call(
        _mod_kernel,
        grid=(DEPTH, nw // tn),
        in_specs=[pl.BlockSpec((SUBLANES, D_MODEL), lambda l, j: (0, 0)),
                  pl.BlockSpec((1, D_MODEL, tn), lambda l, j: (l, 0, j)),
                  pl.BlockSpec((1, 1, tn), lambda l, j: (l, 0, j))],
        out_specs=pl.BlockSpec((1, SUBLANES, tn), lambda l, j: (l, 0, j)),
        out_shape=jax.ShapeDtypeStruct((DEPTH, SUBLANES, nw), F32),
        compiler_params=_cparams("arbitrary", "arbitrary"),
        name="modulation",
    )(cond, w_mod, b_mod.reshape(DEPTH, 1, nw))


def _inproj_kernel(*refs, mode):
    if mode == "plain":
        x_ref, mod_ref, n1_ref, w_ref = refs[:4]
        outs = refs[4:]
        x = x_ref[...]
    else:
        x_ref, pos_ref, mod_ref, n1_ref, w_ref, xo_ref = refs[:6]
        outs = refs[6:]
        x = x_ref[...] + pos_ref[...]
        xo_ref[...] = x
    qk_ref, v_ref, og_ref, uf_ref, ux_ref, ug_ref, lr_ref, wsc = outs

    @pl.when(pl.program_id(0) == 0)
    def _():
        for r in range(0, D_MODEL, 256):
            wsc[r:r + 256, 0:_C_OG_END] = w_ref[r:r + 256, 0:_C_OG_END].astype(BF16)
            wsc[r:r + 256, _C_OG_END:_C_OG_END + 768] = w_ref[r:r + 256, _C_LR_END:IN_W].astype(BF16)
            lrw = w_ref[r:r + 256, _C_OG_END:_C_LR_END].astype(BF16)
            wsc[r:r + 256, _C_OG_END + 768:_C_OG_END + 896] = jnp.concatenate(
                [lrw, jnp.zeros((256, LANES - GLA_LR_W), BF16)], axis=1)

    m = mod_ref[0]
    h = _rms(x) * n1_ref[...] * (1.0 + m[:, D_MODEL:2 * D_MODEL]) + m[:, 0:D_MODEL]
    hb = h.astype(BF16)
    qk_ref[...] = _dot(hb, wsc[:, 0:512])
    v_ref[...] = _dot(hb, wsc[:, 512:1024])
    og_ref[...] = _dot(hb, wsc[:, 1024:1536])
    uf_ref[...] = _dot(hb, wsc[:, 1536:1792])
    ux_ref[...] = _dot(hb, wsc[:, 1792:2048])
    ug_ref[...] = _dot(hb, wsc[:, 2048:2304])
    lr_ref[...] = _dot(hb, wsc[:, 2304:2432])


def _inproj(mode, x, extra, mod_l, norm1_l, w_in, l, B, L):
    T = B * L
    tm = TOKEN_TILE
    tpb = L // tm
    per_batch = mod_l.shape[0] > 1
    bidx = (lambda i: (i // tpb, 0, 0)) if per_batch else (lambda i: (0, 0, 0))
    row = lambda i: (i, 0)
    in_specs = [pl.BlockSpec((tm, D_MODEL), row)]
    args = [x]
    if mode == "pos":
        in_specs.append(pl.BlockSpec((tm, D_MODEL), lambda i: (i % tpb, 0)))
        args.append(extra)
    in_specs += [pl.BlockSpec((1, 1, N_MOD * D_MODEL), bidx),
                 pl.BlockSpec((1, D_MODEL), lambda i: (0, 0)),
                 pl.BlockSpec((None, D_MODEL, IN_W), lambda i: (l, 0, 0))]
    args += [mod_l, norm1_l.reshape(1, D_MODEL), w_in]
    widths = [512, 512, 512, 256, 256, 256, LANES]
    out_specs = [pl.BlockSpec((tm, w), row) for w in widths]
    out_shape = [jax.ShapeDtypeStruct((T, w), F32) for w in widths]
    if mode != "plain":
        out_specs = [pl.BlockSpec((tm, D_MODEL), row)] + out_specs
        out_shape = [jax.ShapeDtypeStruct((T, D_MODEL), F32)] + out_shape
    res = pl.pallas_call(
        functools.partial(_inproj_kernel, mode=mode),
        grid=(T // tm,),
        in_specs=in_specs,
        out_specs=out_specs,
        out_shape=out_shape,
        scratch_shapes=[pltpu.VMEM((D_MODEL, 2432), BF16)],
        compiler_params=_cparams("arbitrary"),
        name="inproj_" + mode,
    )(*args)
    if mode == "plain":
        return (x,) + tuple(res)
    return tuple(res)


def _gla_kernel(*refs, L, nb, has_s0, want_state):
    qk_ref, v_ref, og_ref, lr_ref, wdec_ref, bdec_ref, gn_ref = refs[:7]
    p = 7
    s0_ref = None
    if has_s0:
        s0_ref = refs[p]
        p += 1
    o_ref = refs[p]
    p += 1
    sn_ref = None
    if want_state:
        sn_ref = refs[p]
        p += 1
    g_scr, oacc, s_scr = refs[p:p + 3]

    ch = GLA_CH
    n_chunks = L // ch
    kw = GLA_QK_W
    vw = GLA_V_W

    z16 = jnp.zeros((GLA_RANK, kw), F32)
    wc = jnp.concatenate([
        jnp.concatenate([wdec_ref[0], z16], axis=1),
        jnp.concatenate([z16, wdec_ref[1]], axis=1),
        jnp.zeros((LANES - GLA_LR_W, 2 * kw), F32)], axis=0).astype(BF16)
    bias = jnp.concatenate([bdec_ref[0], bdec_ref[1]], axis=1)
    row = lax.broadcasted_iota(I32, (ch, ch), 0)
    col = lax.broadcasted_iota(I32, (ch, ch), 1)
    lane_head = lax.broadcasted_iota(I32, (1, kw), 1) // GLA_DK
    blockdiag = (lax.broadcasted_iota(I32, (kw, vw), 0) // GLA_DK) == (lax.broadcasted_iota(I32, (kw, vw), 1) // GLA_DV)
    ones_t = jnp.ones((ch, LANES), BF16)
    gn = gn_ref[...]

    def one_batch(bi):
        z = _dot(lr_ref[bi].astype(BF16), wc) + bias
        g_scr[...] = (jnp.minimum(z, 0.0) - jnp.log1p(jnp.exp(-jnp.abs(z)))) * (1.0 / GLA_GATE_NORM)

        def finish(o, r0):
            parts = []
            for h in range(GLA_HEADS):
                parts.append(_rms(o[:, h * GLA_DV:(h + 1) * GLA_DV]) * gn)
            ogv = og_ref[bi, r0:r0 + ch, :]
            return (jnp.concatenate(parts, axis=1) * (ogv * _sigmoid(ogv))).astype(BF16)

        for d in range(2):
            allowed = (col <= row) if d == 0 else (col >= row)
            tri = jnp.where(allowed, 1.0, 0.0).astype(BF16)
            if has_s0:
                s_scr[...] = jnp.zeros((kw, vw), F32)
                for h in range(GLA_HEADS):
                    s_scr[h * GLA_DK:(h + 1) * GLA_DK, h * GLA_DV:(h + 1) * GLA_DV] = s0_ref[bi, d, h]
            for i in range(n_chunks):
                n = i if d == 0 else n_chunks - 1 - i
                r0 = n * ch
                state_is_zero = (i == 0) and not has_s0
                gch = g_scr[r0:r0 + ch, d * kw:(d + 1) * kw]
                g_hi, g_lo = _split(gch)
                b = _dot(tri, g_hi) + _dot(tri, g_lo)
                b_last = b[ch - 1:ch, :] if d == 0 else b[0:1, :]
                bc = b - b[ch // 2:ch // 2 + 1, :]
                qch = qk_ref[bi, r0:r0 + ch, 0:kw] * (GLA_DK ** -0.5)
                kch = qk_ref[bi, r0:r0 + ch, kw:2 * kw]
                vb = v_ref[bi, r0:r0 + ch, :].astype(BF16)
                q_s = (qch * jnp.exp(bc)).astype(BF16)
                k_s = (kch * jnp.exp(-bc)).astype(BF16)
                zero_q = jnp.zeros_like(q_s)
                qbig = jnp.concatenate([jnp.where(lane_head == h, q_s, zero_q) for h in range(GLA_HEADS)], axis=0)
                scores = _dot_nt(qbig, k_s)
                parts = []
                for h in range(GLA_HEADS):
                    ph = jnp.where(allowed, scores[h * ch:(h + 1) * ch, :], 0.0).astype(BF16)
                    parts.append(_dot(ph, vb[:, h * GLA_DV:(h + 1) * GLA_DV]))
                o = jnp.concatenate(parts, axis=1)
                if not state_is_zero:
                    q_t = (qch * jnp.exp(b)).astype(BF16)
                    o = o + _dot(q_t, s_scr[...].astype(BF16))
                if (i < n_chunks - 1) or want_state:
                    k_d = (kch * jnp.exp(b_last - b)).astype(BF16)
                    ds = jnp.where(blockdiag, _dot_tn(k_d, vb), 0.0)
                    if state_is_zero:
                        s_scr[...] = ds
                    else:
                        dcol = _dot_tn(g_hi, ones_t) + _dot_tn(g_lo, ones_t)
                        dec = jnp.exp(dcol)
                        s_scr[...] = s_scr[...] * jnp.concatenate([dec] * (vw // LANES), axis=1) + ds
                if d == 0:
                    oacc[r0:r0 + ch, :] = o
                else:
                    o_ref[bi, r0:r0 + ch, :] = finish(o + oacc[r0:r0 + ch, :], r0)
            if want_state:
                for h in range(GLA_HEADS):
                    sn_ref[bi, d, h] = s_scr[h * GLA_DK:(h + 1) * GLA_DK, h * GLA_DV:(h + 1) * GLA_DV]

    if nb == 1:
        one_batch(0)
    else:
        def body(bi, carry):
            one_batch(bi)
            return carry
        lax.fori_loop(0, nb, body, 0)


def _gla(qk, v, og, lr, wdec_l, bdec_l, gn_l, s0, B, L, want_state):
    has_s0 = s0 is not None
    nb = min(B, max(1, GLA_STEP_TOKENS // L))
    blk = lambda w: pl.BlockSpec((nb, L, w), lambda b: (b, 0, 0))
    in_specs = [blk(512), blk(512), blk(512), blk(LANES),
                pl.BlockSpec((2, GLA_RANK, GLA_QK_W), lambda b: (0, 0, 0)),
                pl.BlockSpec((2, 1, GLA_QK_W), lambda b: (0, 0, 0)),
                pl.BlockSpec((1, GLA_DV), lambda b: (0, 0))]
    args = [qk.reshape(B, L, 512), v.reshape(B, L, 512), og.reshape(B, L, 512), lr.reshape(B, L, LANES),
            wdec_l, bdec_l.reshape(2, 1, GLA_QK_W), gn_l.reshape(1, GLA_DV)]
    st_spec = pl.BlockSpec((nb, 2, GLA_HEADS, GLA_DK, GLA_DV), lambda b: (b, 0, 0, 0, 0))
    if has_s0:
        in_specs.append(st_spec)
        args.append(s0)
    out_specs = [pl.BlockSpec((nb, L, GLA_V_W), lambda b: (b, 0, 0))]
    out_shape = [jax.ShapeDtypeStruct((B, L, GLA_V_W), BF16)]
    if want_state:
        out_specs.append(st_spec)
        out_shape.append(jax.ShapeDtypeStruct((B, 2, GLA_HEADS, GLA_DK, GLA_DV), F32))
    res = pl.pallas_call(
        functools.partial(_gla_kernel, L=L, nb=nb, has_s0=has_s0, want_state=want_state),
        grid=(B // nb,),
        in_specs=in_specs,
        out_specs=out_specs,
        out_shape=out_shape,
        scratch_shapes=[pltpu.VMEM((L, 2 * GLA_QK_W), F32),
                        pltpu.VMEM((L, GLA_V_W), F32),
                        pltpu.VMEM((GLA_QK_W, GLA_V_W), F32)],
        compiler_params=_cparams("arbitrary"),
        name="gla",
    )(*args)
    o = res[0].reshape(B * L, GLA_V_W)
    return o, (res[1] if want_state else None)


def _fft_tables(L):
    m = np.arange(L, dtype=np.int64)
    ang = 2.0 * np.pi * ((m[:, None] * m[None, :]) % L) / L
    cc = np.concatenate([np.cos(ang), -np.sin(ang)], axis=1)
    c = np.arange(FOURIER_GW, dtype=np.int64)
    angc = 2.0 * np.pi * ((c[:, None] * c[None, :]) % FOURIER_GW) / FOURIER_GW
    scale = 1.0 / math.sqrt(L * FOURIER_GW)
    eye = np.eye(FOURIER_GROUPS)
    bdc = np.kron(eye, np.cos(angc) * scale)
    bds = np.kron(eye, np.sin(angc) * scale)
    return (jnp.asarray(cc, dtype=F32), jnp.asarray(bdc, dtype=F32), jnp.asarray(bds, dtype=F32))


def _fft_kernel(u_ref, cc_ref, bdc_ref, bds_ref, o_ref):
    bdc = bdc_ref[...].astype(BF16)
    bds = bds_ref[...].astype(BF16)
    cc = cc_ref[...].astype(BF16)
    for bi in range(u_ref.shape[0]):
        u_hi, u_lo = _split(u_ref[bi])
        uc = _dot(u_hi, bdc) + _dot(u_lo, bdc)
        us = _dot(u_hi, bds) + _dot(u_lo, bds)
        w_hi, w_lo = _split(jnp.concatenate([uc, us], axis=0))
        o_ref[bi] = (_dot(cc, w_hi) + _dot(cc, w_lo)).astype(BF16)


def _fft(uf, B, L):
    cc, bdc, bds = _fft_tables(L)
    nb = min(B, max(1, GLA_STEP_TOKENS // L))
    res = pl.pallas_call(
        _fft_kernel,
        grid=(B // nb,),
        in_specs=[pl.BlockSpec((nb, L, FOURIER_W), lambda b: (b, 0, 0)),
                  pl.BlockSpec((L, 2 * L), lambda b: (0, 0)),
                  pl.BlockSpec((FOURIER_W, FOURIER_W), lambda b: (0, 0)),
                  pl.BlockSpec((FOURIER_W, FOURIER_W), lambda b: (0, 0))],
        out_specs=pl.BlockSpec((nb, L, FOURIER_W), lambda b: (b, 0, 0)),
        out_shape=jax.ShapeDtypeStruct((B, L, FOURIER_W), BF16),
        compiler_params=_cparams("arbitrary"),
        name="fourier",
    )(uf.reshape(B, L, FOURIER_W), cc, bdc, bds)
    return res.reshape(B * L, FOURIER_W)


def _lru_kernel(*refs, L, has_s0, want_state):
    ux_ref, ug_ref, cw_ref, cb_ref, wa_ref, ba_ref, wx_ref, bx_ref, lam_ref = refs[:9]
    p = 9
    s0_ref = None
    if has_s0:
        s0_ref = refs[p]
        p += 1
    o_ref = refs[p]
    p += 1
    sn_ref = None
    if want_state:
        sn_ref = refs[p]
        p += 1
    bd_scr = refs[p]
    scan_scr = refs[p + 1:p + 9]

    @pl.when(pl.program_id(0) == 0)
    def _():
        r = lax.broadcasted_iota(I32, (LRU_BW, LRU_W), 0)
        c = lax.broadcasted_iota(I32, (LRU_BW, LRU_W), 1)
        for d in range(2):
            for gi, w_ref in enumerate((wa_ref, wx_ref)):
                pieces = []
                for h in range(LRU_BLOCKS):
                    place = jnp.where(c == r + h * LRU_BW, 1.0, 0.0).astype(BF16)
                    pieces.append(_dot(w_ref[d, h].astype(BF16), place))
                bd_scr[2 * d + gi] = jnp.concatenate(pieces, axis=0).astype(BF16)

    t = lax.broadcasted_iota(I32, (L, 1), 0)
    x = ux_ref[0]
    xm2 = jnp.where(t >= 2, pltpu.roll(x, 2, 0), 0.0)
    xm1 = jnp.where(t >= 1, pltpu.roll(x, 1, 0), 0.0)
    xp1 = jnp.where(t <= L - 2, pltpu.roll(x, L - 1, 0), 0.0)
    xc = xm2 * cw_ref[0:1, :] + xm1 * cw_ref[1:2, :] + x * cw_ref[2:3, :] + xp1 * cw_ref[3:4, :] + cb_ref[...]
    xcb = xc.astype(BF16)

    nb = L // SUBLANES
    pitch = nb + SUBLANES
    n_slab = LRU_W // LANES
    sub = lax.broadcasted_iota(I32, (SUBLANES, LANES), 0)
    hsum = [[None] * n_slab for _ in range(SUBLANES)]
    for d in range(2):
        r = 0.5 + 0.5 * jnp.tanh(0.5 * (_dot(xcb, bd_scr[2 * d]) + ba_ref[d]))
        ig = 0.5 + 0.5 * jnp.tanh(0.5 * (_dot(xcb, bd_scr[2 * d + 1]) + bx_ref[d]))
        lam = lam_ref[d]
        softplus = jnp.maximum(-lam, 0.0) + jnp.log1p(jnp.exp(-jnp.abs(lam)))
        a = jnp.exp(-LRU_C * r * softplus)
        u = jnp.sqrt(1.0 - a * a) * (ig * xc)
        a_scr, u_scr, h_scr, p_scr = scan_scr[4 * d:4 * d + 4]
        for s in range(SUBLANES):
            for k in range(n_slab):
                a_scr[k, s * pitch:s * pitch + nb, :] = a[s * nb:(s + 1) * nb, k * LANES:(k + 1) * LANES]
                u_scr[k, s * pitch:s * pitch + nb, :] = u[s * nb:(s + 1) * nb, k * LANES:(k + 1) * LANES]
        steps = range(nb) if d == 0 else range(nb - 1, -1, -1)
        for k in range(n_slab):
            h = jnp.zeros((SUBLANES, LANES), F32)
            prod = jnp.ones((SUBLANES, LANES), F32)
            for i in steps:
                rows = pl.ds(i, SUBLANES, stride=pitch)
                ai = a_scr[k, rows, :]
                h = ai * h + u_scr[k, rows, :]
                prod = ai * prod
                h_scr[k, rows, :] = h
                p_scr[k, rows, :] = prod
            if has_s0:
                h0 = jnp.broadcast_to(s0_ref[0, d:d + 1, k * LANES:(k + 1) * LANES], (SUBLANES, LANES))
            else:
                h0 = jnp.zeros((SUBLANES, LANES), F32)
            first = 0 if d == 0 else SUBLANES - 1
            carry = jnp.where(sub == first, h0, 0.0)
            for j in range(1, SUBLANES):
                s = j if d == 0 else SUBLANES - 1 - j
                moved = pltpu.roll(prod * carry + h, 1 if d == 0 else SUBLANES - 1, 0)
                carry = jnp.where(sub == s, moved, carry)
            if want_state:
                last = SUBLANES - 1 - first
                sn_ref[0, d:d + 1, k * LANES:(k + 1) * LANES] = (prod * carry + h)[last:last + 1, :]
            for s in range(SUBLANES):
                blk = h_scr[k, s * pitch:s * pitch + nb, :] + p_scr[k, s * pitch:s * pitch + nb, :] * carry[s:s + 1, :]
                hsum[s][k] = blk if hsum[s][k] is None else hsum[s][k] + blk

    for s in range(SUBLANES):
        ugv = ug_ref[0, s * nb:(s + 1) * nb, :]
        gelu = 0.5 * ugv * (1.0 + jnp.tanh(math.sqrt(2.0 / math.pi) * (ugv + 0.044715 * (ugv * ugv * ugv))))
        o_ref[0, s * nb:(s + 1) * nb, :] = (jnp.concatenate(hsum[s], axis=1) * gelu).astype(BF16)


def _lru(ux, ug, cw_l, cb_l, wa_l, ba_l, wx_l, bx_l, lam_l, s0, B, L, want_state):
    has_s0 = s0 is not None
    blk = pl.BlockSpec((1, L, LRU_W), lambda b: (b, 0, 0))
    vec2 = pl.BlockSpec((2, 1, LRU_W), lambda b: (0, 0, 0))
    wsp = pl.BlockSpec((2, LRU_BLOCKS, LRU_BW, LRU_BW), lambda b: (0, 0, 0, 0))
    in_specs = [blk, blk,
                pl.BlockSpec((CONV_W, LRU_W), lambda b: (0, 0)),
                pl.BlockSpec((1, LRU_W), lambda b: (0, 0)),
                wsp, vec2, wsp, vec2, vec2]
    args = [ux.reshape(B, L, LRU_W), ug.reshape(B, L, LRU_W), cw_l, cb_l.reshape(1, LRU_W),
            wa_l, ba_l.reshape(2, 1, LRU_W), wx_l, bx_l.reshape(2, 1, LRU_W), lam_l.reshape(2, 1, LRU_W)]
    st_spec = pl.BlockSpec((1, 2, LRU_W), lambda b: (b, 0, 0))
    if has_s0:
        in_specs.append(st_spec)
        args.append(s0)
    out_specs = [blk]
    out_shape = [jax.ShapeDtypeStruct((B, L, LRU_W), BF16)]
    if want_state:
        out_specs.append(st_spec)
        out_shape.append(jax.ShapeDtypeStruct((B, 2, LRU_W), F32))
    res = pl.pallas_call(
        functools.partial(_lru_kernel, L=L, has_s0=has_s0, want_state=want_state),
        grid=(B,),
        in_specs=in_specs,
        out_specs=out_specs,
        out_shape=out_shape,
        scratch_shapes=[pltpu.VMEM((4, LRU_W, LRU_W), BF16)]
        + [pltpu.VMEM((LRU_W // LANES, L + SUBLANES * SUBLANES, LANES), F32)] * 8,
        compiler_params=_cparams("arbitrary"),
        name="rglru",
    )(*args)
    return res[0].reshape(B * L, LRU_W), (res[1] if want_state else None)


def _outproj_kernel(og_ref, of_ref, ol_ref, x_ref, mod_ref, n2_ref, wout_ref, wr_ref,
                    x1_ref, hx_ref, afft_ref, wsc, wrs):
    @pl.when(pl.program_id(0) == 0)
    def _():
        for r in range(0, MIX_W, 256):
            wsc[r:r + 256, :] = wout_ref[r:r + 256, :].astype(BF16)
        wrs[...] = jnp.concatenate([wr_ref[...], jnp.zeros((D_MODEL, LANES - N_EXPERTS), F32)], axis=1)

    m = mod_ref[0]
    y = (_dot(og_ref[...], wsc[0:GLA_V_W, :])
         + _dot(of_ref[...], wsc[GLA_V_W:GLA_V_W + FOURIER_W, :])
         + _dot(ol_ref[...], wsc[GLA_V_W + FOURIER_W:MIX_W, :]))
    x1 = x_ref[...] + m[:, 2 * D_MODEL:3 * D_MODEL] * y
    x1_ref[...] = x1
    h2 = _rms(x1) * n2_ref[...] * (1.0 + m[:, 4 * D_MODEL:5 * D_MODEL]) + m[:, 3 * D_MODEL:4 * D_MODEL]
    h_hi = h2.astype(BF16)
    h_hi32 = h_hi.astype(F32)
    tm = h2.shape[0]
    for s in range(ROW_SUB):
        hx_ref[pl.ds(s, tm, stride=ROW_SUB), :] = h_hi32[:, s * LANES:(s + 1) * LANES]
    h_lo = (h2 - h_hi32).astype(BF16)
    w_hi, w_lo = _split(wrs[...])
    logits = _dot(h_hi, w_hi) + _dot(h_lo, w_hi) + _dot(h_hi, w_lo)
    lane = lax.broadcasted_iota(I32, logits.shape, 1)
    logits = jnp.where(lane < N_EXPERTS, logits, -jnp.inf)
    ex = jnp.exp(logits - jnp.max(logits, axis=-1, keepdims=True))
    aff = ex / jnp.sum(ex, axis=-1, keepdims=True)
    afft_ref[...] = aff.T[0:N_EXPERTS, :]


def _outproj(o_gla, o_fft, o_lru, x, mod_l, norm2_l, w_out, w_router_l, l, B, L):
    T = B * L
    tm = OUT_TILE
    tpb = L // tm
    per_batch = mod_l.shape[0] > 1
    bidx = (lambda i: (i // tpb, 0, 0)) if per_batch else (lambda i: (0, 0, 0))
    row = lambda i: (i, 0)
    return pl.pallas_call(
        _outproj_kernel,
        grid=(T // tm,),
        in_specs=[pl.BlockSpec((tm, GLA_V_W), row), pl.BlockSpec((tm, FOURIER_W), row),
                  pl.BlockSpec((tm, LRU_W), row), pl.BlockSpec((tm, D_MODEL), row),
                  pl.BlockSpec((1, 1, N_MOD * D_MODEL), bidx),
                  pl.BlockSpec((1, D_MODEL), lambda i: (0, 0)),
                  pl.BlockSpec((None, MIX_W, D_MODEL), lambda i: (l, 0, 0)),
                  pl.BlockSpec((D_MODEL, N_EXPERTS), lambda i: (0, 0))],
        out_specs=[pl.BlockSpec((tm, D_MODEL), row),
                   pl.BlockSpec((tm * ROW_SUB, LANES), lambda i: (i, 0)),
                   pl.BlockSpec((N_EXPERTS, tm), lambda i: (0, i))],
        out_shape=[jax.ShapeDtypeStruct((T, D_MODEL), F32),
                   jax.ShapeDtypeStruct((T * ROW_SUB, LANES), F32),
                   jax.ShapeDtypeStruct((N_EXPERTS, T), F32)],
        scratch_shapes=[pltpu.VMEM((MIX_W, D_MODEL), BF16), pltpu.VMEM((D_MODEL, LANES), F32)],
        compiler_params=_cparams("arbitrary"),
        name="outproj",
    )(o_gla, o_fft, o_lru, x, mod_l, norm2_l.reshape(1, D_MODEL), w_out, w_router_l)


def _prefix_lanes(x):
    T = x.shape[1]
    w = 256
    nb = T // w
    stacked = jnp.concatenate([x[:, j * w:(j + 1) * w] for j in range(nb)], axis=0)
    upper = jnp.where(lax.broadcasted_iota(I32, (w, w), 0) <= lax.broadcasted_iota(I32, (w, w), 1), 1.0, 0.0)
    pe = _dot(stacked, upper.astype(BF16))
    carry = jnp.zeros((N_EXPERTS, 1), F32)
    outs = []
    for j in range(nb):
        blk = pe[j * N_EXPERTS:(j + 1) * N_EXPERTS, :]
        outs.append(blk + carry)
        carry = carry + blk[:, w - 1:w]
    return jnp.concatenate(outs, axis=1)


def _topk_kernel(aff_ref, out_ref, gate_ref, slot_ref, bnd_ref, *, T, C):
    n_a = C // DIGIT
    aff = aff_ref[...]
    bits = jnp.zeros((N_EXPERTS, 1), I32)
    for bit in range(30, -1, -1):
        cand = bits | (1 << bit)
        cnt = jnp.sum(jnp.where(aff >= pltpu.bitcast(cand, F32), 1.0, 0.0), axis=1, keepdims=True)
        bits = jnp.where(cnt >= C, cand, bits)
    thr = pltpu.bitcast(bits, F32)
    gt = aff > thr
    eq = aff == thr
    eqf = jnp.where(eq, 1.0, 0.0)
    need = C - jnp.sum(jnp.where(gt, 1.0, 0.0), axis=1, keepdims=True)
    eq_before = _prefix_lanes(eqf.astype(BF16)) - eqf
    sel = gt | (eq & (eq_before < need))
    self32 = jnp.where(sel, 1.0, 0.0)
    cnt = _prefix_lanes(self32.astype(BF16))
    slot_ref[...] = jnp.where(sel, cnt - 1.0, -1.0)
    tok = lax.broadcasted_iota(I32, (1, T), 1)
    lane = lax.broadcasted_iota(I32, (1, LANES), 1)
    bnd = jnp.zeros((N_EXPERTS, LANES), F32)
    for j in range(1, T // COMBINE_TILE + 1):
        before = jnp.sum(jnp.where(tok < j * COMBINE_TILE, self32, 0.0), axis=1, keepdims=True)
        bnd = jnp.where(lane == j, before, bnd)
    bnd_ref[...] = bnd

    p_dig = jnp.floor(cnt * (1.0 / DIGIT))
    q_dig = cnt - DIGIT * p_dig
    a_col = lax.broadcasted_iota(I32, (n_a, 1), 0).astype(F32)
    b_col = lax.broadcasted_iota(I32, (DIGIT, 1), 0).astype(F32)
    slot = cnt - 1.0
    ps_dig = jnp.where(sel, jnp.floor(slot * (1.0 / DIGIT)), -1.0)
    qs_dig = slot - DIGIT * jnp.floor(slot * (1.0 / DIGIT))
    aff_hi = aff.astype(BF16).astype(F32)
    aff_lo = aff - aff_hi
    kc = min(T, 2048)
    acc = jnp.zeros((N_EXPERTS * n_a, N_EXPERTS * DIGIT), F32)
    gacc = jnp.zeros((N_EXPERTS * n_a, N_EXPERTS * DIGIT), F32)
    for c0 in range(0, T, kc):
        tk = slice(c0, c0 + kc)
        u = jnp.concatenate([jnp.where(p_dig[e:e + 1, tk] == a_col, 1.0, 0.0).astype(BF16)
                             for e in range(N_EXPERTS)], axis=0)
        v = jnp.concatenate([jnp.where(q_dig[e:e + 1, tk] <= b_col, 1.0, 0.0).astype(BF16)
                             for e in range(N_EXPERTS)], axis=0)
        acc = acc + _dot_nt(u, v)
        us = jnp.concatenate([jnp.where(ps_dig[e:e + 1, tk] == a_col, 1.0, 0.0).astype(BF16)
                              for e in range(N_EXPERTS)], axis=0)
        for part in (aff_hi, aff_lo):
            vs = jnp.concatenate([jnp.where(qs_dig[e:e + 1, tk] == b_col, part[e:e + 1, tk], 0.0).astype(BF16)
                                  for e in range(N_EXPERTS)], axis=0)
            gacc = gacc + _dot_nt(us, vs)
    below = jnp.concatenate([jnp.sum(jnp.where(p_dig[e:e + 1, :] < a_col, 1.0, 0.0), axis=1, keepdims=True)
                             for e in range(N_EXPERTS)], axis=0)
    r_i = lax.broadcasted_iota(I32, acc.shape, 0) // n_a
    c_i = lax.broadcasted_iota(I32, acc.shape, 1) // DIGIT

    def own_block(x):
        x = jnp.where(r_i == c_i, x, 0.0)
        x = x[:, 0:256] + x[:, 256:512]
        x = x[:, 0:LANES] + x[:, LANES:2 * LANES]
        x = x + pltpu.roll(x, 64, 1)
        return x + pltpu.roll(x, 32, 1)

    out_ref[...] = own_block(acc) + below
    gate_ref[...] = own_block(gacc)


def _expert_choice(aff_t, T, C):
    n_a = C // DIGIT
    res = pl.pallas_call(
        functools.partial(_topk_kernel, T=T, C=C),
        grid=(1,),
        in_specs=[pl.BlockSpec((N_EXPERTS, T), lambda i: (0, 0))],
        out_specs=[pl.BlockSpec((N_EXPERTS * n_a, LANES), lambda i: (0, 0)),
                   pl.BlockSpec((N_EXPERTS * n_a, LANES), lambda i: (0, 0)),
                   pl.BlockSpec((N_EXPERTS, T), lambda i: (0, 0)),
                   pl.BlockSpec((N_EXPERTS, LANES), lambda i: (0, 0))],
        out_shape=[jax.ShapeDtypeStruct((N_EXPERTS * n_a, LANES), F32),
                   jax.ShapeDtypeStruct((N_EXPERTS * n_a, LANES), F32),
                   jax.ShapeDtypeStruct((N_EXPERTS, T), F32),
                   jax.ShapeDtypeStruct((N_EXPERTS, LANES), F32)],
        compiler_params=_cparams("arbitrary"),
        name="expert_choice",
    )(aff_t)
    idx = res[0][:, 0:DIGIT].astype(I32).reshape(N_EXPERTS, C)
    gate = res[1][:, 0:DIGIT].reshape(N_EXPERTS, 1, C)
    bnd = res[3][:, 0:T // COMBINE_TILE + 1].astype(I32)
    return idx, gate, res[2], bnd


def _ffn_kernel(*refs, caps):
    n = len(caps)
    idx_refs, hx_refs, gate_refs = refs[0:n], refs[n:2 * n], refs[2 * n:3 * n]
    wg_ref, wu_ref, wd_ref = refs[3 * n:3 * n + 3]
    y_refs = refs[3 * n + 3:4 * n + 3]
    xbufs = refs[4 * n + 3:5 * n + 3]
    wgb, wub, wdb, sems = refs[5 * n + 3:5 * n + 7]
    e = pl.program_id(0)
    last = pl.num_programs(0) - 1
    buf = e % 2
    nxt = 1 - buf
    following = jnp.where(e == last, 0, e + 1)

    def fetch_row(q, expert, c, into, k):
        src = hx_refs[q].at[pl.ds(pl.multiple_of(idx_refs[q][expert, c] * ROW_SUB, ROW_SUB), ROW_SUB)]
        dst = xbufs[q].at[into, pl.ds(pl.multiple_of(c * ROW_SUB, ROW_SUB), ROW_SUB)]
        pltpu.make_async_copy(src, dst, sems.at[2 * q + into]).start(priority=k % 2)

    def wait_rows(q, which):
        pltpu.make_async_copy(hx_refs[q].at[pl.ds(0, caps[q] * ROW_SUB)], xbufs[q].at[which],
                              sems.at[2 * q + which]).wait()

    @pl.when(e == 0)
    def _():
        for q in range(n):
            def body(i, carry, q=q):
                for k in range(GATHER_UNROLL):
                    fetch_row(q, 0, i * GATHER_UNROLL + k, 0, k)
                return carry
            lax.fori_loop(0, caps[q] // GATHER_UNROLL, body, 0)

    for r in range(0, D_MODEL, 256):
        wgb[r:r + 256, :] = wg_ref[r:r + 256, :].astype(BF16)
        wub[r:r + 256, :] = wu_ref[r:r + 256, :].astype(BF16)
        wdb[r:r + 256, :] = wd_ref[r:r + 256, :].astype(BF16)

    for q in range(n):
        wait_rows(q, buf)
        step = min(MOE_ROWS, caps[q])
        for r0 in range(0, caps[q], step):
            for k in range(step):
                fetch_row(q, following, r0 + k, nxt, k)
            x = jnp.concatenate([xbufs[q][buf, pl.ds(r0 * ROW_SUB + s, step, stride=ROW_SUB), :]
                                 for s in range(ROW_SUB)], axis=1).astype(BF16)
            g = _dot(x, wgb[...])
            u = _dot(x, wub[...])
            hid = (g * _sigmoid(g) * u).astype(BF16)
            gate = jnp.broadcast_to(gate_refs[q][:, r0:r0 + step], (LANES, step)).T[:, 0:1]
            y_refs[q][pl.ds(r0, step), :] = (_dot(hid, wdb[...]) * gate).astype(BF16)

    @pl.when(e == last)
    def _():
        for q in range(n):
            wait_rows(q, nxt)


def _expert_ffn(sets, wg, wu, wd, l):
    n = len(sets)
    caps = tuple(s[0].shape[1] for s in sets)
    imap = lambda e, *idx_refs: (l, e, 0, 0)
    wspec = pl.BlockSpec((None, None, D_MODEL, EXPERT_FF), imap)
    emap = lambda e, *idx_refs: (e, 0, 0)
    return pl.pallas_call(
        functools.partial(_ffn_kernel, caps=caps),
        grid_spec=pltpu.PrefetchScalarGridSpec(
            num_scalar_prefetch=n,
            grid=(N_EXPERTS,),
            in_specs=[pl.BlockSpec(memory_space=pl.ANY)] * n
            + [pl.BlockSpec((None, 1, c), emap) for c in caps]
            + [wspec, wspec, pl.BlockSpec((None, None, EXPERT_FF, D_MODEL), imap)],
            out_specs=[pl.BlockSpec((None, c, D_MODEL), emap) for c in caps],
            scratch_shapes=[pltpu.VMEM((2, c * ROW_SUB, LANES), F32) for c in caps]
            + [pltpu.VMEM((D_MODEL, EXPERT_FF), BF16),
               pltpu.VMEM((D_MODEL, EXPERT_FF), BF16),
               pltpu.VMEM((EXPERT_FF, D_MODEL), BF16),
               pltpu.SemaphoreType.DMA((2 * n,))]),
        out_shape=[jax.ShapeDtypeStruct((N_EXPERTS, c, D_MODEL), BF16) for c in caps],
        compiler_params=pltpu.CompilerParams(dimension_semantics=("arbitrary",), vmem_limit_bytes=VMEM_LIMIT,
                                             disable_bounds_checks=True),
        name="expert_ffn",
    )(*[s[0] for s in sets], *[s[1] for s in sets], *[s[2] for s in sets], wg, wu, wd)


def _combine_kernel(bnd_ref, slot_ref, y_hbm, x1_ref, gmod_ref, fn_ref, o_ref, ybuf, onehot, acc, sems, *, C, final):
    j = pl.program_id(0)
    cur = j % 2
    ch = COMBINE_CHUNK
    group = COMBINE_DEPTH // ch
    tt = x1_ref.shape[0]
    w_col = lax.broadcasted_iota(I32, (ch, 1), 0)

    def chunks_of(tile, e):
        first = bnd_ref[e, tile] & (-BF16_ROWS)
        return first, lax.div(bnd_ref[e, tile + 1] - first + (ch - 1), jnp.int32(ch))

    def padded(total):
        return lax.div(total + (group - 1), jnp.int32(group)) * group

    def stage(tile, into):
        tok0 = pl.multiple_of(tile * tt, tt)
        total = jnp.int32(0)
        for e in range(N_EXPERTS):
            first, n_chunks = chunks_of(tile, e)

            def fetch_chunk(c, carry, e=e, first=first, base=total):
                want_lo = first + c * ch
                src_row = pl.multiple_of(jnp.minimum(want_lo, C - ch), BF16_ROWS)
                dst_row = pl.multiple_of((base + c) * ch, ch)
                pltpu.make_async_copy(y_hbm.at[e, pl.ds(src_row, ch)], ybuf.at[into, pl.ds(dst_row, ch)],
                                      sems.at[into]).start()
                row = src_row + w_col
                want = jnp.where(row >= want_lo, row, -2).astype(F32)
                hit = slot_ref[e:e + 1, pl.ds(tok0, tt)] == want
                onehot[into, pl.ds(dst_row, ch), :] = jnp.where(hit, 1.0, 0.0).astype(BF16)
                return carry

            lax.fori_loop(0, n_chunks, fetch_chunk, 0)
            total = total + n_chunks

        def clear_chunk(k, carry):
            rows = pl.ds(pl.multiple_of(k * ch, ch), ch)
            onehot[into, rows, :] = jnp.zeros((ch, tt), BF16)
            ybuf[into, rows, :] = jnp.zeros((ch, D_MODEL), BF16)
            return carry

        lax.fori_loop(total, padded(total), clear_chunk, 0)

    @pl.when(j == 0)
    def _():
        stage(0, 0)

    @pl.when(j + 1 < pl.num_programs(0))
    def _():
        stage(j + 1, 1 - cur)

    total = jnp.int32(0)
    for e in range(N_EXPERTS):
        total = total + chunks_of(j, e)[1]

    def wait_chunk(k, carry):
        pltpu.make_async_copy(y_hbm.at[0, pl.ds(0, ch)], ybuf.at[cur, pl.ds(0, ch)], sems.at[cur]).wait()
        return carry

    lax.fori_loop(0, total, wait_chunk, 0)
    acc[...] = jnp.zeros(acc.shape, F32)

    def add_group(g, carry):
        rows = pl.ds(pl.multiple_of(g * COMBINE_DEPTH, COMBINE_DEPTH), COMBINE_DEPTH)
        acc[...] += _dot_tn(onehot[cur, rows, :], ybuf[cur, rows, :])
        return carry

    lax.fori_loop(0, lax.div(padded(total), jnp.int32(group)), add_group, 0)
    x = x1_ref[...] + gmod_ref[0][:, 5 * D_MODEL:6 * D_MODEL] * acc[...]
    o_ref[...] = _rms(x) * fn_ref[...] if final else x


def _combine(bnd, slot, y, x1, gmod, final_norm, final, B, L, C):
    T = B * L
    tt = COMBINE_TILE
    tpb = L // tt if L >= tt else None
    per_batch = gmod.shape[0] > 1
    if per_batch:
        bidx = lambda j, b: (j // tpb, 0, 0)
    else:
        bidx = lambda j, b: (0, 0, 0)
    max_chunks = N_EXPERTS * (tt // COMBINE_CHUNK + 2)
    max_rows = pl.cdiv(max_chunks * COMBINE_CHUNK, COMBINE_DEPTH) * COMBINE_DEPTH
    return pl.pallas_call(
        functools.partial(_combine_kernel, C=C, final=final),
        grid_spec=pltpu.PrefetchScalarGridSpec(
            num_scalar_prefetch=1,
            grid=(T // tt,),
            in_specs=[pl.BlockSpec((N_EXPERTS, T), lambda j, b: (0, 0)),
                      pl.BlockSpec(memory_space=pl.ANY),
                      pl.BlockSpec((tt, D_MODEL), lambda j, b: (j, 0)),
                      pl.BlockSpec((1, 1, N_MOD * D_MODEL), bidx),
                      pl.BlockSpec((1, D_MODEL), lambda j, b: (0, 0))],
            out_specs=pl.BlockSpec((tt, D_MODEL), lambda j, b: (j, 0)),
            scratch_shapes=[pltpu.VMEM((2, max_rows, D_MODEL), BF16),
                            pltpu.VMEM((2, max_rows, tt), BF16),
                            pltpu.VMEM((tt, D_MODEL), F32),
                            pltpu.SemaphoreType.DMA((2,))]),
        out_shape=jax.ShapeDtypeStruct((T, D_MODEL), F32),
        compiler_params=_cparams("arbitrary"),
        name="combine_final" if final else "combine",
    )(bnd, slot, y, x1, gmod, final_norm.reshape(1, D_MODEL))


def _grid_position_embedding(n_tokens):
    rows = n_tokens // GRID_W
    r, col = jnp.meshgrid(jnp.arange(rows, dtype=F32), jnp.arange(GRID_W, dtype=F32), indexing="ij")
    n_freq = D_MODEL // 4
    omega = 1.0 / (POS_BASE ** (jnp.arange(n_freq, dtype=F32) / n_freq))
    ar = r.reshape(-1)[:, None] * omega
    ac = col.reshape(-1)[:, None] * omega
    return jnp.concatenate([jnp.sin(ar), jnp.cos(ar), jnp.sin(ac), jnp.cos(ac)], axis=-1)


def _mixers_and_routing(x, pos, mod_l, gla_s0, lru_s0, want_state, w, l, B, L):
    T = B * L
    C = CAPACITY_FACTOR * T // N_EXPERTS
    mode = "pos" if pos is not None else "plain"
    x, qk, v, og, uf, ux, ug, lr = _inproj(mode, x, pos, mod_l, w["norm1"][l], w["w_in"], l, B, L)
    o_gla, gs = _gla(qk, v, og, lr, w["gla_w_decay"][l], w["gla_b_decay"][l], w["gla_norm"][l],
                     None if gla_s0 is None else gla_s0[:, l], B, L, want_state)
    o_fft = _fft(uf, B, L)
    o_lru, ls = _lru(ux, ug, w["lru_conv_w"][l], w["lru_conv_b"][l], w["lru_wa"][l], w["lru_ba"][l],
                     w["lru_wx"][l], w["lru_bx"][l], w["lru_lambda"][l],
                     None if lru_s0 is None else lru_s0[:, l], B, L, want_state)
    x1, hx, aff_t = _outproj(o_gla, o_fft, o_lru, x, mod_l, w["norm2"][l], w["w_out"], w["w_router"][l], l, B, L)
    idx, gate, slot, bnd = _expert_choice(aff_t, T, C)
    return dict(x1=x1, sets=(idx, hx, gate), slot=slot, bnd=bnd, gla_state=gs, lru_state=ls, C=C)


def kernel(x_prompt, x_sample, state_gla, state_rglru, c, c_ctx, w_mod, b_mod, norm1, norm2, w_in, gla_w_decay, gla_b_decay, gla_norm, lru_conv_w, lru_conv_b, lru_wa, lru_ba, lru_wx, lru_bx, lru_lambda, w_out, w_router, w_expert_gate, w_expert_up, w_expert_down, final_norm):
    w = dict(norm1=norm1, norm2=norm2, w_in=w_in, gla_w_decay=gla_w_decay, gla_b_decay=gla_b_decay,
             gla_norm=gla_norm, lru_conv_w=lru_conv_w, lru_conv_b=lru_conv_b, lru_wa=lru_wa, lru_ba=lru_ba,
             lru_wx=lru_wx, lru_bx=lru_bx, lru_lambda=lru_lambda, w_out=w_out, w_router=w_router,
             w_expert_gate=w_expert_gate, w_expert_up=w_expert_up, w_expert_down=w_expert_down,
             final_norm=final_norm)
    n_lat = c.shape[0]
    cond = jnp.concatenate([c_ctx[None, :], c, jnp.zeros((SUBLANES - 1 - n_lat, D_MODEL), F32)], axis=0)
    mod = _modulation(cond, w_mod, b_mod)
    mod_ctx = mod[:, 0:1].reshape(DEPTH, 1, 1, N_MOD * D_MODEL)
    mod_lat = mod[:, 1:1 + n_lat].reshape(DEPTH, n_lat, 1, N_MOD * D_MODEL)

    bp, lp, _ = x_prompt.shape
    bs, ls_, _ = x_sample.shape
    xp = x_prompt.reshape(bp * lp, D_MODEL)
    xs = x_sample.reshape(bs * ls_, D_MODEL)
    pos = _grid_position_embedding(ls_)
    gla_states = []
    lru_states = []
    for l in range(DEPTH):
        final = l == DEPTH - 1
        gp = _mixers_and_routing(xp, None, mod_ctx[l], None, None, True, w, l, bp, lp)
        gs = _mixers_and_routing(xs, pos if l == 0 else None, mod_lat[l], state_gla, state_rglru, False, w, l, bs, ls_)
        yp, ys = _expert_ffn((gp["sets"], gs["sets"]), w["w_expert_gate"], w["w_expert_up"], w["w_expert_down"], l)
        xp = _combine(gp["bnd"], gp["slot"], yp, gp["x1"], mod_ctx[l], final_norm, final, bp, lp, gp["C"])
        xs = _combine(gs["bnd"], gs["slot"], ys, gs["x1"], mod_lat[l], final_norm, final, bs, ls_, gs["C"])
        gla_states.append(gp["gla_state"])
        lru_states.append(gp["lru_state"])
    y_prompt = xp.reshape(bp, lp, D_MODEL)
    y_sample = xs.reshape(bs, ls_, D_MODEL)
    new_state_gla = jnp.stack(gla_states, axis=1)
    new_state_rglru = jnp.stack(lru_states, axis=1)
    return (y_prompt, y_sample, new_state_gla, new_state_rglru)
```

```python
import functools
import math

import numpy as np
import jax
import jax.numpy as jnp
from jax import lax
from jax.experimental import pallas as pl
from jax.experimental.pallas import tpu as pltpu

F32 = jnp.float32
BF16 = jnp.bfloat16
I32 = jnp.int32

D_MODEL = 1024
DEPTH = 4
GRID_W = 64
N_MOD = 6
RMS_EPS = 1e-6
POS_BASE = 10000.0
GLA_HEADS = 4
GLA_DK = 64
GLA_DV = 128
GLA_RANK = 16
GLA_GATE_NORM = 16.0
FOURIER_GROUPS = 4
FOURIER_GW = 64
FOURIER_W = FOURIER_GROUPS * FOURIER_GW
LRU_BLOCKS = 4
LRU_BW = 64
LRU_W = LRU_BLOCKS * LRU_BW
LRU_C = 8.0
CONV_W = 4
N_EXPERTS = 16
EXPERT_FF = 1024
CAPACITY_FACTOR = 2
GLA_QK_W = GLA_HEADS * GLA_DK
GLA_V_W = GLA_HEADS * GLA_DV
GLA_LR_W = 2 * GLA_RANK
MIX_W = GLA_V_W + FOURIER_W + LRU_W
IN_W = 2 * GLA_QK_W + 2 * GLA_V_W + GLA_LR_W + FOURIER_W + 2 * LRU_W
_C_OG_END = 2 * GLA_QK_W + 2 * GLA_V_W
_C_LR_END = _C_OG_END + GLA_LR_W

LANES = 128
SUBLANES = 8
VMEM_LIMIT = 56 * 1024 * 1024

TOKEN_TILE = 512
OUT_TILE = 256
GLA_CH = 256
GLA_STEP_TOKENS = 1024
ROW_SUB = D_MODEL // LANES
MOE_ROWS = 256
GATHER_UNROLL = 8
DIGIT = 32
COMBINE_TILE = 256
COMBINE_CHUNK = 64
COMBINE_DEPTH = 512
BF16_ROWS = 16


def _cparams(*sem):
    return pltpu.CompilerParams(dimension_semantics=sem, vmem_limit_bytes=VMEM_LIMIT)


def _dot(a, b):
    return jnp.dot(a, b, preferred_element_type=F32)


def _dot_nt(a, b):
    return lax.dot_general(a, b, (((1,), (1,)), ((), ())), preferred_element_type=F32)


def _dot_tn(a, b):
    return lax.dot_general(a, b, (((0,), (0,)), ((), ())), preferred_element_type=F32)


def _split(x):
    hi = x.astype(BF16)
    lo = (x - hi.astype(F32)).astype(BF16)
    return hi, lo


def _sigmoid(x):
    return 1.0 / (1.0 + jnp.exp(-x))


def _rms(x):
    return x * lax.rsqrt(jnp.mean(x * x, axis=-1, keepdims=True) + RMS_EPS)


def _mod_kernel(cond_ref, w_ref, b_ref, o_ref):
    a = cond_ref[...]
    a = a * _sigmoid(a)
    o_ref[0] = _dot(a.astype(BF16), w_ref[0].astype(BF16)) + b_ref[0]


def _modulation(cond, w_mod, b_mod):
    tn = 1536
    nw = N_MOD * D_MODEL
    return pl.pallas_call(
        _mod_kernel,
        grid=(DEPTH, nw // tn),
        in_specs=[pl.BlockSpec((SUBLANES, D_MODEL), lambda l, j: (0, 0)),
                  pl.BlockSpec((1, D_MODEL, tn), lambda l, j: (l, 0, j)),
                  pl.BlockSpec((1, 1, tn), lambda l, j: (l, 0, j))],
        out_specs=pl.BlockSpec((1, SUBLANES, tn), lambda l, j: (l, 0, j)),
        out_shape=jax.ShapeDtypeStruct((DEPTH, SUBLANES, nw), F32),
        compiler_params=_cparams("arbitrary", "arbitrary"),
        name="modulation",
    )(cond, w_mod, b_mod.reshape(DEPTH, 1, nw))


def _inproj_kernel(*refs, mode):
    if mode == "plain":
        x_ref, mod_ref, n1_ref, w_ref = refs[:4]
        outs = refs[4:]
        x = x_ref[...]
    else:
        x_ref, pos_ref, mod_ref, n1_ref, w_ref, xo_ref = refs[:6]
        outs = refs[6:]
        x = x_ref[...] + pos_ref[...]
        xo_ref[...] = x
    qk_ref, v_ref, og_ref, uf_ref, ux_ref, ug_ref, lr_ref, wsc = outs

    @pl.when(pl.program_id(0) == 0)
    def _():
        for r in range(0, D_MODEL, 256):
            wsc[r:r + 256, 0:_C_OG_END] = w_ref[r:r + 256, 0:_C_OG_END].astype(BF16)
            wsc[r:r + 256, _C_OG_END:_C_OG_END + 768] = w_ref[r:r + 256, _C_LR_END:IN_W].astype(BF16)
            lrw = w_ref[r:r + 256, _C_OG_END:_C_LR_END].astype(BF16)
            wsc[r:r + 256, _C_OG_END + 768:_C_OG_END + 896] = jnp.concatenate(
                [lrw, jnp.zeros((256, LANES - GLA_LR_W), BF16)], axis=1)

    m = mod_ref[0]
    h = _rms(x) * n1_ref[...] * (1.0 + m[:, D_MODEL:2 * D_MODEL]) + m[:, 0:D_MODEL]
    hb = h.astype(BF16)
    qk_ref[...] = _dot(hb, wsc[:, 0:512])
    v_ref[...] = _dot(hb, wsc[:, 512:1024])
    og_ref[...] = _dot(hb, wsc[:, 1024:1536])
    uf_ref[...] = _dot(hb, wsc[:, 1536:1792])
    ux_ref[...] = _dot(hb, wsc[:, 1792:2048])
    ug_ref[...] = _dot(hb, wsc[:, 2048:2304])
    lr_ref[...] = _dot(hb, wsc[:, 2304:2432])


def _inproj(mode, x, extra, mod_l, norm1_l, w_in, l, B, L):
    T = B * L
    tm = TOKEN_TILE
    tpb = L // tm
    per_batch = mod_l.shape[0] > 1
    bidx = (lambda i: (i // tpb, 0, 0)) if per_batch else (lambda i: (0, 0, 0))
    row = lambda i: (i, 0)
    in_specs = [pl.BlockSpec((tm, D_MODEL), row)]
    args = [x]
    if mode == "pos":
        in_specs.append(pl.BlockSpec((tm, D_MODEL), lambda i: (i % tpb, 0)))
        args.append(extra)
    in_specs += [pl.BlockSpec((1, 1, N_MOD * D_MODEL), bidx),
                 pl.BlockSpec((1, D_MODEL), lambda i: (0, 0)),
                 pl.BlockSpec((None, D_MODEL, IN_W), lambda i: (l, 0, 0))]
    args += [mod_l, norm1_l.reshape(1, D_MODEL), w_in]
    widths = [512, 512, 512, 256, 256, 256, LANES]
    out_specs = [pl.BlockSpec((tm, w), row) for w in widths]
    out_shape = [jax.ShapeDtypeStruct((T, w), F32) for w in widths]
    if mode != "plain":
        out_specs = [pl.BlockSpec((tm, D_MODEL), row)] + out_specs
        out_shape = [jax.ShapeDtypeStruct((T, D_MODEL), F32)] + out_shape
    res = pl.pallas_call(
        functools.partial(_inproj_kernel, mode=mode),
        grid=(T // tm,),
        in_specs=in_specs,
        out_specs=out_specs,
        out_shape=out_shape,
        scratch_shapes=[pltpu.VMEM((D_MODEL, 2432), BF16)],
        compiler_params=_cparams("arbitrary"),
        name="inproj_" + mode,
    )(*args)
    if mode == "plain":
        return (x,) + tuple(res)
    return tuple(res)


def _gla_kernel(*refs, L, nb, has_s0, want_state):
    qk_ref, v_ref, og_ref, lr_ref, wdec_ref, bdec_ref, gn_ref = refs[:7]
    p = 7
    s0_ref = None
    if has_s0:
        s0_ref = refs[p]
        p += 1
    o_ref = refs[p]
    p += 1
    sn_ref = None
    if want_state:
        sn_ref = refs[p]
        p += 1
    g_scr, oacc, s_scr = refs[p:p + 3]

    ch = GLA_CH
    n_chunks = L // ch
    kw = GLA_QK_W
    vw = GLA_V_W

    z16 = jnp.zeros((GLA_RANK, kw), F32)
    wc = jnp.concatenate([
        jnp.concatenate([wdec_ref[0], z16], axis=1),
        jnp.concatenate([z16, wdec_ref[1]], axis=1),
        jnp.zeros((LANES - GLA_LR_W, 2 * kw), F32)], axis=0).astype(BF16)
    bias = jnp.concatenate([bdec_ref[0], bdec_ref[1]], axis=1)
    row = lax.broadcasted_iota(I32, (ch, ch), 0)
    col = lax.broadcasted_iota(I32, (ch, ch), 1)
    lane_head = lax.broadcasted_iota(I32, (1, kw), 1) // GLA_DK
    blockdiag = (lax.broadcasted_iota(I32, (kw, vw), 0) // GLA_DK) == (lax.broadcasted_iota(I32, (kw, vw), 1) // GLA_DV)
    ones_t = jnp.ones((ch, LANES), BF16)
    gn = gn_ref[...]

    def one_batch(bi):
        z = _dot(lr_ref[bi].astype(BF16), wc) + bias
        g_scr[...] = (jnp.minimum(z, 0.0) - jnp.log1p(jnp.exp(-jnp.abs(z)))) * (1.0 / GLA_GATE_NORM)

        def finish(o, r0):
            parts = []
            for h in range(GLA_HEADS):
                parts.append(_rms(o[:, h * GLA_DV:(h + 1) * GLA_DV]) * gn)
            ogv = og_ref[bi, r0:r0 + ch, :]
            return (jnp.concatenate(parts, axis=1) * (ogv * _sigmoid(ogv))).astype(BF16)

        for d in range(2):
            allowed = (col <= row) if d == 0 else (col >= row)
            tri = jnp.where(allowed, 1.0, 0.0).astype(BF16)
            if has_s0:
                s_scr[...] = jnp.zeros((kw, vw), F32)
                for h in range(GLA_HEADS):
                    s_scr[h * GLA_DK:(h + 1) * GLA_DK, h * GLA_DV:(h + 1) * GLA_DV] = s0_ref[bi, d, h]
            for i in range(n_chunks):
                n = i if d == 0 else n_chunks - 1 - i
                r0 = n * ch
                state_is_zero = (i == 0) and not has_s0
                gch = g_scr[r0:r0 + ch, d * kw:(d + 1) * kw]
                g_hi, g_lo = _split(gch)
                b = _dot(tri, g_hi) + _dot(tri, g_lo)
                b_last = b[ch - 1:ch, :] if d == 0 else b[0:1, :]
                bc = b - b[ch // 2:ch // 2 + 1, :]
                qch = qk_ref[bi, r0:r0 + ch, 0:kw] * (GLA_DK ** -0.5)
                kch = qk_ref[bi, r0:r0 + ch, kw:2 * kw]
                vb = v_ref[bi, r0:r0 + ch, :].astype(BF16)
                q_s = (qch * jnp.exp(bc)).astype(BF16)
                k_s = (kch * jnp.exp(-bc)).astype(BF16)
                zero_q = jnp.zeros_like(q_s)
                qbig = jnp.concatenate([jnp.where(lane_head == h, q_s, zero_q) for h in range(GLA_HEADS)], axis=0)
                scores = _dot_nt(qbig, k_s)
                parts = []
                for h in range(GLA_HEADS):
                    ph = jnp.where(allowed, scores[h * ch:(h + 1) * ch, :], 0.0).astype(BF16)
                    parts.append(_dot(ph, vb[:, h * GLA_DV:(h + 1) * GLA_DV]))
                o = jnp.concatenate(parts, axis=1)
                if not state_is_zero:
                    q_t = (qch * jnp.exp(b)).astype(BF16)
                    o = o + _dot(q_t, s_scr[...].astype(BF16))
                if (i < n_chunks - 1) or want_state:
                    k_d = (kch * jnp.exp(b_last - b)).astype(BF16)
                    ds = jnp.where(blockdiag, _dot_tn(k_d, vb), 0.0)
                    if state_is_zero:
                        s_scr[...] = ds
                    else:
                        dcol = _dot_tn(g_hi, ones_t) + _dot_tn(g_lo, ones_t)
                        dec = jnp.exp(dcol)
                        s_scr[...] = s_scr[...] * jnp.concatenate([dec] * (vw // LANES), axis=1) + ds
                if d == 0:
                    oacc[r0:r0 + ch, :] = o
                else:
                    o_ref[bi, r0:r0 + ch, :] = finish(o + oacc[r0:r0 + ch, :], r0)
            if want_state:
                for h in range(GLA_HEADS):
                    sn_ref[bi, d, h] = s_scr[h * GLA_DK:(h + 1) * GLA_DK, h * GLA_DV:(h + 1) * GLA_DV]

    if nb == 1:
        one_batch(0)
    else:
        def body(bi, carry):
            one_batch(bi)
            return carry
        lax.fori_loop(0, nb, body, 0)


def _gla(qk, v, og, lr, wdec_l, bdec_l, gn_l, s0, B, L, want_state):
    has_s0 = s0 is not None
    nb = min(B, max(1, GLA_STEP_TOKENS // L))
    blk = lambda w: pl.BlockSpec((nb, L, w), lambda b: (b, 0, 0))
    in_specs = [blk(512), blk(512), blk(512), blk(LANES),
                pl.BlockSpec((2, GLA_RANK, GLA_QK_W), lambda b: (0, 0, 0)),
                pl.BlockSpec((2, 1, GLA_QK_W), lambda b: (0, 0, 0)),
                pl.BlockSpec((1, GLA_DV), lambda b: (0, 0))]
    args = [qk.reshape(B, L, 512), v.reshape(B, L, 512), og.reshape(B, L, 512), lr.reshape(B, L, LANES),
            wdec_l, bdec_l.reshape(2, 1, GLA_QK_W), gn_l.reshape(1, GLA_DV)]
    st_spec = pl.BlockSpec((nb, 2, GLA_HEADS, GLA_DK, GLA_DV), lambda b: (b, 0, 0, 0, 0))
    if has_s0:
        in_specs.append(st_spec)
        args.append(s0)
    out_specs = [pl.BlockSpec((nb, L, GLA_V_W), lambda b: (b, 0, 0))]
    out_shape = [jax.ShapeDtypeStruct((B, L, GLA_V_W), BF16)]
    if want_state:
        out_specs.append(st_spec)
        out_shape.append(jax.ShapeDtypeStruct((B, 2, GLA_HEADS, GLA_DK, GLA_DV), F32))
    res = pl.pallas_call(
        functools.partial(_gla_kernel, L=L, nb=nb, has_s0=has_s0, want_state=want_state),
        grid=(B // nb,),
        in_specs=in_specs,
        out_specs=out_specs,
        out_shape=out_shape,
        scratch_shapes=[pltpu.VMEM((L, 2 * GLA_QK_W), F32),
                        pltpu.VMEM((L, GLA_V_W), F32),
                        pltpu.VMEM((GLA_QK_W, GLA_V_W), F32)],
        compiler_params=_cparams("arbitrary"),
        name="gla",
    )(*args)
    o = res[0].reshape(B * L, GLA_V_W)
    return o, (res[1] if want_state else None)


def _fft_tables(L):
    m = np.arange(L, dtype=np.int64)
    ang = 2.0 * np.pi * ((m[:, None] * m[None, :]) % L) / L
    cc = np.concatenate([np.cos(ang), -np.sin(ang)], axis=1)
    c = np.arange(FOURIER_GW, dtype=np.int64)
    angc = 2.0 * np.pi * ((c[:, None] * c[None, :]) % FOURIER_GW) / FOURIER_GW
    scale = 1.0 / math.sqrt(L * FOURIER_GW)
    eye = np.eye(FOURIER_GROUPS)
    bdc = np.kron(eye, np.cos(angc) * scale)
    bds = np.kron(eye, np.sin(angc) * scale)
    return (jnp.asarray(cc, dtype=F32), jnp.asarray(bdc, dtype=F32), jnp.asarray(bds, dtype=F32))


def _fft_kernel(u_ref, cc_ref, bdc_ref, bds_ref, o_ref):
    bdc = bdc_ref[...].astype(BF16)
    bds = bds_ref[...].astype(BF16)
    cc = cc_ref[...].astype(BF16)
    for bi in range(u_ref.shape[0]):
        u_hi, u_lo = _split(u_ref[bi])
        uc = _dot(u_hi, bdc) + _dot(u_lo, bdc)
        us = _dot(u_hi, bds) + _dot(u_lo, bds)
        w_hi, w_lo = _split(jnp.concatenate([uc, us], axis=0))
        o_ref[bi] = (_dot(cc, w_hi) + _dot(cc, w_lo)).astype(BF16)


def _fft(uf, B, L):
    cc, bdc, bds = _fft_tables(L)
    nb = min(B, max(1, GLA_STEP_TOKENS // L))
    res = pl.pallas_call(
        _fft_kernel,
        grid=(B // nb,),
        in_specs=[pl.BlockSpec((nb, L, FOURIER_W), lambda b: (b, 0, 0)),
                  pl.BlockSpec((L, 2 * L), lambda b: (0, 0)),
                  pl.BlockSpec((FOURIER_W, FOURIER_W), lambda b: (0, 0)),
                  pl.BlockSpec((FOURIER_W, FOURIER_W), lambda b: (0, 0))],
        out_specs=pl.BlockSpec((nb, L, FOURIER_W), lambda b: (b, 0, 0)),
        out_shape=jax.ShapeDtypeStruct((B, L, FOURIER_W), BF16),
        compiler_params=_cparams("arbitrary"),
        name="fourier",
    )(uf.reshape(B, L, FOURIER_W), cc, bdc, bds)
    return res.reshape(B * L, FOURIER_W)


def _lru_kernel(*refs, L, has_s0, want_state):
    ux_ref, ug_ref, cw_ref, cb_ref, wa_ref, ba_ref, wx_ref, bx_ref, lam_ref = refs[:9]
    p = 9
    s0_ref = None
    if has_s0:
        s0_ref = refs[p]
        p += 1
    o_ref = refs[p]
    p += 1
    sn_ref = None
    if want_state:
        sn_ref = refs[p]
        p += 1
    bd_scr = refs[p]
    scan_scr = refs[p + 1:p + 9]

    @pl.when(pl.program_id(0) == 0)
    def _():
        r = lax.broadcasted_iota(I32, (LRU_BW, LRU_W), 0)
        c = lax.broadcasted_iota(I32, (LRU_BW, LRU_W), 1)
        for d in range(2):
            for gi, w_ref in enumerate((wa_ref, wx_ref)):
                pieces = []
                for h in range(LRU_BLOCKS):
                    place = jnp.where(c == r + h * LRU_BW, 1.0, 0.0).astype(BF16)
                    pieces.append(_dot(w_ref[d, h].astype(BF16), place))
                bd_scr[2 * d + gi] = jnp.concatenate(pieces, axis=0).astype(BF16)

    t = lax.broadcasted_iota(I32, (L, 1), 0)
    x = ux_ref[0]
    xm2 = jnp.where(t >= 2, pltpu.roll(x, 2, 0), 0.0)
    xm1 = jnp.where(t >= 1, pltpu.roll(x, 1, 0), 0.0)
    xp1 = jnp.where(t <= L - 2, pltpu.roll(x, L - 1, 0), 0.0)
    xc = xm2 * cw_ref[0:1, :] + xm1 * cw_ref[1:2, :] + x * cw_ref[2:3, :] + xp1 * cw_ref[3:4, :] + cb_ref[...]
    xcb = xc.astype(BF16)

    nb = L // SUBLANES
    pitch = nb + SUBLANES
    n_slab = LRU_W // LANES
    sub = lax.broadcasted_iota(I32, (SUBLANES, LANES), 0)
    hsum = [[None] * n_slab for _ in range(SUBLANES)]
    for d in range(2):
        r = 0.5 + 0.5 * jnp.tanh(0.5 * (_dot(xcb, bd_scr[2 * d]) + ba_ref[d]))
        ig = 0.5 + 0.5 * jnp.tanh(0.5 * (_dot(xcb, bd_scr[2 * d + 1]) + bx_ref[d]))
        lam = lam_ref[d]
        softplus = jnp.maximum(-lam, 0.0) + jnp.log1p(jnp.exp(-jnp.abs(lam)))
        a = jnp.exp(-LRU_C * r * softplus)
        u = jnp.sqrt(1.0 - a * a) * (ig * xc)
        a_scr, u_scr, h_scr, p_scr = scan_scr[4 * d:4 * d + 4]
        for s in range(SUBLANES):
            for k in range(n_slab):
                a_scr[k, s * pitch:s * pitch + nb, :] = a[s * nb:(s + 1) * nb, k * LANES:(k + 1) * LANES]
                u_scr[k, s * pitch:s * pitch + nb, :] = u[s * nb:(s + 1) * nb, k * LANES:(k + 1) * LANES]
        steps = range(nb) if d == 0 else range(nb - 1, -1, -1)
        for k in range(n_slab):
            h = jnp.zeros((SUBLANES, LANES), F32)
            prod = jnp.ones((SUBLANES, LANES), F32)
            for i in steps:
                rows = pl.ds(i, SUBLANES, stride=pitch)
                ai = a_scr[k, rows, :]
                h = ai * h + u_scr[k, rows, :]
                prod = ai * prod
                h_scr[k, rows, :] = h
                p_scr[k, rows, :] = prod
            if has_s0:
                h0 = jnp.broadcast_to(s0_ref[0, d:d + 1, k * LANES:(k + 1) * LANES], (SUBLANES, LANES))
            else:
                h0 = jnp.zeros((SUBLANES, LANES), F32)
            first = 0 if d == 0 else SUBLANES - 1
            carry = jnp.where(sub == first, h0, 0.0)
            for j in range(1, SUBLANES):
                s = j if d == 0 else SUBLANES - 1 - j
                moved = pltpu.roll(prod * carry + h, 1 if d == 0 else SUBLANES - 1, 0)
                carry = jnp.where(sub == s, moved, carry)
            if want_state:
                last = SUBLANES - 1 - first
                sn_ref[0, d:d + 1, k * LANES:(k + 1) * LANES] = (prod * carry + h)[last:last + 1, :]
            for s in range(SUBLANES):
                blk = h_scr[k, s * pitch:s * pitch + nb, :] + p_scr[k, s * pitch:s * pitch + nb, :] * carry[s:s + 1, :]
                hsum[s][k] = blk if hsum[s][k] is None else hsum[s][k] + blk

    for s in range(SUBLANES):
        ugv = ug_ref[0, s * nb:(s + 1) * nb, :]
        gelu = 0.5 * ugv * (1.0 + jnp.tanh(math.sqrt(2.0 / math.pi) * (ugv + 0.044715 * (ugv * ugv * ugv))))
        o_ref[0, s * nb:(s + 1) * nb, :] = (jnp.concatenate(hsum[s], axis=1) * gelu).astype(BF16)


def _lru(ux, ug, cw_l, cb_l, wa_l, ba_l, wx_l, bx_l, lam_l, s0, B, L, want_state):
    has_s0 = s0 is not None
    blk = pl.BlockSpec((1, L, LRU_W), lambda b: (b, 0, 0))
    vec2 = pl.BlockSpec((2, 1, LRU_W), lambda b: (0, 0, 0))
    wsp = pl.BlockSpec((2, LRU_BLOCKS, LRU_BW, LRU_BW), lambda b: (0, 0, 0, 0))
    in_specs = [blk, blk,
                pl.BlockSpec((CONV_W, LRU_W), lambda b: (0, 0)),
                pl.BlockSpec((1, LRU_W), lambda b: (0, 0)),
                wsp, vec2, wsp, vec2, vec2]
    args = [ux.reshape(B, L, LRU_W), ug.reshape(B, L, LRU_W), cw_l, cb_l.reshape(1, LRU_W),
            wa_l, ba_l.reshape(2, 1, LRU_W), wx_l, bx_l.reshape(2, 1, LRU_W), lam_l.reshape(2, 1, LRU_W)]
    st_spec = pl.BlockSpec((1, 2, LRU_W), lambda b: (b, 0, 0))
    if has_s0:
        in_specs.append(st_spec)
        args.append(s0)
    out_specs = [blk]
    out_shape = [jax.ShapeDtypeStruct((B, L, LRU_W), BF16)]
    if want_state:
        out_specs.append(st_spec)
        out_shape.append(jax.ShapeDtypeStruct((B, 2, LRU_W), F32))
    res = pl.pallas_call(
        functools.partial(_lru_kernel, L=L, has_s0=has_s0, want_state=want_state),
        grid=(B,),
        in_specs=in_specs,
        out_specs=out_specs,
        out_shape=out_shape,
        scratch_shapes=[pltpu.VMEM((4, LRU_W, LRU_W), BF16)]
        + [pltpu.VMEM((LRU_W // LANES, L + SUBLANES * SUBLANES, LANES), F32)] * 8,
        compiler_params=_cparams("arbitrary"),
        name="rglru",
    )(*args)
    return res[0].reshape(B * L, LRU_W), (res[1] if want_state else None)


def _outproj_kernel(og_ref, of_ref, ol_ref, x_ref, mod_ref, n2_ref, wout_ref, wr_ref,
                    x1_ref, hx_ref, afft_ref, wsc, wrs):
    @pl.when(pl.program_id(0) == 0)
    def _():
        for r in range(0, MIX_W, 256):
            wsc[r:r + 256, :] = wout_ref[r:r + 256, :].astype(BF16)
        wrs[...] = jnp.concatenate([wr_ref[...], jnp.zeros((D_MODEL, LANES - N_EXPERTS), F32)], axis=1)

    m = mod_ref[0]
    y = (_dot(og_ref[...], wsc[0:GLA_V_W, :])
         + _dot(of_ref[...], wsc[GLA_V_W:GLA_V_W + FOURIER_W, :])
         + _dot(ol_ref[...], wsc[GLA_V_W + FOURIER_W:MIX_W, :]))
    x1 = x_ref[...] + m[:, 2 * D_MODEL:3 * D_MODEL] * y
    x1_ref[...] = x1
    h2 = _rms(x1) * n2_ref[...] * (1.0 + m[:, 4 * D_MODEL:5 * D_MODEL]) + m[:, 3 * D_MODEL:4 * D_MODEL]
    h_hi = h2.astype(BF16)
    h_hi32 = h_hi.astype(F32)
    tm = h2.shape[0]
    for s in range(ROW_SUB):
        hx_ref[pl.ds(s, tm, stride=ROW_SUB), :] = h_hi32[:, s * LANES:(s + 1) * LANES]
    h_lo = (h2 - h_hi32).astype(BF16)
    w_hi, w_lo = _split(wrs[...])
    logits = _dot(h_hi, w_hi) + _dot(h_lo, w_hi) + _dot(h_hi, w_lo)
    lane = lax.broadcasted_iota(I32, logits.shape, 1)
    logits = jnp.where(lane < N_EXPERTS, logits, -jnp.inf)
    ex = jnp.exp(logits - jnp.max(logits, axis=-1, keepdims=True))
    aff = ex / jnp.sum(ex, axis=-1, keepdims=True)
    afft_ref[...] = aff.T[0:N_EXPERTS, :]


def _outproj(o_gla, o_fft, o_lru, x, mod_l, norm2_l, w_out, w_router_l, l, B, L):
    T = B * L
    tm = OUT_TILE
    tpb = L // tm
    per_batch = mod_l.shape[0] > 1
    bidx = (lambda i: (i // tpb, 0, 0)) if per_batch else (lambda i: (0, 0, 0))
    row = lambda i: (i, 0)
    return pl.pallas_call(
        _outproj_kernel,
        grid=(T // tm,),
        in_specs=[pl.BlockSpec((tm, GLA_V_W), row), pl.BlockSpec((tm, FOURIER_W), row),
                  pl.BlockSpec((tm, LRU_W), row), pl.BlockSpec((tm, D_MODEL), row),
                  pl.BlockSpec((1, 1, N_MOD * D_MODEL), bidx),
                  pl.BlockSpec((1, D_MODEL), lambda i: (0, 0)),
                  pl.BlockSpec((None, MIX_W, D_MODEL), lambda i: (l, 0, 0)),
                  pl.BlockSpec((D_MODEL, N_EXPERTS), lambda i: (0, 0))],
        out_specs=[pl.BlockSpec((tm, D_MODEL), row),
                   pl.BlockSpec((tm * ROW_SUB, LANES), lambda i: (i, 0)),
                   pl.BlockSpec((N_EXPERTS, tm), lambda i: (0, i))],
        out_shape=[jax.ShapeDtypeStruct((T, D_MODEL), F32),
                   jax.ShapeDtypeStruct((T * ROW_SUB, LANES), F32),
                   jax.ShapeDtypeStruct((N_EXPERTS, T), F32)],
        scratch_shapes=[pltpu.VMEM((MIX_W, D_MODEL), BF16), pltpu.VMEM((D_MODEL, LANES), F32)],
        compiler_params=_cparams("arbitrary"),
        name="outproj",
    )(o_gla, o_fft, o_lru, x, mod_l, norm2_l.reshape(1, D_MODEL), w_out, w_router_l)


def _prefix_lanes(x):
    T = x.shape[1]
    w = 256
    nb = T // w
    stacked = jnp.concatenate([x[:, j * w:(j + 1) * w] for j in range(nb)], axis=0)
    upper = jnp.where(lax.broadcasted_iota(I32, (w, w), 0) <= lax.broadcasted_iota(I32, (w, w), 1), 1.0, 0.0)
    pe = _dot(stacked, upper.astype(BF16))
    carry = jnp.zeros((N_EXPERTS, 1), F32)
    outs = []
    for j in range(nb):
        blk = pe[j * N_EXPERTS:(j + 1) * N_EXPERTS, :]
        outs.append(blk + carry)
        carry = carry + blk[:, w - 1:w]
    return jnp.concatenate(outs, axis=1)


def _topk_kernel(aff_ref, out_ref, gate_ref, slot_ref, bnd_ref, *, T, C):
    n_a = C // DIGIT
    aff = aff_ref[...]
    bits = jnp.zeros((N_EXPERTS, 1), I32)
    for bit in range(30, -1, -1):
        cand = bits | (1 << bit)
        cnt = jnp.sum(jnp.where(aff >= pltpu.bitcast(cand, F32), 1.0, 0.0), axis=1, keepdims=True)
        bits = jnp.where(cnt >= C, cand, bits)
    thr = pltpu.bitcast(bits, F32)
    gt = aff > thr
    eq = aff == thr
    eqf = jnp.where(eq, 1.0, 0.0)
    need = C - jnp.sum(jnp.where(gt, 1.0, 0.0), axis=1, keepdims=True)
    eq_before = _prefix_lanes(eqf.astype(BF16)) - eqf
    sel = gt | (eq & (eq_before < need))
    self32 = jnp.where(sel, 1.0, 0.0)
    cnt = _prefix_lanes(self32.astype(BF16))
    slot_ref[...] = jnp.where(sel, cnt - 1.0, -1.0)
    tok = lax.broadcasted_iota(I32, (1, T), 1)
    lane = lax.broadcasted_iota(I32, (1, LANES), 1)
    bnd = jnp.zeros((N_EXPERTS, LANES), F32)
    for j in range(1, T // COMBINE_TILE + 1):
        before = jnp.sum(jnp.where(tok < j * COMBINE_TILE, self32, 0.0), axis=1, keepdims=True)
        bnd = jnp.where(lane == j, before, bnd)
    bnd_ref[...] = bnd

    p_dig = jnp.floor(cnt * (1.0 / DIGIT))
    q_dig = cnt - DIGIT * p_dig
    a_col = lax.broadcasted_iota(I32, (n_a, 1), 0).astype(F32)
    b_col = lax.broadcasted_iota(I32, (DIGIT, 1), 0).astype(F32)
    slot = cnt - 1.0
    ps_dig = jnp.where(sel, jnp.floor(slot * (1.0 / DIGIT)), -1.0)
    qs_dig = slot - DIGIT * jnp.floor(slot * (1.0 / DIGIT))
    aff_hi = aff.astype(BF16).astype(F32)
    aff_lo = aff - aff_hi
    kc = min(T, 2048)
    acc = jnp.zeros((N_EXPERTS * n_a, N_EXPERTS * DIGIT), F32)
    gacc = jnp.zeros((N_EXPERTS * n_a, N_EXPERTS * DIGIT), F32)
    for c0 in range(0, T, kc):
        tk = slice(c0, c0 + kc)
        u = jnp.concatenate([jnp.where(p_dig[e:e + 1, tk] == a_col, 1.0, 0.0).astype(BF16)
                             for e in range(N_EXPERTS)], axis=0)
        v = jnp.concatenate([jnp.where(q_dig[e:e + 1, tk] <= b_col, 1.0, 0.0).astype(BF16)
                             for e in range(N_EXPERTS)], axis=0)
        acc = acc + _dot_nt(u, v)
        us = jnp.concatenate([jnp.where(ps_dig[e:e + 1, tk] == a_col, 1.0, 0.0).astype(BF16)
                              for e in range(N_EXPERTS)], axis=0)
        for part in (aff_hi, aff_lo):
            vs = jnp.concatenate([jnp.where(qs_dig[e:e + 1, tk] == b_col, part[e:e + 1, tk], 0.0).astype(BF16)
                                  for e in range(N_EXPERTS)], axis=0)
            gacc = gacc + _dot_nt(us, vs)
    below = jnp.concatenate([jnp.sum(jnp.where(p_dig[e:e + 1, :] < a_col, 1.0, 0.0), axis=1, keepdims=True)
                             for e in range(N_EXPERTS)], axis=0)
    r_i = lax.broadcasted_iota(I32, acc.shape, 0) // n_a
    c_i = lax.broadcasted_iota(I32, acc.shape, 1) // DIGIT

    def own_block(x):
        x = jnp.where(r_i == c_i, x, 0.0)
        x = x[:, 0:256] + x[:, 256:512]
        x = x[:, 0:LANES] + x[:, LANES:2 * LANES]
        x = x + pltpu.roll(x, 64, 1)
        return x + pltpu.roll(x, 32, 1)

    out_ref[...] = own_block(acc) + below
    gate_ref[...] = own_block(gacc)


def _expert_choice(aff_t, T, C):
    n_a = C // DIGIT
    res = pl.pallas_call(
        functools.partial(_topk_kernel, T=T, C=C),
        grid=(1,),
        in_specs=[pl.BlockSpec((N_EXPERTS, T), lambda i: (0, 0))],
        out_specs=[pl.BlockSpec((N_EXPERTS * n_a, LANES), lambda i: (0, 0)),
                   pl.BlockSpec((N_EXPERTS * n_a, LANES), lambda i: (0, 0)),
                   pl.BlockSpec((N_EXPERTS, T), lambda i: (0, 0)),
                   pl.BlockSpec((N_EXPERTS, LANES), lambda i: (0, 0))],
        out_shape=[jax.ShapeDtypeStruct((N_EXPERTS * n_a, LANES), F32),
                   jax.ShapeDtypeStruct((N_EXPERTS * n_a, LANES), F32),
                   jax.ShapeDtypeStruct((N_EXPERTS, T), F32),
                   jax.ShapeDtypeStruct((N_EXPERTS, LANES), F32)],
        compiler_params=_cparams("arbitrary"),
        name="expert_choice",
    )(aff_t)
    idx = res[0][:, 0:DIGIT].astype(I32).reshape(N_EXPERTS, C)
    gate = res[1][:, 0:DIGIT].reshape(N_EXPERTS, 1, C)
    bnd = res[3][:, 0:T // COMBINE_TILE + 1].astype(I32)
    return idx, gate, res[2], bnd


def _ffn_kernel(*refs, caps):
    n = len(caps)
    idx_refs, hx_refs, gate_refs = refs[0:n], refs[n:2 * n], refs[2 * n:3 * n]
    wg_ref, wu_ref, wd_ref = refs[3 * n:3 * n + 3]
    y_refs = refs[3 * n + 3:4 * n + 3]
    xbufs = refs[4 * n + 3:5 * n + 3]
    wgb, wub, wdb, sems = refs[5 * n + 3:5 * n + 7]
    e = pl.program_id(0)
    last = pl.num_programs(0) - 1
    buf = e % 2
    nxt = 1 - buf
    following = jnp.where(e == last, 0, e + 1)

    def fetch_row(q, expert, c, into, k):
        src = hx_refs[q].at[pl.ds(pl.multiple_of(idx_refs[q][expert, c] * ROW_SUB, ROW_SUB), ROW_SUB)]
        dst = xbufs[q].at[into, pl.ds(pl.multiple_of(c * ROW_SUB, ROW_SUB), ROW_SUB)]
        del k
        pltpu.make_async_copy(src, dst, sems.at[2 * q + into]).start(priority=1)

    def wait_rows(q, which):
        pltpu.make_async_copy(hx_refs[q].at[pl.ds(0, caps[q] * ROW_SUB)], xbufs[q].at[which],
                              sems.at[2 * q + which]).wait()

    @pl.when(e == 0)
    def _():
        for q in range(n):
            def body(i, carry, q=q):
                for k in range(GATHER_UNROLL):
                    fetch_row(q, 0, i * GATHER_UNROLL + k, 0, k)
                return carry
            lax.fori_loop(0, caps[q] // GATHER_UNROLL, body, 0)

    for r in range(0, D_MODEL, 256):
        wgb[r:r + 256, :] = wg_ref[r:r + 256, :].astype(BF16)
        wub[r:r + 256, :] = wu_ref[r:r + 256, :].astype(BF16)
        wdb[r:r + 256, :] = wd_ref[r:r + 256, :].astype(BF16)

    for q in range(n):
        wait_rows(q, buf)
        step = min(MOE_ROWS, caps[q])
        for r0 in range(0, caps[q], step):
            for k in range(step):
                fetch_row(q, following, r0 + k, nxt, k)
            x = jnp.concatenate([xbufs[q][buf, pl.ds(r0 * ROW_SUB + s, step, stride=ROW_SUB), :]
                                 for s in range(ROW_SUB)], axis=1).astype(BF16)
            g = _dot(x, wgb[...])
            u = _dot(x, wub[...])
            hid = (g * _sigmoid(g) * u).astype(BF16)
            gate = jnp.broadcast_to(gate_refs[q][:, r0:r0 + step], (LANES, step)).T[:, 0:1]
            y_refs[q][pl.ds(r0, step), :] = (_dot(hid, wdb[...]) * gate).astype(BF16)

    @pl.when(e == last)
    def _():
        for q in range(n):
            wait_rows(q, nxt)


def _expert_ffn(sets, wg, wu, wd, l):
    n = len(sets)
    caps = tuple(s[0].shape[1] for s in sets)
    imap = lambda e, *idx_refs: (l, e, 0, 0)
    wspec = pl.BlockSpec((None, None, D_MODEL, EXPERT_FF), imap)
    emap = lambda e, *idx_refs: (e, 0, 0)
    return pl.pallas_call(
        functools.partial(_ffn_kernel, caps=caps),
        grid_spec=pltpu.PrefetchScalarGridSpec(
            num_scalar_prefetch=n,
            grid=(N_EXPERTS,),
            in_specs=[pl.BlockSpec(memory_space=pl.ANY)] * n
            + [pl.BlockSpec((None, 1, c), emap) for c in caps]
            + [wspec, wspec, pl.BlockSpec((None, None, EXPERT_FF, D_MODEL), imap)],
            out_specs=[pl.BlockSpec((None, c, D_MODEL), emap) for c in caps],
            scratch_shapes=[pltpu.VMEM((2, c * ROW_SUB, LANES), F32) for c in caps]
            + [pltpu.VMEM((D_MODEL, EXPERT_FF), BF16),
               pltpu.VMEM((D_MODEL, EXPERT_FF), BF16),
               pltpu.VMEM((EXPERT_FF, D_MODEL), BF16),
               pltpu.SemaphoreType.DMA((2 * n,))]),
        out_shape=[jax.ShapeDtypeStruct((N_EXPERTS, c, D_MODEL), BF16) for c in caps],
        compiler_params=pltpu.CompilerParams(dimension_semantics=("arbitrary",), vmem_limit_bytes=VMEM_LIMIT,
                                             disable_bounds_checks=True),
        name="expert_ffn",
    )(*[s[0] for s in sets], *[s[1] for s in sets], *[s[2] for s in sets], wg, wu, wd)


def _combine_kernel(bnd_ref, slot_ref, y_hbm, x1_ref, gmod_ref, fn_ref, o_ref, ybuf, onehot, acc, sems, *, C, final):
    j = pl.program_id(0)
    cur = j % 2
    ch = COMBINE_CHUNK
    group = COMBINE_DEPTH // ch
    tt = x1_ref.shape[0]
    w_col = lax.broadcasted_iota(I32, (ch, 1), 0)

    def chunks_of(tile, e):
        first = bnd_ref[e, tile] & (-BF16_ROWS)
        return first, lax.div(bnd_ref[e, tile + 1] - first + (ch - 1), jnp.int32(ch))

    def padded(total):
        return lax.div(total + (group - 1), jnp.int32(group)) * group

    def stage(tile, into):
        tok0 = pl.multiple_of(tile * tt, tt)
        total = jnp.int32(0)
        for e in range(N_EXPERTS):
            first, n_chunks = chunks_of(tile, e)

            def fetch_chunk(c, carry, e=e, first=first, base=total):
                want_lo = first + c * ch
                src_row = pl.multiple_of(jnp.minimum(want_lo, C - ch), BF16_ROWS)
                dst_row = pl.multiple_of((base + c) * ch, ch)
                pltpu.make_async_copy(y_hbm.at[e, pl.ds(src_row, ch)], ybuf.at[into, pl.ds(dst_row, ch)],
                                      sems.at[into]).start()
                row = src_row + w_col
                want = jnp.where(row >= want_lo, row, -2).astype(F32)
                hit = slot_ref[e:e + 1, pl.ds(tok0, tt)] == want
                onehot[into, pl.ds(dst_row, ch), :] = jnp.where(hit, 1.0, 0.0).astype(BF16)
                return carry

            lax.fori_loop(0, n_chunks, fetch_chunk, 0)
            total = total + n_chunks

        def clear_chunk(k, carry):
            rows = pl.ds(pl.multiple_of(k * ch, ch), ch)
            onehot[into, rows, :] = jnp.zeros((ch, tt), BF16)
            ybuf[into, rows, :] = jnp.zeros((ch, D_MODEL), BF16)
            return carry

        lax.fori_loop(total, padded(total), clear_chunk, 0)

    @pl.when(j == 0)
    def _():
        stage(0, 0)

    @pl.when(j + 1 < pl.num_programs(0))
    def _():
        stage(j + 1, 1 - cur)

    total = jnp.int32(0)
    for e in range(N_EXPERTS):
        total = total + chunks_of(j, e)[1]

    def wait_chunk(k, carry):
        pltpu.make_async_copy(y_hbm.at[0, pl.ds(0, ch)], ybuf.at[cur, pl.ds(0, ch)], sems.at[cur]).wait()
        return carry

    lax.fori_loop(0, total, wait_chunk, 0)
    acc[...] = jnp.zeros(acc.shape, F32)

    def add_group(g, carry):
        rows = pl.ds(pl.multiple_of(g * COMBINE_DEPTH, COMBINE_DEPTH), COMBINE_DEPTH)
        acc[...] += _dot_tn(onehot[cur, rows, :], ybuf[cur, rows, :])
        return carry

    lax.fori_loop(0, lax.div(padded(total), jnp.int32(group)), add_group, 0)
    x = x1_ref[...] + gmod_ref[0][:, 5 * D_MODEL:6 * D_MODEL] * acc[...]
    o_ref[...] = _rms(x) * fn_ref[...] if final else x


def _combine(bnd, slot, y, x1, gmod, final_norm, final, B, L, C):
    T = B * L
    tt = COMBINE_TILE
    tpb = L // tt if L >= tt else None
    per_batch = gmod.shape[0] > 1
    if per_batch:
        bidx = lambda j, b: (j // tpb, 0, 0)
    else:
        bidx = lambda j, b: (0, 0, 0)
    max_chunks = N_EXPERTS * (tt // COMBINE_CHUNK + 2)
    max_rows = pl.cdiv(max_chunks * COMBINE_CHUNK, COMBINE_DEPTH) * COMBINE_DEPTH
    return pl.pallas_call(
        functools.partial(_combine_kernel, C=C, final=final),
        grid_spec=pltpu.PrefetchScalarGridSpec(
            num_scalar_prefetch=1,
            grid=(T // tt,),
            in_specs=[pl.BlockSpec((N_EXPERTS, T), lambda j, b: (0, 0)),
                      pl.BlockSpec(memory_space=pl.ANY),
                      pl.BlockSpec((tt, D_MODEL), lambda j, b: (j, 0)),
                      pl.BlockSpec((1, 1, N_MOD * D_MODEL), bidx),
                      pl.BlockSpec((1, D_MODEL), lambda j, b: (0, 0))],
            out_specs=pl.BlockSpec((tt, D_MODEL), lambda j, b: (j, 0)),
            scratch_shapes=[pltpu.VMEM((2, max_rows, D_MODEL), BF16),
                            pltpu.VMEM((2, max_rows, tt), BF16),
                            pltpu.VMEM((tt, D_MODEL), F32),
                            pltpu.SemaphoreType.DMA((2,))]),
        out_shape=jax.ShapeDtypeStruct((T, D_MODEL), F32),
        compiler_params=_cparams("arbitrary"),
        name="combine_final" if final else "combine",
    )(bnd, slot, y, x1, gmod, final_norm.reshape(1, D_MODEL))


def _grid_position_embedding(n_tokens):
    rows = n_tokens // GRID_W
    r, col = jnp.meshgrid(jnp.arange(rows, dtype=F32), jnp.arange(GRID_W, dtype=F32), indexing="ij")
    n_freq = D_MODEL // 4
    omega = 1.0 / (POS_BASE ** (jnp.arange(n_freq, dtype=F32) / n_freq))
    ar = r.reshape(-1)[:, None] * omega
    ac = col.reshape(-1)[:, None] * omega
    return jnp.concatenate([jnp.sin(ar), jnp.cos(ar), jnp.sin(ac), jnp.cos(ac)], axis=-1)


def _mixers_and_routing(x, pos, mod_l, gla_s0, lru_s0, want_state, w, l, B, L):
    T = B * L
    C = CAPACITY_FACTOR * T // N_EXPERTS
    mode = "pos" if pos is not None else "plain"
    x, qk, v, og, uf, ux, ug, lr = _inproj(mode, x, pos, mod_l, w["norm1"][l], w["w_in"], l, B, L)
    o_gla, gs = _gla(qk, v, og, lr, w["gla_w_decay"][l], w["gla_b_decay"][l], w["gla_norm"][l],
                     None if gla_s0 is None else gla_s0[:, l], B, L, want_state)
    o_fft = _fft(uf, B, L)
    o_lru, ls = _lru(ux, ug, w["lru_conv_w"][l], w["lru_conv_b"][l], w["lru_wa"][l], w["lru_ba"][l],
                     w["lru_wx"][l], w["lru_bx"][l], w["lru_lambda"][l],
                     None if lru_s0 is None else lru_s0[:, l], B, L, want_state)
    x1, hx, aff_t = _outproj(o_gla, o_fft, o_lru, x, mod_l, w["norm2"][l], w["w_out"], w["w_router"][l], l, B, L)
    idx, gate, slot, bnd = _expert_choice(aff_t, T, C)
    return dict(x1=x1, sets=(idx, hx, gate), slot=slot, bnd=bnd, gla_state=gs, lru_state=ls, C=C)


def kernel(x_prompt, x_sample, state_gla, state_rglru, c, c_ctx, w_mod, b_mod, norm1, norm2, w_in, gla_w_decay, gla_b_decay, gla_norm, lru_conv_w, lru_conv_b, lru_wa, lru_ba, lru_wx, lru_bx, lru_lambda, w_out, w_router, w_expert_gate, w_expert_up, w_expert_down, final_norm):
    w = dict(norm1=norm1, norm2=norm2, w_in=w_in, gla_w_decay=gla_w_decay, gla_b_decay=gla_b_decay,
             gla_norm=gla_norm, lru_conv_w=lru_conv_w, lru_conv_b=lru_conv_b, lru_wa=lru_wa, lru_ba=lru_ba,
             lru_wx=lru_wx, lru_bx=lru_bx, lru_lambda=lru_lambda, w_out=w_out, w_router=w_router,
             w_expert_gate=w_expert_gate, w_expert_up=w_expert_up, w_expert_down=w_expert_down,
             final_norm=final_norm)
    n_lat = c.shape[0]
    cond = jnp.concatenate([c_ctx[None, :], c, jnp.zeros((SUBLANES - 1 - n_lat, D_MODEL), F32)], axis=0)
    mod = _modulation(cond, w_mod, b_mod)
    mod_ctx = mod[:, 0:1].reshape(DEPTH, 1, 1, N_MOD * D_MODEL)
    mod_lat = mod[:, 1:1 + n_lat].reshape(DEPTH, n_lat, 1, N_MOD * D_MODEL)

    bp, lp, _ = x_prompt.shape
    bs, ls_, _ = x_sample.shape
    xp = x_prompt.reshape(bp * lp, D_MODEL)
    xs = x_sample.reshape(bs * ls_, D_MODEL)
    pos = _grid_position_embedding(ls_)
    gla_states = []
    lru_states = []
    for l in range(DEPTH):
        final = l == DEPTH - 1
        gp = _mixers_and_routing(xp, None, mod_ctx[l], None, None, True, w, l, bp, lp)
        gs = _mixers_and_routing(xs, pos if l == 0 else None, mod_lat[l], state_gla, state_rglru, False, w, l, bs, ls_)
        yp, ys = _expert_ffn((gp["sets"], gs["sets"]), w["w_expert_gate"], w["w_expert_up"], w["w_expert_down"], l)
        xp = _combine(gp["bnd"], gp["slot"], yp, gp["x1"], mod_ctx[l], final_norm, final, bp, lp, gp["C"])
        xs = _combine(gs["bnd"], gs["slot"], ys, gs["x1"], mod_lat[l], final_norm, final, bs, ls_, gs["C"])
        gla_states.append(gp["gla_state"])
        lru_states.append(gp["lru_state"])
    y_prompt = xp.reshape(bp, lp, D_MODEL)
    y_sample = xs.reshape(bs, ls_, D_MODEL)
    new_state_gla = jnp.stack(gla_states, axis=1)
    new_state_rglru = jnp.stack(lru_states, axis=1)
    return (y_prompt, y_sample, new_state_gla, new_state_rglru)
```
